```python
import math
import jax, jax.numpy as jnp
from jax import lax
import numpy as np

D_MODEL = 1024
BATCH = 2
SEQ = 8192
DEPTH = 1

SSD_HEADS = 16
SSD_HEAD_DIM = 64
SSD_INNER = SSD_HEADS * SSD_HEAD_DIM
SSD_GROUPS = 2
SSD_STATE = 128
SSD_CONV = 4
SSD_CHUNK = 128
SSD_XBC = SSD_INNER + 2 * SSD_GROUPS * SSD_STATE
ATTN_HEADS = 8
ATTN_QK_DIM = 64
ATTN_V_DIM = 2 * ATTN_QK_DIM
ATTN_QK_WIDTH = ATTN_HEADS * 2 * ATTN_QK_DIM
ATTN_INNER = ATTN_HEADS * ATTN_V_DIM
Q_BLOCK = 128
MIX_WIDTH = SSD_INNER + ATTN_INNER
IN_WIDTH = SSD_INNER + SSD_XBC + SSD_HEADS + 2 * ATTN_QK_WIDTH + ATTN_INNER
REL_BUCKETS = 32
REL_MAX_DIST = 128
D_FF = 2816
N_MOD = 9
EPS = 1e-6

kernel_name = "hymba_ssd_diffattn_macaron_layer"


def rmsnorm(x, g):
    xf = x.astype(jnp.float32)
    y = xf * lax.rsqrt(jnp.mean(xf * xf, axis=-1, keepdims=True) + EPS)
    return (y * g.astype(jnp.float32)).astype(x.dtype)


def swiglu(h, w_gate, w_up, w_down):
    return (jax.nn.silu(h @ w_gate) * (h @ w_up)) @ w_down


def causal_dwconv(u, w, b):
    k_w, ch = w.shape
    out = lax.conv_general_dilated(
        u, w[:, None, :].astype(u.dtype), window_strides=(1,), padding=[(k_w - 1, 0)],
        dimension_numbers=('NWC', 'WIO', 'NWC'), feature_group_count=ch)
    return out + b


def ssd_chunked(xs, dt, a, bm, cm):
    bsz, s, h, p = xs.shape
    g, n = bm.shape[2], bm.shape[3]
    e = h // g
    q = SSD_CHUNK
    nc = s // q
    x = xs.reshape(bsz, nc, q, g, e, p)
    dtc = dt.reshape(bsz, nc, q, g, e)
    bc = bm.reshape(bsz, nc, q, g, n)
    cc = cm.reshape(bsz, nc, q, g, n)
    a_cum = jnp.cumsum(dtc * a.reshape(g, e), axis=2)
    seg = a_cum[:, :, :, None] - a_cum[:, :, None, :]
    causal = jnp.tril(jnp.ones((q, q), dtype=bool))[:, :, None, None]
    decay_in = jnp.exp(jnp.where(causal, seg, -jnp.inf))
    xdt = x * dtc[..., None]
    cb = jnp.einsum('bcign,bcjgn->bcijg', cc, bc)
    y_diag = jnp.einsum('bcijg,bcijge,bcjgep->bcigep', cb, decay_in, xdt)
    decay_to_end = jnp.exp(a_cum[:, :, -1:] - a_cum)
    states = jnp.einsum('bcjgn,bcjge,bcjgep->bcgepn', bc, decay_to_end, xdt)
    chunk_decay = jnp.exp(a_cum[:, :, -1])

    def step(hs, inp):
        st, dec = inp
        return hs * dec[..., None, None] + st, hs

    h0 = jnp.zeros((bsz, g, e, p, n), states.dtype)
    _, h_in = lax.scan(step, h0, (jnp.moveaxis(states, 1, 0), jnp.moveaxis(chunk_decay, 1, 0)))
    h_in = jnp.moveaxis(h_in, 0, 1)
    y_off = jnp.einsum('bcign,bcgepn,bcige->bcigep', cc, h_in, jnp.exp(a_cum))
    return (y_diag + y_off).reshape(bsz, s, h, p)


def t5_bucket(rel):
    n = jnp.maximum(rel, 0)
    max_exact = REL_BUCKETS // 2
    nf = jnp.maximum(n, 1).astype(jnp.float32)
    large = max_exact + (jnp.log(nf / max_exact) / math.log(REL_MAX_DIST / max_exact)
                         * (REL_BUCKETS - max_exact)).astype(jnp.int32)
    large = jnp.minimum(large, REL_BUCKETS - 1)
    return jnp.where(n < max_exact, n, large)


def diff_attention(q1, q2, k1, k2, v, lam, rel_bias):
    bsz, h, s, dq = q1.shape
    dv = v.shape[-1]
    nb = s // Q_BLOCK
    scale = dq ** -0.5
    kpos = jnp.arange(s)

    def blocks(t):
        return jnp.moveaxis(t.reshape(bsz, h, nb, Q_BLOCK, dq), 2, 0)

    def one_block(args):
        qb1, qb2, start = args
        qpos = start + jnp.arange(Q_BLOCK)
        rel = qpos[:, None] - kpos[None, :]
        bias = jnp.transpose(rel_bias[t5_bucket(rel)], (2, 0, 1)).astype(jnp.float32)
        mask = rel >= 0

        def probs(qb, kk):
            logits = jnp.einsum('bhqd,bhkd->bhqk', qb, kk).astype(jnp.float32) * scale + bias
            return jax.nn.softmax(jnp.where(mask, logits, -jnp.inf), axis=-1)

        attn = probs(qb1, k1) - lam * probs(qb2, k2)
        return jnp.einsum('bhqk,bhkv->bhqv', attn.astype(v.dtype), v)

    starts = jnp.arange(nb) * Q_BLOCK
    out = lax.map(one_block, (blocks(q1), blocks(q2), starts))
    return jnp.moveaxis(out, 0, 2).reshape(bsz, h, s, dv)


def token_mix(h, layer_idx, w_in, conv_w, conv_b, dt_bias, a_log, d_skip, ssd_norm_g,
              lambda_q1, lambda_k1, lambda_q2, lambda_k2, subln_g, rel_bias, w_out):
    bsz, s, _ = h.shape
    sizes = [SSD_INNER, SSD_XBC, SSD_HEADS, ATTN_QK_WIDTH, ATTN_QK_WIDTH, ATTN_INNER]
    splits = [int(v) for v in np.cumsum(sizes)[:-1]]
    proj = h @ w_in
    z, xbc, dt_raw, q, k, v = jnp.split(proj, splits, axis=-1)

    xbc = jax.nn.silu(causal_dwconv(xbc, conv_w, conv_b))
    xs, bm, cm = jnp.split(xbc, [SSD_INNER, SSD_INNER + SSD_GROUPS * SSD_STATE], axis=-1)
    xs4 = xs.reshape(bsz, s, SSD_HEADS, SSD_HEAD_DIM)
    dt = jax.nn.softplus(dt_raw.astype(jnp.float32) + dt_bias.astype(jnp.float32))
    a = -jnp.exp(a_log.astype(jnp.float32))
    y = ssd_chunked(xs4, dt, a,
                    bm.reshape(bsz, s, SSD_GROUPS, SSD_STATE),
                    cm.reshape(bsz, s, SSD_GROUPS, SSD_STATE))
    y = y + d_skip[:, None] * xs4
    y = y.reshape(bsz, s, SSD_INNER) * jax.nn.silu(z)
    y = rmsnorm(y.reshape(bsz, s, SSD_GROUPS, SSD_INNER // SSD_GROUPS),
                ssd_norm_g.reshape(SSD_GROUPS, SSD_INNER // SSD_GROUPS))
    y_ssd = y.reshape(bsz, s, SSD_INNER).astype(h.dtype)

    qh = q.reshape(bsz, s, ATTN_HEADS, 2, ATTN_QK_DIM)
    kh = k.reshape(bsz, s, ATTN_HEADS, 2, ATTN_QK_DIM)
    q1 = jnp.transpose(qh[:, :, :, 0], (0, 2, 1, 3))
    q2 = jnp.transpose(qh[:, :, :, 1], (0, 2, 1, 3))
    k1 = jnp.transpose(kh[:, :, :, 0], (0, 2, 1, 3))
    k2 = jnp.transpose(kh[:, :, :, 1], (0, 2, 1, 3))
    vh = jnp.transpose(v.reshape(bsz, s, ATTN_HEADS, ATTN_V_DIM), (0, 2, 1, 3))
    lam_init = 0.8 - 0.6 * math.exp(-0.3 * layer_idx)
    lam = (jnp.exp(jnp.sum(lambda_q1.astype(jnp.float32) * lambda_k1.astype(jnp.float32)))
           - jnp.exp(jnp.sum(lambda_q2.astype(jnp.float32) * lambda_k2.astype(jnp.float32)))
           + lam_init)
    o = diff_attention(q1, q2, k1, k2, vh, lam, rel_bias)
    o = rmsnorm(o, subln_g) * (1.0 - lam_init)
    y_attn = jnp.transpose(o, (0, 2, 1, 3)).reshape(bsz, s, ATTN_INNER).astype(h.dtype)

    return jnp.concatenate([y_ssd, y_attn], axis=-1) @ w_out


def setup_inputs(seed: int = 0) -> dict:
    key = jax.random.key(seed)
    ks = iter(jax.random.split(key, 40))

    def nrm(shape, scale):
        return jax.random.normal(next(ks), shape, jnp.float32) * scale

    def gain(shape):
        return 1.0 + nrm(shape, 0.02)

    L = DEPTH
    d = D_MODEL
    dt0 = jnp.exp(jax.random.uniform(next(ks), (L, SSD_HEADS)) * (math.log(0.1) - math.log(0.001))
                  + math.log(0.001))
    dt_bias = dt0 + jnp.log(-jnp.expm1(-dt0))
    a_log = jnp.log(jax.random.uniform(next(ks), (L, SSD_HEADS), minval=1.0, maxval=16.0))
    return {
        "x": nrm((BATCH, SEQ, d), 1.0),
        "c": nrm((BATCH, d), 1.0),
        "w_ada": nrm((L, d, N_MOD * d), 0.5 * d ** -0.5),
        "b_ada": nrm((L, N_MOD * d), 0.01),
        "ffn1_pre_g": gain((L, d)),
        "ffn1_post_g": gain((L, d)),
        "ffn1_w_gate": nrm((L, d, D_FF), d ** -0.5),
        "ffn1_w_up": nrm((L, d, D_FF), d ** -0.5),
        "ffn1_w_down": nrm((L, D_FF, d), D_FF ** -0.5),
        "mix_pre_g": gain((L, d)),
        "mix_post_g": gain((L, d)),
        "w_in": nrm((L, d, IN_WIDTH), d ** -0.5),
        "conv_w": nrm((L, SSD_CONV, SSD_XBC), SSD_CONV ** -0.5),
        "conv_b": nrm((L, SSD_XBC), 0.01),
        "dt_bias": dt_bias,
        "a_log": a_log,
        "d_skip": gain((L, SSD_HEADS)),
        "ssd_norm_g": gain((L, SSD_INNER)),
        "lambda_q1": nrm((L, ATTN_QK_DIM), 0.1),
        "lambda_k1": nrm((L, ATTN_QK_DIM), 0.1),
        "lambda_q2": nrm((L, ATTN_QK_DIM), 0.1),
        "lambda_k2": nrm((L, ATTN_QK_DIM), 0.1),
        "subln_g": gain((L, ATTN_V_DIM)),
        "w_out": nrm((L, MIX_WIDTH, d), MIX_WIDTH ** -0.5),
        "ffn2_pre_g": gain((L, d)),
        "ffn2_post_g": gain((L, d)),
        "ffn2_w_gate": nrm((L, d, D_FF), d ** -0.5),
        "ffn2_w_up": nrm((L, d, D_FF), d ** -0.5),
        "ffn2_w_down": nrm((L, D_FF, d), D_FF ** -0.5),
        "rel_bias": nrm((REL_BUCKETS, ATTN_HEADS), 0.5),
    }


def reference(x, c, w_ada, b_ada, ffn1_pre_g, ffn1_post_g, ffn1_w_gate, ffn1_w_up, ffn1_w_down,
              mix_pre_g, mix_post_g, w_in, conv_w, conv_b, dt_bias, a_log, d_skip, ssd_norm_g,
              lambda_q1, lambda_k1, lambda_q2, lambda_k2, subln_g, w_out,
              ffn2_pre_g, ffn2_post_g, ffn2_w_gate, ffn2_w_up, ffn2_w_down, rel_bias):
    cs = jax.nn.silu(c)
    for l in range(DEPTH):
        mod = (cs @ w_ada[l] + b_ada[l])[:, None, :]
        s1, sc1, g1, s2, sc2, g2, s3, sc3, g3 = jnp.split(mod, N_MOD, axis=-1)

        h = rmsnorm(x, ffn1_pre_g[l]) * (1.0 + sc1) + s1
        x = x + 0.5 * g1 * rmsnorm(swiglu(h, ffn1_w_gate[l], ffn1_w_up[l], ffn1_w_down[l]), ffn1_post_g[l])

        h = rmsnorm(x, mix_pre_g[l]) * (1.0 + sc2) + s2
        m = token_mix(h, l, w_in[l], conv_w[l], conv_b[l], dt_bias[l], a_log[l], d_skip[l],
                      ssd_norm_g[l], lambda_q1[l], lambda_k1[l], lambda_q2[l], lambda_k2[l],
                      subln_g[l], rel_bias, w_out[l])
        x = x + g2 * rmsnorm(m, mix_post_g[l])

        h = rmsnorm(x, ffn2_pre_g[l]) * (1.0 + sc3) + s3
        x = x + 0.5 * g3 * rmsnorm(swiglu(h, ffn2_w_gate[l], ffn2_w_up[l], ffn2_w_down[l]), ffn2_post_g[l])
    return x
```

```python
import functools
import math

import jax
import jax.numpy as jnp
from jax import lax
from jax.experimental import pallas as pl
from jax.experimental.pallas import tpu as pltpu

F32 = jnp.float32
BF16 = jnp.bfloat16

LANES = 128
EPS = 1e-6

SSD_HEADS = 16
SSD_HEAD_DIM = 64
SSD_INNER = SSD_HEADS * SSD_HEAD_DIM
SSD_GROUPS = 2
SSD_STATE = 128
SSD_CONV = 4
SSD_CHUNK = 128
SSD_XBC = SSD_INNER + 2 * SSD_GROUPS * SSD_STATE
ATTN_HEADS = 8
ATTN_QK_DIM = 64
ATTN_V_DIM = 128
ATTN_WIDTH = ATTN_HEADS * ATTN_V_DIM
REL_BUCKETS = 32
REL_MAX_DIST = 128
N_MOD = 9
DT_PAD = LANES

VMEM_LIMIT = 56 * 1024 * 1024


def _dot(a, b):
    return jnp.dot(a, b, preferred_element_type=F32)


def _dot_nt(a, b):
    return lax.dot_general(a, b, (((1,), (1,)), ((), ())), preferred_element_type=F32)


def _rms(x, g):
    return x * lax.rsqrt(jnp.mean(x * x, axis=-1, keepdims=True) + EPS) * g


def _silu(x):
    return x / (1.0 + jnp.exp(-x))


def _split_bf16(v):
    hi = v.astype(BF16)
    lo = (v - hi.astype(F32)).astype(BF16)
    return hi, lo


def _mod_kernel(c_ref, w_ref, b_ref, o_ref):
    cs = _silu(c_ref[...]).astype(BF16)
    o_ref[...] = _dot(cs, w_ref[...].astype(BF16)) + b_ref[...]


def _modulation(c_pad, w_ada, b_ada, tn):
    rows, d = c_pad.shape
    n = w_ada.shape[1]
    return pl.pallas_call(
        _mod_kernel,
        grid=(n // tn,),
        in_specs=[
            pl.BlockSpec((rows, d), lambda j: (0, 0)),
            pl.BlockSpec((d, tn), lambda j: (0, j)),
            pl.BlockSpec((1, tn), lambda j: (0, j)),
        ],
        out_specs=pl.BlockSpec((rows, tn), lambda j: (0, j)),
        out_shape=jax.ShapeDtypeStruct((rows, n), F32),
        compiler_params=pltpu.CompilerParams(
            dimension_semantics=("arbitrary",), vmem_limit_bytes=VMEM_LIMIT),
        name="adaln_mod",
    )(c_pad, w_ada, b_ada)


def _mod_spec(idx, d):
    return pl.BlockSpec((None, None, 1, d), lambda b, t: (b, idx, 0, 0))


def _const_spec(shape):
    return pl.BlockSpec(shape, lambda b, t: tuple(0 for _ in shape))


def _ffn_kernel(x_ref, sh_ref, sc_ref, gt_ref, pre_ref, post_ref, wg_ref, wu_ref, wd_ref,
                o_ref, *, ff_chunk):
    x = x_ref[...]
    h = (_rms(x, pre_ref[...]) * (1.0 + sc_ref[...]) + sh_ref[...]).astype(BF16)
    d_ff = wg_ref.shape[1]
    y = None
    for c0 in range(0, d_ff, ff_chunk):
        g = _dot(h, wg_ref[:, c0:c0 + ff_chunk])
        u = _dot(h, wu_ref[:, c0:c0 + ff_chunk])
        a = (_silu(g) * u).astype(BF16)
        part = _dot(a, wd_ref[c0:c0 + ff_chunk, :])
        y = part if y is None else y + part
    o_ref[...] = x + (0.5 * gt_ref[...]) * _rms(y, post_ref[...])


def _ffn(x, mod4, mod_base, pre_g, post_g, wg, wu, wd, tm, ff_chunk):
    bsz, s, d = x.shape
    d_ff = wg.shape[1]
    tok = pl.BlockSpec((None, tm, d), lambda b, t: (b, t, 0))
    return pl.pallas_call(
        functools.partial(_ffn_kernel, ff_chunk=ff_chunk),
        grid=(bsz, s // tm),
        in_specs=[
            tok,
            _mod_spec(mod_base, d), _mod_spec(mod_base + 1, d), _mod_spec(mod_base + 2, d),
            _const_spec((1, d)), _const_spec((1, d)),
            _const_spec((d, d_ff)), _const_spec((d, d_ff)), _const_spec((d_ff, d)),
        ],
        out_specs=tok,
        out_shape=jax.ShapeDtypeStruct((bsz, s, d), F32),
        compiler_params=pltpu.CompilerParams(
            dimension_semantics=("arbitrary", "arbitrary"), vmem_limit_bytes=VMEM_LIMIT),
        name="ffn_half_step",
    )(x, mod4, mod4, mod4, pre_g, post_g, wg, wu, wd)


_Z0 = 0
_XBC0 = _Z0 + SSD_INNER
_DT0 = _XBC0 + SSD_XBC
_Q0 = _DT0 + DT_PAD
_K0 = _Q0 + ATTN_WIDTH
_V0 = _K0 + ATTN_WIDTH
_IN_PAD = _V0 + ATTN_WIDTH


def _inproj_kernel(x_ref, sh_ref, sc_ref, pre_ref, w_ref,
                   z_ref, xbc_ref, dt_ref, q_ref, k_ref, v_ref):
    h = (_rms(x_ref[...], pre_ref[...]) * (1.0 + sc_ref[...]) + sh_ref[...]).astype(BF16)
    z_ref[...] = _dot(h, w_ref[:, _Z0:_XBC0])
    xbc_ref[...] = _dot(h, w_ref[:, _XBC0:_DT0])
    dt_ref[...] = _dot(h, w_ref[:, _DT0:_Q0])
    q_ref[...] = _dot(h, w_ref[:, _Q0:_K0]).astype(BF16)
    k_ref[...] = _dot(h, w_ref[:, _K0:_V0]).astype(BF16)
    v_ref[...] = _dot(h, w_ref[:, _V0:_IN_PAD]).astype(BF16)


def _inproj(x, mod4, pre_g, w_pad, tm):
    bsz, s, d = x.shape

    def tok(width):
        return pl.BlockSpec((None, tm, width), lambda b, t: (b, t, 0))

    def out(width, dtype):
        return jax.ShapeDtypeStruct((bsz, s, width), dtype)

    return pl.pallas_call(
        _inproj_kernel,
        grid=(bsz, s // tm),
        in_specs=[tok(d), _mod_spec(3, d), _mod_spec(4, d), _const_spec((1, d)),
                  _const_spec((d, _IN_PAD))],
        out_specs=[tok(SSD_INNER), tok(SSD_XBC), tok(DT_PAD),
                   tok(ATTN_WIDTH), tok(ATTN_WIDTH), tok(ATTN_WIDTH)],
        out_shape=[out(SSD_INNER, F32), out(SSD_XBC, F32), out(DT_PAD, F32),
                   out(ATTN_WIDTH, BF16), out(ATTN_WIDTH, BF16), out(ATTN_WIDTH, BF16)],
        compiler_params=pltpu.CompilerParams(
            dimension_semantics=("arbitrary", "arbitrary"), vmem_limit_bytes=VMEM_LIMIT),
        name="mixer_in_proj",
    )(x, mod4, mod4, pre_g, w_pad)


_CONV_HALO = 8


def _ssd_kernel(xbc_ref, dt_ref, z_ref, cw_ref, cb_ref, dtb_ref, alog_ref, dskip_ref, ng_ref,
                y_ref, convbuf, hstate, e64, e128, tril):
    q = SSD_CHUNK
    gw = SSD_INNER // SSD_GROUPS
    hpg = SSD_HEADS // SSD_GROUPS

    @pl.when(pl.program_id(1) == 0)
    def _init():
        convbuf[0:_CONV_HALO, :] = jnp.zeros((_CONV_HALO, SSD_XBC), F32)
        hstate[...] = jnp.zeros(hstate.shape, F32)
        r = lax.broadcasted_iota(jnp.int32, (LANES, SSD_INNER), 0)
        c = lax.broadcasted_iota(jnp.int32, (LANES, SSD_INNER), 1)
        e64[...] = (r == c // SSD_HEAD_DIM).astype(BF16)
        r = lax.broadcasted_iota(jnp.int32, (LANES, SSD_HEADS * LANES), 0)
        c = lax.broadcasted_iota(jnp.int32, (LANES, SSD_HEADS * LANES), 1)
        e128[...] = (r == c // LANES).astype(BF16)
        r = lax.broadcasted_iota(jnp.int32, (q, q), 0)
        c = lax.broadcasted_iota(jnp.int32, (q, q), 1)
        tril[...] = (r >= c).astype(BF16)

    u = xbc_ref[...]
    convbuf[_CONV_HALO:_CONV_HALO + q, :] = u
    acc = cb_ref[...] + cw_ref[0:1, :] * convbuf[pl.ds(_CONV_HALO - SSD_CONV + 1, q), :]
    for k in range(1, SSD_CONV):
        acc = acc + cw_ref[k:k + 1, :] * convbuf[pl.ds(_CONV_HALO - SSD_CONV + 1 + k, q), :]
    convbuf[0:_CONV_HALO, :] = u[q - _CONV_HALO:q, :]
    xc = _silu(acc)
    xs = xc[:, :SSD_INNER]
    bm = xc[:, SSD_INNER:SSD_INNER + SSD_GROUPS * SSD_STATE]
    cm = xc[:, SSD_INNER + SSD_GROUPS * SSD_STATE:]

    dt_in = dt_ref[...] + dtb_ref[...]
    dtv = jnp.maximum(dt_in, 0.0) + jnp.log1p(jnp.exp(-jnp.abs(dt_in)))
    d_a = dtv * (-jnp.exp(alog_ref[...]))
    da_hi, da_lo = _split_bf16(d_a)
    acum = _dot(tril[...], da_hi) + _dot(tril[...], da_lo)
    acum_t = acum.T

    def expand(v, e_ref):
        hi, lo = _split_bf16(v)
        return _dot(hi, e_ref[...]) + _dot(lo, e_ref[...])

    dt_x = expand(dtv, e64)
    acum_x = expand(acum, e64)
    acum_x128 = expand(acum, e128)
    alast_x = acum_x[q - 1:q, :]
    ea_x = jnp.exp(acum_x)
    dte_x = jnp.exp(alast_x - acum_x)
    cd_x = jnp.exp(alast_x)

    xdt = xs * dt_x
    xw = (xdt * dte_x).astype(BF16)
    causal = (lax.broadcasted_iota(jnp.int32, (q, q), 0)
              >= lax.broadcasted_iota(jnp.int32, (q, q), 1))
    lane = lax.broadcasted_iota(jnp.int32, (q, LANES), 1)
    low_half = lane < SSD_HEAD_DIM

    y_diag = []
    y_off = []
    for g in range(SSD_GROUPS):
        bg = bm[:, g * SSD_STATE:(g + 1) * SSD_STATE]
        cg = cm[:, g * SSD_STATE:(g + 1) * SSD_STATE].astype(BF16)
        cb = _dot_nt(cg, bg.astype(BF16))
        for pr in range(hpg // 2):
            blk = g * (hpg // 2) + pr
            ms = []
            for hh in (2 * blk, 2 * blk + 1):
                seg = acum_x128[:, hh * LANES:(hh + 1) * LANES] - acum_t[hh:hh + 1, :]
                decay = jnp.exp(jnp.where(causal, seg, -jnp.inf))
                ms.append((cb * decay).astype(BF16))
            xb = xdt[:, blk * LANES:(blk + 1) * LANES]
            x_lo = jnp.where(low_half, xb, 0.0).astype(BF16)
            x_hi = jnp.where(low_half, 0.0, xb).astype(BF16)
            y_diag.append(_dot(jnp.concatenate(ms, axis=1),
                               jnp.concatenate([x_lo, x_hi], axis=0)))
        h_old = hstate[g]
        y_off.append(_dot(cg, h_old.astype(BF16)))
        new_state = _dot(bg.T.astype(BF16), xw[:, g * gw:(g + 1) * gw])
        hstate[g] = h_old * cd_x[:, g * gw:(g + 1) * gw] + new_state

    y = (jnp.concatenate(y_diag, axis=1) + jnp.concatenate(y_off, axis=1) * ea_x
         + dskip_ref[...] * xs)
    y = y * _silu(z_ref[...])
    outs = []
    for g in range(SSD_GROUPS):
        outs.append(_rms(y[:, g * gw:(g + 1) * gw], ng_ref[:, g * gw:(g + 1) * gw]))
    y_ref[...] = jnp.concatenate(outs, axis=1).astype(y_ref.dtype)


def _ssd(xbc, dt, z, conv_w, conv_b, dt_bias, a_log, d_skip, norm_g):
    bsz, s, _ = xbc.shape
    q = SSD_CHUNK

    def tok(width):
        return pl.BlockSpec((None, q, width), lambda b, t: (b, t, 0))

    return pl.pallas_call(
        _ssd_kernel,
        grid=(bsz, s // q),
        in_specs=[tok(SSD_XBC), tok(DT_PAD), tok(SSD_INNER),
                  _const_spec((SSD_CONV, SSD_XBC)), _const_spec((1, SSD_XBC)),
                  _const_spec((1, DT_PAD)), _const_spec((1, DT_PAD)),
                  _const_spec((1, SSD_INNER)), _const_spec((1, SSD_INNER))],
        out_specs=tok(SSD_INNER),
        out_shape=jax.ShapeDtypeStruct((bsz, s, SSD_INNER), BF16),
        scratch_shapes=[
            pltpu.VMEM((_CONV_HALO + q, SSD_XBC), F32),
            pltpu.VMEM((SSD_GROUPS, SSD_STATE, SSD_INNER // SSD_GROUPS), F32),
            pltpu.VMEM((LANES, SSD_INNER), BF16),
            pltpu.VMEM((LANES, SSD_HEADS * LANES), BF16),
            pltpu.VMEM((q, q), BF16),
        ],
        compiler_params=pltpu.CompilerParams(
            dimension_semantics=("arbitrary", "arbitrary"), vmem_limit_bytes=VMEM_LIMIT),
        name="ssd_scan",
    )(xbc, dt, z, conv_w, conv_b, dt_bias, a_log, d_skip, norm_g)


def _bias_kernel(tab_ref, o_ref, *, t):
    h = pl.program_id(0)
    r = lax.broadcasted_iota(jnp.int32, (t, 2 * t), 0)
    c = lax.broadcasted_iota(jnp.int32, (t, 2 * t), 1)
    rel = r - c + t
    n = jnp.maximum(rel, 0)
    max_exact = REL_BUCKETS // 2
    nf = jnp.maximum(n, 1).astype(F32)
    large = max_exact + (jnp.log(nf / max_exact) / math.log(REL_MAX_DIST / max_exact)
                         * (REL_BUCKETS - max_exact)).astype(jnp.int32)
    large = jnp.minimum(large, REL_BUCKETS - 1)
    bucket = jnp.where(n < max_exact, n, large)
    far = tab_ref[REL_BUCKETS - 1, h]
    acc = jnp.zeros((t, 2 * t), F32)
    for b in range(REL_BUCKETS - 1):
        acc = jnp.where(bucket == b, tab_ref[b, h] - far, acc)
    o_ref[...] = jnp.where(rel >= 0, acc, -jnp.inf)


def _bias_tiles(rel_bias, t):
    return pl.pallas_call(
        functools.partial(_bias_kernel, t=t),
        grid=(ATTN_HEADS,),
        in_specs=[pl.BlockSpec(memory_space=pltpu.SMEM)],
        out_specs=pl.BlockSpec((None, t, 2 * t), lambda h: (h, 0, 0)),
        out_shape=jax.ShapeDtypeStruct((ATTN_HEADS, t, 2 * t), F32),
        compiler_params=pltpu.CompilerParams(
            dimension_semantics=("arbitrary",), vmem_limit_bytes=VMEM_LIMIT),
        name="rel_bias_tiles",
    )(rel_bias)


def _attn_kernel(lq1_ref, lk1_ref, lq2_ref, lk2_ref, subg_ref, q_ref, k_ref, v_ref, bias_ref,
                 o_ref, m_sc, l_sc, acc_sc, *, t, lam_init):
    i = pl.program_id(2)
    lam = (jnp.exp(jnp.sum(lq1_ref[...] * lk1_ref[...], axis=-1, keepdims=True))
           - jnp.exp(jnp.sum(lq2_ref[...] * lk2_ref[...], axis=-1, keepdims=True))
           + lam_init)

    qs = q_ref[...] * jnp.asarray(ATTN_QK_DIM ** -0.5, BF16)
    lane = lax.broadcasted_iota(jnp.int32, (t, LANES), 1)
    zero = jnp.zeros_like(qs)
    qq = jnp.concatenate([jnp.where(lane < ATTN_QK_DIM, qs, zero),
                          jnp.where(lane < ATTN_QK_DIM, zero, qs)], axis=0)

    def scores(kstart, bias):
        s = _dot_nt(qq, k_ref[pl.ds(kstart, t), :])
        if bias is not None:
            s = (s.reshape(2, t, t) + bias[None]).reshape(2 * t, t)
        return s

    kd = pl.multiple_of(i * t, t)
    s = scores(kd, bias_ref[:, t:2 * t])
    m0 = jnp.max(s, axis=-1, keepdims=True)
    p = jnp.exp(s - m0)
    m_sc[...] = m0
    l_sc[...] = jnp.sum(p, axis=-1, keepdims=True)
    acc_sc[...] = _dot(p.astype(BF16), v_ref[pl.ds(kd, t), :])

    def update(kstart, bias):
        s = scores(kstart, bias)
        m_prev = m_sc[...]
        m_new = jnp.maximum(m_prev, jnp.max(s, axis=-1, keepdims=True))
        alpha = jnp.exp(m_prev - m_new)
        p = jnp.exp(s - m_new)
        l_sc[...] = alpha * l_sc[...] + jnp.sum(p, axis=-1, keepdims=True)
        acc_sc[...] = alpha * acc_sc[...] + _dot(p.astype(BF16), v_ref[pl.ds(kstart, t), :])
        m_sc[...] = m_new

    @pl.when(i > 0)
    def _prev_tile():
        update(pl.multiple_of((i - 1) * t, t), bias_ref[:, 0:t])

    def far_tile(j, carry):
        update(pl.multiple_of(j * t, t), None)
        return carry

    lax.fori_loop(0, i - 1, far_tile, 0)

    o_all = acc_sc[...] / l_sc[...]
    o = o_all[:t] - lam * o_all[t:]
    o = _rms(o, subg_ref[...]) * (1.0 - lam_init)
    o_ref[...] = o.astype(o_ref.dtype)


def _attention(q, k, v, bias, lq1, lk1, lq2, lk2, subg, t, lam_init):
    bsz, s, _ = q.shape
    small = lambda shape: pl.BlockSpec(shape, lambda b, h, i: (0, 0))
    return pl.pallas_call(
        functools.partial(_attn_kernel, t=t, lam_init=lam_init),
        grid=(bsz, ATTN_HEADS, s // t),
        in_specs=[
            small((1, ATTN_QK_DIM)), small((1, ATTN_QK_DIM)),
            small((1, ATTN_QK_DIM)), small((1, ATTN_QK_DIM)),
            small((1, ATTN_V_DIM)),
            pl.BlockSpec((None, t, LANES), lambda b, h, i: (b, i, h)),
            pl.BlockSpec((None, s, LANES), lambda b, h, i: (b, 0, h)),
            pl.BlockSpec((None, s, LANES), lambda b, h, i: (b, 0, h)),
            pl.BlockSpec((None, t, 2 * t), lambda b, h, i: (h, 0, 0)),
        ],
        out_specs=pl.BlockSpec((None, t, LANES), lambda b, h, i: (b, i, h)),
        out_shape=jax.ShapeDtypeStruct((bsz, s, ATTN_WIDTH), BF16),
        scratch_shapes=[
            pltpu.VMEM((2 * t, 1), F32),
            pltpu.VMEM((2 * t, 1), F32),
            pltpu.VMEM((2 * t, ATTN_V_DIM), F32),
        ],
        compiler_params=pltpu.CompilerParams(
            dimension_semantics=("arbitrary", "arbitrary", "arbitrary"),
            vmem_limit_bytes=VMEM_LIMIT),
        name="diff_attention",
    )(lq1, lk1, lq2, lk2, subg, q, k, v, bias)


def _outproj_kernel(x_ref, ys_ref, ya_ref, gt_ref, post_ref, w_ref, o_ref):
    k_ssd = ys_ref.shape[-1]
    m = _dot(ys_ref[...], w_ref[0:k_ssd, :]) + _dot(ya_ref[...], w_ref[k_ssd:, :])
    o_ref[...] = x_ref[...] + gt_ref[...] * _rms(m, post_ref[...])


def _outproj(x, y_ssd, y_attn, mod4, post_g, w_out, tm):
    bsz, s, d = x.shape

    def tok(width):
        return pl.BlockSpec((None, tm, width), lambda b, t: (b, t, 0))

    return pl.pallas_call(
        _outproj_kernel,
        grid=(bsz, s // tm),
        in_specs=[tok(d), tok(SSD_INNER), tok(ATTN_WIDTH), _mod_spec(5, d),
                  _const_spec((1, d)), _const_spec((SSD_INNER + ATTN_WIDTH, d))],
        out_specs=tok(d),
        out_shape=jax.ShapeDtypeStruct((bsz, s, d), F32),
        compiler_params=pltpu.CompilerParams(
            dimension_semantics=("arbitrary", "arbitrary"), vmem_limit_bytes=VMEM_LIMIT),
        name="mixer_out_proj",
    )(x, y_ssd, y_attn, mod4, post_g, w_out)


def _tiles(s):
    return dict(
        tm_ffn=min(512, s),
        ff_chunk=1408,
        t_attn=min(512, s),
        tn_mod=1152,
    )


def kernel(x, c, w_ada, b_ada, ffn1_pre_g, ffn1_post_g, ffn1_w_gate, ffn1_w_up, ffn1_w_down, mix_pre_g, mix_post_g, w_in, conv_w, conv_b, dt_bias, a_log, d_skip, ssd_norm_g, lambda_q1, lambda_k1, lambda_q2, lambda_k2, subln_g, w_out, ffn2_pre_g, ffn2_post_g, ffn2_w_gate, ffn2_w_up, ffn2_w_down, rel_bias):
    bsz, s, d = x.shape
    depth = w_ada.shape[0]
    cfg = _tiles(s)
    row = lambda v: v.reshape(1, -1)

    c_pad = jnp.zeros((8, d), F32).at[:bsz].set(c)
    bias = _bias_tiles(rel_bias, cfg["t_attn"])

    for l in range(depth):
        mod = _modulation(c_pad, w_ada[l], row(b_ada[l]), cfg["tn_mod"])
        mod4 = mod[:bsz].reshape(bsz, N_MOD, 1, d)

        x = _ffn(x, mod4, 0, row(ffn1_pre_g[l]), row(ffn1_post_g[l]),
                 ffn1_w_gate[l].astype(BF16), ffn1_w_up[l].astype(BF16),
                 ffn1_w_down[l].astype(BF16), cfg["tm_ffn"], cfg["ff_chunk"])

        w = w_in[l]
        sizes = [SSD_INNER, SSD_XBC, SSD_HEADS, ATTN_WIDTH, ATTN_WIDTH, ATTN_WIDTH]
        offs = [0]
        for sz in sizes:
            offs.append(offs[-1] + sz)
        w_pad = jnp.concatenate(
            [w[:, offs[0]:offs[3]], jnp.zeros((d, DT_PAD - SSD_HEADS), w.dtype), w[:, offs[3]:]],
            axis=1).astype(BF16)
        z, xbc, dt, q, k, v = _inproj(x, mod4, row(mix_pre_g[l]), w_pad, cfg["tm_ffn"])

        pad16 = lambda vec: jnp.zeros((1, DT_PAD), F32).at[0, :SSD_HEADS].set(vec)
        y_ssd = _ssd(xbc, dt, z, conv_w[l], row(conv_b[l]), pad16(dt_bias[l]), pad16(a_log[l]),
                     row(jnp.repeat(d_skip[l], SSD_HEAD_DIM)), row(ssd_norm_g[l]))

        lam_init = 0.8 - 0.6 * math.exp(-0.3 * l)
        y_attn = _attention(q, k, v, bias, row(lambda_q1[l]), row(lambda_k1[l]),
                            row(lambda_q2[l]), row(lambda_k2[l]), row(subln_g[l]),
                            cfg["t_attn"], lam_init)

        x = _outproj(x, y_ssd, y_attn, mod4, row(mix_post_g[l]), w_out[l].astype(BF16),
                     cfg["tm_ffn"])

        x = _ffn(x, mod4, 6, row(ffn2_pre_g[l]), row(ffn2_post_g[l]),
                 ffn2_w_gate[l].astype(BF16), ffn2_w_up[l].astype(BF16),
                 ffn2_w_down[l].astype(BF16), cfg["tm_ffn"], cfg["ff_chunk"])
    return x
```

```python
import functools
import math

import jax
import jax.numpy as jnp
from jax import lax
from jax.experimental import pallas as pl
from jax.experimental.pallas import tpu as pltpu

F32 = jnp.float32
BF16 = jnp.bfloat16

LANES = 128
EPS = 1e-6

SSD_HEADS = 16
SSD_HEAD_DIM = 64
SSD_INNER = SSD_HEADS * SSD_HEAD_DIM
SSD_GROUPS = 2
SSD_STATE = 128
SSD_CONV = 4
SSD_CHUNK = 128
SSD_XBC = SSD_INNER + 2 * SSD_GROUPS * SSD_STATE
ATTN_HEADS = 8
ATTN_QK_DIM = 64
ATTN_V_DIM = 128
ATTN_WIDTH = ATTN_HEADS * ATTN_V_DIM
REL_BUCKETS = 32
REL_MAX_DIST = 128
N_MOD = 9
DT_PAD = LANES

VMEM_LIMIT = 56 * 1024 * 1024


def _dot(a, b):
    return jnp.dot(a, b, preferred_element_type=F32)


def _dot_nt(a, b):
    return lax.dot_general(a, b, (((1,), (1,)), ((), ())), preferred_element_type=F32)


def _rms(x, g):
    return x * lax.rsqrt(jnp.mean(x * x, axis=-1, keepdims=True) + EPS) * g


def _silu(x):
    return x / (1.0 + jnp.exp(-x))


def _split_bf16(v):
    hi = v.astype(BF16)
    lo = (v - hi.astype(F32)).astype(BF16)
    return hi, lo


def _mod_kernel(c_ref, w_ref, b_ref, o_ref):
    cs = _silu(c_ref[...]).astype(BF16)
    o_ref[...] = _dot(cs, w_ref[...].astype(BF16)) + b_ref[...]


def _modulation(c_pad, w_ada, b_ada, tn):
    rows, d = c_pad.shape
    n = w_ada.shape[1]
    return pl.pallas_call(
        _mod_kernel,
        grid=(n // tn,),
        in_specs=[
            pl.BlockSpec((rows, d), lambda j: (0, 0)),
            pl.BlockSpec((d, tn), lambda j: (0, j)),
            pl.BlockSpec((1, tn), lambda j: (0, j)),
        ],
        out_specs=pl.BlockSpec((rows, tn), lambda j: (0, j)),
        out_shape=jax.ShapeDtypeStruct((rows, n), F32),
        compiler_params=pltpu.CompilerParams(
            dimension_semantics=("arbitrary",), vmem_limit_bytes=VMEM_LIMIT),
        name="adaln_mod",
    )(c_pad, w_ada, b_ada)


def _mod_spec(idx, d):
    return pl.BlockSpec((None, None, 1, d), lambda b, t: (b, idx, 0, 0))


def _const_spec(shape):
    return pl.BlockSpec(shape, lambda b, t: tuple(0 for _ in shape))


def _ffn_kernel(x_ref, sh_ref, sc_ref, gt_ref, pre_ref, post_ref, wg_ref, wu_ref, wd_ref,
                o_ref, *, ff_chunk):
    x = x_ref[...]
    h = (_rms(x, pre_ref[...]) * (1.0 + sc_ref[...]) + sh_ref[...]).astype(BF16)
    d_ff = wg_ref.shape[1]
    y = None
    for c0 in range(0, d_ff, ff_chunk):
        g = _dot(h, wg_ref[:, c0:c0 + ff_chunk])
        u = _dot(h, wu_ref[:, c0:c0 + ff_chunk])
        a = (_silu(g) * u).astype(BF16)
        part = _dot(a, wd_ref[c0:c0 + ff_chunk, :])
        y = part if y is None else y + part
    o_ref[...] = x + (0.5 * gt_ref[...]) * _rms(y, post_ref[...])


def _ffn(x, mod4, mod_base, pre_g, post_g, wg, wu, wd, tm, ff_chunk):
    bsz, s, d = x.shape
    d_ff = wg.shape[1]
    tok = pl.BlockSpec((None, tm, d), lambda b, t: (b, t, 0))
    return pl.pallas_call(
        functools.partial(_ffn_kernel, ff_chunk=ff_chunk),
        grid=(bsz, s // tm),
        in_specs=[
            tok,
            _mod_spec(mod_base, d), _mod_spec(mod_base + 1, d), _mod_spec(mod_base + 2, d),
            _const_spec((1, d)), _const_spec((1, d)),
            _const_spec((d, d_ff)), _const_spec((d, d_ff)), _const_spec((d_ff, d)),
        ],
        out_specs=tok,
        out_shape=jax.ShapeDtypeStruct((bsz, s, d), F32),
        compiler_params=pltpu.CompilerParams(
            dimension_semantics=("arbitrary", "arbitrary"), vmem_limit_bytes=VMEM_LIMIT),
        name="ffn_half_step",
    )(x, mod4, mod4, mod4, pre_g, post_g, wg, wu, wd)


_Z0 = 0
_XBC0 = _Z0 + SSD_INNER
_DT0 = _XBC0 + SSD_XBC
_Q0 = _DT0 + DT_PAD
_K0 = _Q0 + ATTN_WIDTH
_V0 = _K0 + ATTN_WIDTH
_IN_PAD = _V0 + ATTN_WIDTH


def _inproj_kernel(x_ref, sh_ref, sc_ref, pre_ref, w_ref,
                   z_ref, xbc_ref, dt_ref, q_ref, k_ref, vt_ref):
    h = (_rms(x_ref[...], pre_ref[...]) * (1.0 + sc_ref[...]) + sh_ref[...]).astype(BF16)
    z_ref[...] = _dot(h, w_ref[:, _Z0:_XBC0])
    xbc_ref[...] = _dot(h, w_ref[:, _XBC0:_DT0])
    dt_ref[...] = _dot(h, w_ref[:, _DT0:_Q0])
    q_ref[...] = _dot(h, w_ref[:, _Q0:_K0]).astype(BF16)
    k_ref[...] = _dot(h, w_ref[:, _K0:_V0]).astype(BF16)
    for hh in range(ATTN_HEADS):
        v_h = _dot(h, w_ref[:, _V0 + hh * ATTN_V_DIM:_V0 + (hh + 1) * ATTN_V_DIM])
        vt_ref[hh] = v_h.T.astype(BF16)


def _inproj(x, mod4, pre_g, w_pad, tm):
    bsz, s, d = x.shape

    def tok(width):
        return pl.BlockSpec((None, tm, width), lambda b, t: (b, t, 0))

    def out(width, dtype):
        return jax.ShapeDtypeStruct((bsz, s, width), dtype)

    return pl.pallas_call(
        _inproj_kernel,
        grid=(bsz, s // tm),
        in_specs=[tok(d), _mod_spec(3, d), _mod_spec(4, d), _const_spec((1, d)),
                  _const_spec((d, _IN_PAD))],
        out_specs=[tok(SSD_INNER), tok(SSD_XBC), tok(DT_PAD),
                   tok(ATTN_WIDTH), tok(ATTN_WIDTH),
                   pl.BlockSpec((None, None, ATTN_HEADS, ATTN_V_DIM, tm),
                                lambda b, t: (b, t, 0, 0, 0))],
        out_shape=[out(SSD_INNER, F32), out(SSD_XBC, F32), out(DT_PAD, F32),
                   out(ATTN_WIDTH, BF16), out(ATTN_WIDTH, BF16),
                   jax.ShapeDtypeStruct((bsz, s // tm, ATTN_HEADS, ATTN_V_DIM, tm), BF16)],
        compiler_params=pltpu.CompilerParams(
            dimension_semantics=("arbitrary", "arbitrary"), vmem_limit_bytes=VMEM_LIMIT),
        name="mixer_in_proj",
    )(x, mod4, mod4, pre_g, w_pad)


_CONV_HALO = 8


def _ssd_kernel(xbc_ref, dt_ref, z_ref, cw_ref, cb_ref, dtb_ref, alog_ref, dskip_ref, ng_ref,
                y_ref, convbuf, hstate, e64, e128, tril):
    q = SSD_CHUNK
    gw = SSD_INNER // SSD_GROUPS
    hpg = SSD_HEADS // SSD_GROUPS

    @pl.when(pl.program_id(1) == 0)
    def _init():
        convbuf[0:_CONV_HALO, :] = jnp.zeros((_CONV_HALO, SSD_XBC), F32)
        hstate[...] = jnp.zeros(hstate.shape, F32)
        r = lax.broadcasted_iota(jnp.int32, (LANES, SSD_INNER), 0)
        c = lax.broadcasted_iota(jnp.int32, (LANES, SSD_INNER), 1)
        e64[...] = (r == c // SSD_HEAD_DIM).astype(BF16)
        r = lax.broadcasted_iota(jnp.int32, (LANES, SSD_HEADS * LANES), 0)
        c = lax.broadcasted_iota(jnp.int32, (LANES, SSD_HEADS * LANES), 1)
        e128[...] = (r == c // LANES).astype(BF16)
        r = lax.broadcasted_iota(jnp.int32, (q, q), 0)
        c = lax.broadcasted_iota(jnp.int32, (q, q), 1)
        tril[...] = (r >= c).astype(BF16)

    u = xbc_ref[...]
    convbuf[_CONV_HALO:_CONV_HALO + q, :] = u
    acc = cb_ref[...] + cw_ref[0:1, :] * convbuf[pl.ds(_CONV_HALO - SSD_CONV + 1, q), :]
    for k in range(1, SSD_CONV):
        acc = acc + cw_ref[k:k + 1, :] * convbuf[pl.ds(_CONV_HALO - SSD_CONV + 1 + k, q), :]
    convbuf[0:_CONV_HALO, :] = u[q - _CONV_HALO:q, :]
    xc = _silu(acc)
    xs = xc[:, :SSD_INNER]
    bm = xc[:, SSD_INNER:SSD_INNER + SSD_GROUPS * SSD_STATE]
    cm = xc[:, SSD_INNER + SSD_GROUPS * SSD_STATE:]

    dt_in = dt_ref[...] + dtb_ref[...]
    dtv = jnp.maximum(dt_in, 0.0) + jnp.log1p(jnp.exp(-jnp.abs(dt_in)))
    d_a = dtv * (-jnp.exp(alog_ref[...]))
    da_hi, da_lo = _split_bf16(d_a)
    acum = _dot(tril[...], da_hi) + _dot(tril[...], da_lo)
    acum_t = acum.T

    def expand(v, e_ref):
        hi, lo = _split_bf16(v)
        return _dot(hi, e_ref[...]) + _dot(lo, e_ref[...])

    dt_x = expand(dtv, e64)
    acum_x = expand(acum, e64)
    acum_x128 = expand(acum, e128)
    alast_x = acum_x[q - 1:q, :]
    ea_x = jnp.exp(acum_x)
    dte_x = jnp.exp(alast_x - acum_x)
    cd_x = jnp.exp(alast_x)

    xdt = xs * dt_x
    xw = (xdt * dte_x).astype(BF16)
    causal = (lax.broadcasted_iota(jnp.int32, (q, q), 0)
              >= lax.broadcasted_iota(jnp.int32, (q, q), 1))
    lane = lax.broadcasted_iota(jnp.int32, (q, LANES), 1)
    low_half = lane < SSD_HEAD_DIM

    y_diag = []
    y_off = []
    for g in range(SSD_GROUPS):
        bg = bm[:, g * SSD_STATE:(g + 1) * SSD_STATE]
        cg = cm[:, g * SSD_STATE:(g + 1) * SSD_STATE].astype(BF16)
        cb = _dot_nt(cg, bg.astype(BF16))
        for pr in range(hpg // 2):
            blk = g * (hpg // 2) + pr
            ms = []
            for hh in (2 * blk, 2 * blk + 1):
                seg = acum_x128[:, hh * LANES:(hh + 1) * LANES] - acum_t[hh:hh + 1, :]
                decay = jnp.exp(jnp.where(causal, seg, -jnp.inf))
                ms.append((cb * decay).astype(BF16))
            xb = xdt[:, blk * LANES:(blk + 1) * LANES]
            x_lo = jnp.where(low_half, xb, 0.0).astype(BF16)
            x_hi = jnp.where(low_half, 0.0, xb).astype(BF16)
            y_diag.append(_dot(jnp.concatenate(ms, axis=1),
                               jnp.concatenate([x_lo, x_hi], axis=0)))
        h_old = hstate[g]
        y_off.append(_dot(cg, h_old.astype(BF16)))
        new_state = _dot(bg.T.astype(BF16), xw[:, g * gw:(g + 1) * gw])
        hstate[g] = h_old * cd_x[:, g * gw:(g + 1) * gw] + new_state

    y = (jnp.concatenate(y_diag, axis=1) + jnp.concatenate(y_off, axis=1) * ea_x
         + dskip_ref[...] * xs)
    y = y * _silu(z_ref[...])
    outs = []
    for g in range(SSD_GROUPS):
        outs.append(_rms(y[:, g * gw:(g + 1) * gw], ng_ref[:, g * gw:(g + 1) * gw]))
    y_ref[...] = jnp.concatenate(outs, axis=1).astype(y_ref.dtype)


def _ssd(xbc, dt, z, conv_w, conv_b, dt_bias, a_log, d_skip, norm_g):
    bsz, s, _ = xbc.shape
    q = SSD_CHUNK

    def tok(width):
        return pl.BlockSpec((None, q, width), lambda b, t: (b, t, 0))

    return pl.pallas_call(
        _ssd_kernel,
        grid=(bsz, s // q),
        in_specs=[tok(SSD_XBC), tok(DT_PAD), tok(SSD_INNER),
                  _const_spec((SSD_CONV, SSD_XBC)), _const_spec((1, SSD_XBC)),
                  _const_spec((1, DT_PAD)), _const_spec((1, DT_PAD)),
                  _const_spec((1, SSD_INNER)), _const_spec((1, SSD_INNER))],
        out_specs=tok(SSD_INNER),
        out_shape=jax.ShapeDtypeStruct((bsz, s, SSD_INNER), BF16),
        scratch_shapes=[
            pltpu.VMEM((_CONV_HALO + q, SSD_XBC), F32),
            pltpu.VMEM((SSD_GROUPS, SSD_STATE, SSD_INNER // SSD_GROUPS), F32),
            pltpu.VMEM((LANES, SSD_INNER), BF16),
            pltpu.VMEM((LANES, SSD_HEADS * LANES), BF16),
            pltpu.VMEM((q, q), BF16),
        ],
        compiler_params=pltpu.CompilerParams(
            dimension_semantics=("arbitrary", "arbitrary"), vmem_limit_bytes=VMEM_LIMIT),
        name="ssd_scan",
    )(xbc, dt, z, conv_w, conv_b, dt_bias, a_log, d_skip, norm_g)


def _bias_kernel(tab_ref, o_ref, *, t):
    h = pl.program_id(0)
    kk = lax.broadcasted_iota(jnp.int32, (2 * t, t), 0)
    qq = lax.broadcasted_iota(jnp.int32, (2 * t, t), 1)
    rel = qq - kk + t
    n = jnp.maximum(rel, 0)
    max_exact = REL_BUCKETS // 2
    nf = jnp.maximum(n, 1).astype(F32)
    large = max_exact + (jnp.log(nf / max_exact) / math.log(REL_MAX_DIST / max_exact)
                         * (REL_BUCKETS - max_exact)).astype(jnp.int32)
    large = jnp.minimum(large, REL_BUCKETS - 1)
    bucket = jnp.where(n < max_exact, n, large)
    far = tab_ref[REL_BUCKETS - 1, h]
    acc = jnp.zeros((2 * t, t), F32)
    for b in range(REL_BUCKETS - 1):
        acc = jnp.where(bucket == b, tab_ref[b, h] - far, acc)
    o_ref[...] = jnp.where(rel >= 0, acc, -jnp.inf)


def _bias_tiles(rel_bias, t):
    return pl.pallas_call(
        functools.partial(_bias_kernel, t=t),
        grid=(ATTN_HEADS,),
        in_specs=[pl.BlockSpec(memory_space=pltpu.SMEM)],
        out_specs=pl.BlockSpec((None, 2 * t, t), lambda h: (h, 0, 0)),
        out_shape=jax.ShapeDtypeStruct((ATTN_HEADS, 2 * t, t), F32),
        compiler_params=pltpu.CompilerParams(
            dimension_semantics=("arbitrary",), vmem_limit_bytes=VMEM_LIMIT),
        name="rel_bias_tiles",
    )(rel_bias)


def _attn_kernel(lq1_ref, lk1_ref, lq2_ref, lk2_ref, subg_ref, q_ref, k_ref, vt_ref, bias_ref,
                 o_ref, m_sc, l_sc, acc_sc, *, t, lam_init):
    i = pl.program_id(2)
    lam = (jnp.exp(jnp.sum(lq1_ref[...] * lk1_ref[...], axis=-1, keepdims=True))
           - jnp.exp(jnp.sum(lq2_ref[...] * lk2_ref[...], axis=-1, keepdims=True))
           + lam_init)

    qs = q_ref[...] * jnp.asarray(ATTN_QK_DIM ** -0.5, BF16)
    lane = lax.broadcasted_iota(jnp.int32, (t, LANES), 1)
    zero = jnp.zeros_like(qs)
    qq = jnp.concatenate([jnp.where(lane < ATTN_QK_DIM, qs, zero),
                          jnp.where(lane < ATTN_QK_DIM, zero, qs)], axis=0)

    def scores(j, bias):
        s = _dot_nt(k_ref[pl.ds(pl.multiple_of(j * t, t), t), :], qq)
        if bias is not None:
            s = s + jnp.concatenate([bias, bias], axis=1)
        return s

    s = scores(i, bias_ref[t:2 * t, :])
    m0 = jnp.max(s, axis=0, keepdims=True)
    p = jnp.exp(s - m0)
    m_sc[...] = m0
    l_sc[...] = jnp.sum(p, axis=0, keepdims=True)
    acc_sc[...] = _dot(vt_ref[i], p.astype(BF16))

    def update(j, bias):
        s = scores(j, bias)
        m_prev = m_sc[...]
        m_new = jnp.maximum(m_prev, jnp.max(s, axis=0, keepdims=True))
        alpha = jnp.exp(m_prev - m_new)
        p = jnp.exp(s - m_new)
        l_sc[...] = alpha * l_sc[...] + jnp.sum(p, axis=0, keepdims=True)
        acc_sc[...] = alpha * acc_sc[...] + _dot(vt_ref[j], p.astype(BF16))
        m_sc[...] = m_new

    @pl.when(i > 0)
    def _prev_tile():
        update(i - 1, bias_ref[0:t, :])

    def far_tile(j, carry):
        update(j, None)
        return carry

    lax.fori_loop(0, i - 1, far_tile, 0)

    o_all = acc_sc[...] / l_sc[...]
    o = (o_all[:, :t] - lam * o_all[:, t:]).T
    o = _rms(o, subg_ref[...]) * (1.0 - lam_init)
    o_ref[...] = o.astype(o_ref.dtype)


def _attention(q, k, vt, bias, lq1, lk1, lq2, lk2, subg, t, lam_init):
    bsz, s, _ = q.shape
    small = lambda shape: pl.BlockSpec(shape, lambda b, h, i: (0, 0))
    return pl.pallas_call(
        functools.partial(_attn_kernel, t=t, lam_init=lam_init),
        grid=(bsz, ATTN_HEADS, s // t),
        in_specs=[
            small((1, ATTN_QK_DIM)), small((1, ATTN_QK_DIM)),
            small((1, ATTN_QK_DIM)), small((1, ATTN_QK_DIM)),
            small((1, ATTN_V_DIM)),
            pl.BlockSpec((None, t, LANES), lambda b, h, i: (b, i, h)),
            pl.BlockSpec((None, s, LANES), lambda b, h, i: (b, 0, h)),
            pl.BlockSpec((None, s // t, None, ATTN_V_DIM, t), lambda b, h, i: (b, 0, h, 0, 0)),
            pl.BlockSpec((None, 2 * t, t), lambda b, h, i: (h, 0, 0)),
        ],
        out_specs=pl.BlockSpec((None, t, LANES), lambda b, h, i: (b, i, h)),
        out_shape=jax.ShapeDtypeStruct((bsz, s, ATTN_WIDTH), BF16),
        scratch_shapes=[
            pltpu.VMEM((1, 2 * t), F32),
            pltpu.VMEM((1, 2 * t), F32),
            pltpu.VMEM((ATTN_V_DIM, 2 * t), F32),
        ],
        compiler_params=pltpu.CompilerParams(
            dimension_semantics=("arbitrary", "arbitrary", "arbitrary"),
            vmem_limit_bytes=VMEM_LIMIT),
        name="diff_attention",
    )(lq1, lk1, lq2, lk2, subg, q, k, vt, bias)


def _outproj_kernel(x_ref, ys_ref, ya_ref, gt_ref, post_ref, w_ref, o_ref):
    k_ssd = ys_ref.shape[-1]
    m = _dot(ys_ref[...], w_ref[0:k_ssd, :]) + _dot(ya_ref[...], w_ref[k_ssd:, :])
    o_ref[...] = x_ref[...] + gt_ref[...] * _rms(m, post_ref[...])


def _outproj(x, y_ssd, y_attn, mod4, post_g, w_out, tm):
    bsz, s, d = x.shape

    def tok(width):
        return pl.BlockSpec((None, tm, width), lambda b, t: (b, t, 0))

    return pl.pallas_call(
        _outproj_kernel,
        grid=(bsz, s // tm),
        in_specs=[tok(d), tok(SSD_INNER), tok(ATTN_WIDTH), _mod_spec(5, d),
                  _const_spec((1, d)), _const_spec((SSD_INNER + ATTN_WIDTH, d))],
        out_specs=tok(d),
        out_shape=jax.ShapeDtypeStruct((bsz, s, d), F32),
        compiler_params=pltpu.CompilerParams(
            dimension_semantics=("arbitrary", "arbitrary"), vmem_limit_bytes=VMEM_LIMIT),
        name="mixer_out_proj",
    )(x, y_ssd, y_attn, mod4, post_g, w_out)


def _tiles(s):
    return dict(
        tm_ffn=min(512, s),
        ff_chunk=1408,
        t_attn=min(512, s),
        tn_mod=1152,
    )


def kernel(x, c, w_ada, b_ada, ffn1_pre_g, ffn1_post_g, ffn1_w_gate, ffn1_w_up, ffn1_w_down, mix_pre_g, mix_post_g, w_in, conv_w, conv_b, dt_bias, a_log, d_skip, ssd_norm_g, lambda_q1, lambda_k1, lambda_q2, lambda_k2, subln_g, w_out, ffn2_pre_g, ffn2_post_g, ffn2_w_gate, ffn2_w_up, ffn2_w_down, rel_bias):
    bsz, s, d = x.shape
    depth = w_ada.shape[0]
    cfg = _tiles(s)
    row = lambda v: v.reshape(1, -1)

    c_pad = jnp.zeros((8, d), F32).at[:bsz].set(c)
    bias = _bias_tiles(rel_bias, cfg["t_attn"])

    for l in range(depth):
        mod = _modulation(c_pad, w_ada[l], row(b_ada[l]), cfg["tn_mod"])
        mod4 = mod[:bsz].reshape(bsz, N_MOD, 1, d)

        x = _ffn(x, mod4, 0, row(ffn1_pre_g[l]), row(ffn1_post_g[l]),
                 ffn1_w_gate[l].astype(BF16), ffn1_w_up[l].astype(BF16),
                 ffn1_w_down[l].astype(BF16), cfg["tm_ffn"], cfg["ff_chunk"])

        w = w_in[l]
        sizes = [SSD_INNER, SSD_XBC, SSD_HEADS, ATTN_WIDTH, ATTN_WIDTH, ATTN_WIDTH]
        offs = [0]
        for sz in sizes:
            offs.append(offs[-1] + sz)
        w_pad = jnp.concatenate(
            [w[:, offs[0]:offs[3]], jnp.zeros((d, DT_PAD - SSD_HEADS), w.dtype), w[:, offs[3]:]],
            axis=1).astype(BF16)
        z, xbc, dt, q, k, vt = _inproj(x, mod4, row(mix_pre_g[l]), w_pad, cfg["t_attn"])

        pad16 = lambda vec: jnp.zeros((1, DT_PAD), F32).at[0, :SSD_HEADS].set(vec)
        y_ssd = _ssd(xbc, dt, z, conv_w[l], row(conv_b[l]), pad16(dt_bias[l]), pad16(a_log[l]),
                     row(jnp.repeat(d_skip[l], SSD_HEAD_DIM)), row(ssd_norm_g[l]))

        lam_init = 0.8 - 0.6 * math.exp(-0.3 * l)
        y_attn = _attention(q, k, vt, bias, row(lambda_q1[l]), row(lambda_k1[l]),
                            row(lambda_q2[l]), row(lambda_k2[l]), row(subln_g[l]),
                            cfg["t_attn"], lam_init)

        x = _outproj(x, y_ssd, y_attn, mod4, row(mix_post_g[l]), w_out[l].astype(BF16),
                     cfg["tm_ffn"])

        x = _ffn(x, mod4, 6, row(ffn2_pre_g[l]), row(ffn2_post_g[l]),
                 ffn2_w_gate[l].astype(BF16), ffn2_w_up[l].astype(BF16),
                 ffn2_w_down[l].astype(BF16), cfg["tm_ffn"], cfg["ff_chunk"])
    return x
```

```python
import functools
import math

import jax
import jax.numpy as jnp
from jax import lax
from jax.experimental import pallas as pl
from jax.experimental.pallas import tpu as pltpu

F32 = jnp.float32
BF16 = jnp.bfloat16

LANES = 128
EPS = 1e-6

SSD_HEADS = 16
SSD_HEAD_DIM = 64
SSD_INNER = SSD_HEADS * SSD_HEAD_DIM
SSD_GROUPS = 2
SSD_STATE = 128
SSD_CONV = 4
SSD_CHUNK = 128
SSD_XBC = SSD_INNER + 2 * SSD_GROUPS * SSD_STATE
ATTN_HEADS = 8
ATTN_QK_DIM = 64
ATTN_V_DIM = 128
ATTN_WIDTH = ATTN_HEADS * ATTN_V_DIM
REL_BUCKETS = 32
REL_MAX_DIST = 128
N_MOD = 9
DT_PAD = LANES
BF16_SUBLANES = 16
V_AUG = ATTN_V_DIM + BF16_SUBLANES
LOG2E = 1.4426950408889634

VMEM_LIMIT = 56 * 1024 * 1024


def _dot(a, b):
    return jnp.dot(a, b, preferred_element_type=F32)


def _dot_nt(a, b):
    return lax.dot_general(a, b, (((1,), (1,)), ((), ())), preferred_element_type=F32)


def _rms(x, g):
    return x * lax.rsqrt(jnp.mean(x * x, axis=-1, keepdims=True) + EPS) * g


def _silu(x):
    return x / (1.0 + jnp.exp(-x))


def _split_bf16(v):
    hi = v.astype(BF16)
    lo = (v - hi.astype(F32)).astype(BF16)
    return hi, lo


def _mod_kernel(c_ref, w_ref, b_ref, o_ref):
    cs = _silu(c_ref[...]).astype(BF16)
    o_ref[...] = _dot(cs, w_ref[...].astype(BF16)) + b_ref[...]


def _modulation(c_pad, w_ada, b_ada, tn):
    rows, d = c_pad.shape
    n = w_ada.shape[1]
    return pl.pallas_call(
        _mod_kernel,
        grid=(n // tn,),
        in_specs=[
            pl.BlockSpec((rows, d), lambda j: (0, 0)),
            pl.BlockSpec((d, tn), lambda j: (0, j)),
            pl.BlockSpec((1, tn), lambda j: (0, j)),
        ],
        out_specs=pl.BlockSpec((rows, tn), lambda j: (0, j)),
        out_shape=jax.ShapeDtypeStruct((rows, n), F32),
        compiler_params=pltpu.CompilerParams(
            dimension_semantics=("arbitrary",), vmem_limit_bytes=VMEM_LIMIT),
        name="adaln_mod",
    )(c_pad, w_ada, b_ada)


def _mod_spec(idx, d):
    return pl.BlockSpec((None, None, 1, d), lambda b, t: (b, idx, 0, 0))


def _const_spec(shape):
    return pl.BlockSpec(shape, lambda b, t: tuple(0 for _ in shape))


def _ffn_kernel(x_ref, sh_ref, sc_ref, gt_ref, pre_ref, post_ref, wg_ref, wu_ref, wd_ref,
                o_ref, *, ff_chunk):
    x = x_ref[...]
    h = (_rms(x, pre_ref[...]) * (1.0 + sc_ref[...]) + sh_ref[...]).astype(BF16)
    d_ff = wg_ref.shape[1]
    y = None
    for c0 in range(0, d_ff, ff_chunk):
        g = _dot(h, wg_ref[:, c0:c0 + ff_chunk])
        u = _dot(h, wu_ref[:, c0:c0 + ff_chunk])
        a = (_silu(g) * u).astype(BF16)
        part = _dot(a, wd_ref[c0:c0 + ff_chunk, :])
        y = part if y is None else y + part
    o_ref[...] = x + (0.5 * gt_ref[...]) * _rms(y, post_ref[...])


def _ffn(x, mod4, mod_base, pre_g, post_g, wg, wu, wd, tm, ff_chunk):
    bsz, s, d = x.shape
    d_ff = wg.shape[1]
    tok = pl.BlockSpec((None, tm, d), lambda b, t: (b, t, 0))
    return pl.pallas_call(
        functools.partial(_ffn_kernel, ff_chunk=ff_chunk),
        grid=(bsz, s // tm),
        in_specs=[
            tok,
            _mod_spec(mod_base, d), _mod_spec(mod_base + 1, d), _mod_spec(mod_base + 2, d),
            _const_spec((1, d)), _const_spec((1, d)),
            _const_spec((d, d_ff)), _const_spec((d, d_ff)), _const_spec((d_ff, d)),
        ],
        out_specs=tok,
        out_shape=jax.ShapeDtypeStruct((bsz, s, d), F32),
        compiler_params=pltpu.CompilerParams(
            dimension_semantics=("arbitrary", "arbitrary"), vmem_limit_bytes=VMEM_LIMIT),
        name="ffn_half_step",
    )(x, mod4, mod4, mod4, pre_g, post_g, wg, wu, wd)


_Z0 = 0
_XBC0 = _Z0 + SSD_INNER
_DT0 = _XBC0 + SSD_XBC
_Q0 = _DT0 + DT_PAD
_K0 = _Q0 + ATTN_WIDTH
_V0 = _K0 + ATTN_WIDTH
_IN_PAD = _V0 + ATTN_WIDTH


def _inproj_kernel(x_ref, sh_ref, sc_ref, pre_ref, w_ref,
                   z_ref, xbc_ref, dt_ref, q_ref, k_ref, vt_ref):
    h = (_rms(x_ref[...], pre_ref[...]) * (1.0 + sc_ref[...]) + sh_ref[...]).astype(BF16)
    z_ref[...] = _dot(h, w_ref[:, _Z0:_XBC0])
    xbc_ref[...] = _dot(h, w_ref[:, _XBC0:_DT0])
    dt_ref[...] = _dot(h, w_ref[:, _DT0:_Q0])
    q_ref[...] = (_dot(h, w_ref[:, _Q0:_K0]) * (ATTN_QK_DIM ** -0.5 * LOG2E)).astype(BF16)
    k_ref[...] = _dot(h, w_ref[:, _K0:_V0]).astype(BF16)
    ones = jnp.ones((V_AUG - ATTN_V_DIM, x_ref.shape[0]), BF16)
    for hh in range(ATTN_HEADS):
        v_h = _dot(h, w_ref[:, _V0 + hh * ATTN_V_DIM:_V0 + (hh + 1) * ATTN_V_DIM])
        vt_ref[hh, 0:ATTN_V_DIM, :] = v_h.T.astype(BF16)
        vt_ref[hh, ATTN_V_DIM:V_AUG, :] = ones


def _inproj(x, mod4, pre_g, w_pad, tm):
    bsz, s, d = x.shape

    def tok(width):
        return pl.BlockSpec((None, tm, width), lambda b, t: (b, t, 0))

    def out(width, dtype):
        return jax.ShapeDtypeStruct((bsz, s, width), dtype)

    return pl.pallas_call(
        _inproj_kernel,
        grid=(bsz, s // tm),
        in_specs=[tok(d), _mod_spec(3, d), _mod_spec(4, d), _const_spec((1, d)),
                  _const_spec((d, _IN_PAD))],
        out_specs=[tok(SSD_INNER), tok(SSD_XBC), tok(DT_PAD),
                   tok(ATTN_WIDTH), tok(ATTN_WIDTH),
                   pl.BlockSpec((None, None, ATTN_HEADS, V_AUG, tm),
                                lambda b, t: (b, t, 0, 0, 0))],
        out_shape=[out(SSD_INNER, F32), out(SSD_XBC, F32), out(DT_PAD, F32),
                   out(ATTN_WIDTH, BF16), out(ATTN_WIDTH, BF16),
                   jax.ShapeDtypeStruct((bsz, s // tm, ATTN_HEADS, V_AUG, tm), BF16)],
        compiler_params=pltpu.CompilerParams(
            dimension_semantics=("arbitrary", "arbitrary"), vmem_limit_bytes=VMEM_LIMIT),
        name="mixer_in_proj",
    )(x, mod4, mod4, pre_g, w_pad)


_CONV_HALO = 8


def _ssd_kernel(xbc_ref, dt_ref, z_ref, cw_ref, cb_ref, dtb_ref, alog_ref, dskip_ref, ng_ref,
                y_ref, convbuf, hstate, e64, e128, tril):
    q = SSD_CHUNK
    gw = SSD_INNER // SSD_GROUPS
    hpg = SSD_HEADS // SSD_GROUPS

    @pl.when(pl.program_id(1) == 0)
    def _init():
        convbuf[0:_CONV_HALO, :] = jnp.zeros((_CONV_HALO, SSD_XBC), F32)
        hstate[...] = jnp.zeros(hstate.shape, F32)
        r = lax.broadcasted_iota(jnp.int32, (LANES, SSD_INNER), 0)
        c = lax.broadcasted_iota(jnp.int32, (LANES, SSD_INNER), 1)
        e64[...] = (r == c // SSD_HEAD_DIM).astype(BF16)
        r = lax.broadcasted_iota(jnp.int32, (LANES, SSD_HEADS * LANES), 0)
        c = lax.broadcasted_iota(jnp.int32, (LANES, SSD_HEADS * LANES), 1)
        e128[...] = (r == c // LANES).astype(BF16)
        r = lax.broadcasted_iota(jnp.int32, (q, q), 0)
        c = lax.broadcasted_iota(jnp.int32, (q, q), 1)
        tril[...] = (r >= c).astype(BF16)

    u = xbc_ref[...]
    convbuf[_CONV_HALO:_CONV_HALO + q, :] = u
    acc = cb_ref[...] + cw_ref[0:1, :] * convbuf[pl.ds(_CONV_HALO - SSD_CONV + 1, q), :]
    for k in range(1, SSD_CONV):
        acc = acc + cw_ref[k:k + 1, :] * convbuf[pl.ds(_CONV_HALO - SSD_CONV + 1 + k, q), :]
    convbuf[0:_CONV_HALO, :] = u[q - _CONV_HALO:q, :]
    xc = _silu(acc)
    xs = xc[:, :SSD_INNER]
    bm = xc[:, SSD_INNER:SSD_INNER + SSD_GROUPS * SSD_STATE]
    cm = xc[:, SSD_INNER + SSD_GROUPS * SSD_STATE:]

    dt_in = dt_ref[...] + dtb_ref[...]
    dtv = jnp.maximum(dt_in, 0.0) + jnp.log1p(jnp.exp(-jnp.abs(dt_in)))
    d_a = dtv * (-jnp.exp(alog_ref[...]))
    da_hi, da_lo = _split_bf16(d_a)
    acum = _dot(tril[...], da_hi) + _dot(tril[...], da_lo)
    acum_t = acum.T

    def expand(v, e_ref):
        hi, lo = _split_bf16(v)
        return _dot(hi, e_ref[...]) + _dot(lo, e_ref[...])

    dt_x = expand(dtv, e64)
    acum_x = expand(acum, e64)
    acum_x128 = expand(acum, e128)
    alast_x = acum_x[q - 1:q, :]
    ea_x = jnp.exp(acum_x)
    dte_x = jnp.exp(alast_x - acum_x)
    cd_x = jnp.exp(alast_x)

    xdt = xs * dt_x
    xw = (xdt * dte_x).astype(BF16)
    causal = (lax.broadcasted_iota(jnp.int32, (q, q), 0)
              >= lax.broadcasted_iota(jnp.int32, (q, q), 1))
    lane = lax.broadcasted_iota(jnp.int32, (q, LANES), 1)
    low_half = lane < SSD_HEAD_DIM

    y_diag = []
    y_off = []
    for g in range(SSD_GROUPS):
        bg = bm[:, g * SSD_STATE:(g + 1) * SSD_STATE]
        cg = cm[:, g * SSD_STATE:(g + 1) * SSD_STATE].astype(BF16)
        cb = _dot_nt(cg, bg.astype(BF16))
        for pr in range(hpg // 2):
            blk = g * (hpg // 2) + pr
            ms = []
            for hh in (2 * blk, 2 * blk + 1):
                seg = acum_x128[:, hh * LANES:(hh + 1) * LANES] - acum_t[hh:hh + 1, :]
                decay = jnp.exp(jnp.where(causal, seg, -jnp.inf))
                ms.append((cb * decay).astype(BF16))
            xb = xdt[:, blk * LANES:(blk + 1) * LANES]
            x_lo = jnp.where(low_half, xb, 0.0).astype(BF16)
            x_hi = jnp.where(low_half, 0.0, xb).astype(BF16)
            y_diag.append(_dot(jnp.concatenate(ms, axis=1),
                               jnp.concatenate([x_lo, x_hi], axis=0)))
        h_old = hstate[g]
        y_off.append(_dot(cg, h_old.astype(BF16)))
        new_state = _dot(bg.T.astype(BF16), xw[:, g * gw:(g + 1) * gw])
        hstate[g] = h_old * cd_x[:, g * gw:(g + 1) * gw] + new_state

    y = (jnp.concatenate(y_diag, axis=1) + jnp.concatenate(y_off, axis=1) * ea_x
         + dskip_ref[...] * xs)
    y = y * _silu(z_ref[...])
    outs = []
    for g in range(SSD_GROUPS):
        outs.append(_rms(y[:, g * gw:(g + 1) * gw], ng_ref[:, g * gw:(g + 1) * gw]))
    y_ref[...] = jnp.concatenate(outs, axis=1).astype(y_ref.dtype)


def _ssd(xbc, dt, z, conv_w, conv_b, dt_bias, a_log, d_skip, norm_g):
    bsz, s, _ = xbc.shape
    q = SSD_CHUNK

    def tok(width):
        return pl.BlockSpec((None, q, width), lambda b, t: (b, t, 0))

    return pl.pallas_call(
        _ssd_kernel,
        grid=(bsz, s // q),
        in_specs=[tok(SSD_XBC), tok(DT_PAD), tok(SSD_INNER),
                  _const_spec((SSD_CONV, SSD_XBC)), _const_spec((1, SSD_XBC)),
                  _const_spec((1, DT_PAD)), _const_spec((1, DT_PAD)),
                  _const_spec((1, SSD_INNER)), _const_spec((1, SSD_INNER))],
        out_specs=tok(SSD_INNER),
        out_shape=jax.ShapeDtypeStruct((bsz, s, SSD_INNER), BF16),
        scratch_shapes=[
            pltpu.VMEM((_CONV_HALO + q, SSD_XBC), F32),
            pltpu.VMEM((SSD_GROUPS, SSD_STATE, SSD_INNER // SSD_GROUPS), F32),
            pltpu.VMEM((LANES, SSD_INNER), BF16),
            pltpu.VMEM((LANES, SSD_HEADS * LANES), BF16),
            pltpu.VMEM((q, q), BF16),
        ],
        compiler_params=pltpu.CompilerParams(
            dimension_semantics=("arbitrary", "arbitrary"), vmem_limit_bytes=VMEM_LIMIT),
        name="ssd_scan",
    )(xbc, dt, z, conv_w, conv_b, dt_bias, a_log, d_skip, norm_g)


def _bias_kernel(tab_ref, o_ref, *, t):
    h = pl.program_id(0)
    kk = lax.broadcasted_iota(jnp.int32, (2 * t, t), 0)
    qq = lax.broadcasted_iota(jnp.int32, (2 * t, t), 1)
    rel = qq - kk + t
    n = jnp.maximum(rel, 0)
    max_exact = REL_BUCKETS // 2
    nf = jnp.maximum(n, 1).astype(F32)
    large = max_exact + (jnp.log(nf / max_exact) / math.log(REL_MAX_DIST / max_exact)
                         * (REL_BUCKETS - max_exact)).astype(jnp.int32)
    large = jnp.minimum(large, REL_BUCKETS - 1)
    bucket = jnp.where(n < max_exact, n, large)
    far = tab_ref[REL_BUCKETS - 1, h]
    acc = jnp.zeros((2 * t, t), F32)
    for b in range(REL_BUCKETS - 1):
        acc = jnp.where(bucket == b, (tab_ref[b, h] - far) * LOG2E, acc)
    o_ref[...] = jnp.where(rel >= 0, acc, -jnp.inf)


def _bias_tiles(rel_bias, t):
    return pl.pallas_call(
        functools.partial(_bias_kernel, t=t),
        grid=(ATTN_HEADS,),
        in_specs=[pl.BlockSpec(memory_space=pltpu.SMEM)],
        out_specs=pl.BlockSpec((None, 2 * t, t), lambda h: (h, 0, 0)),
        out_shape=jax.ShapeDtypeStruct((ATTN_HEADS, 2 * t, t), F32),
        compiler_params=pltpu.CompilerParams(
            dimension_semantics=("arbitrary",), vmem_limit_bytes=VMEM_LIMIT),
        name="rel_bias_tiles",
    )(rel_bias)


def _attn_kernel(lq1_ref, lk1_ref, lq2_ref, lk2_ref, subg_ref, q_ref, k_ref, vt_ref, bias_ref,
                 o_ref, m_sc, acc_sc, s_a, s_b, *, t, lam_init):
    i = pl.program_id(2)
    lam = (jnp.exp(jnp.sum(lq1_ref[...] * lk1_ref[...], axis=-1, keepdims=True))
           - jnp.exp(jnp.sum(lq2_ref[...] * lk2_ref[...], axis=-1, keepdims=True))
           + lam_init)

    qs = q_ref[...]
    lane = lax.broadcasted_iota(jnp.int32, (t, LANES), 1)
    zero = jnp.zeros_like(qs)
    qz = (jnp.where(lane < ATTN_QK_DIM, qs, zero), jnp.where(lane < ATTN_QK_DIM, zero, qs))

    def qk(j, dst):
        kt = k_ref[pl.ds(pl.multiple_of(j * t, t), t), :]
        for mp in range(2):
            dst[mp] = _dot_nt(kt, qz[mp])

    def softmax_pv(src, j, bias):
        for mp in range(2):
            s = src[mp]
            if bias is not None:
                s = s + bias
            m_prev = m_sc[mp]
            m_new = jnp.maximum(m_prev, jnp.max(s, axis=0, keepdims=True))
            alpha = jnp.exp2(m_prev - m_new)
            p = jnp.exp2(s - m_new).astype(BF16)
            acc_sc[mp] = alpha * acc_sc[mp] + _dot(vt_ref[j], p)
            m_sc[mp] = m_new

    m_sc[...] = jnp.full(m_sc.shape, -jnp.inf, F32)
    acc_sc[...] = jnp.zeros(acc_sc.shape, F32)
    bias_prev = bias_ref[0:t, :]
    bias_diag = bias_ref[t:2 * t, :]
    n_far = jnp.maximum(i - 1, 0)
    qk(0, s_a)

    def far_pair(jj, carry):
        j = 2 * jj
        qk(j + 1, s_b)
        softmax_pv(s_a, j, None)
        qk(j + 2, s_a)
        softmax_pv(s_b, j + 1, None)
        return carry

    lax.fori_loop(0, n_far // 2, far_pair, 0)

    @pl.when(i == 0)
    def _only_diag():
        softmax_pv(s_a, i, bias_diag)

    @pl.when(jnp.logical_and(i > 0, n_far % 2 == 0))
    def _prev_diag():
        qk(i, s_b)
        softmax_pv(s_a, i - 1, bias_prev)
        softmax_pv(s_b, i, bias_diag)

    @pl.when(n_far % 2 == 1)
    def _far_prev_diag():
        qk(i - 1, s_b)
        softmax_pv(s_a, i - 2, None)
        qk(i, s_a)
        softmax_pv(s_b, i - 1, bias_prev)
        softmax_pv(s_a, i, bias_diag)

    o1 = acc_sc[0, 0:ATTN_V_DIM, :] / acc_sc[0, ATTN_V_DIM:ATTN_V_DIM + 1, :]
    o2 = acc_sc[1, 0:ATTN_V_DIM, :] / acc_sc[1, ATTN_V_DIM:ATTN_V_DIM + 1, :]
    o = (o1 - lam * o2).T
    o = _rms(o, subg_ref[...]) * (1.0 - lam_init)
    o_ref[...] = o.astype(o_ref.dtype)


def _attention(q, k, vt, bias, lq1, lk1, lq2, lk2, subg, t, lam_init):
    bsz, s, _ = q.shape
    small = lambda shape: pl.BlockSpec(shape, lambda b, h, i: (0, 0))
    return pl.pallas_call(
        functools.partial(_attn_kernel, t=t, lam_init=lam_init),
        grid=(bsz, ATTN_HEADS, s // t),
        in_specs=[
            small((1, ATTN_QK_DIM)), small((1, ATTN_QK_DIM)),
            small((1, ATTN_QK_DIM)), small((1, ATTN_QK_DIM)),
            small((1, ATTN_V_DIM)),
            pl.BlockSpec((None, t, LANES), lambda b, h, i: (b, i, h)),
            pl.BlockSpec((None, s, LANES), lambda b, h, i: (b, 0, h)),
            pl.BlockSpec((None, s // t, None, V_AUG, t), lambda b, h, i: (b, 0, h, 0, 0)),
            pl.BlockSpec((None, 2 * t, t), lambda b, h, i: (h, 0, 0)),
        ],
        out_specs=pl.BlockSpec((None, t, LANES), lambda b, h, i: (b, i, h)),
        out_shape=jax.ShapeDtypeStruct((bsz, s, ATTN_WIDTH), BF16),
        scratch_shapes=[
            pltpu.VMEM((2, 1, t), F32),
            pltpu.VMEM((2, V_AUG, t), F32),
            pltpu.VMEM((2, t, t), F32),
            pltpu.VMEM((2, t, t), F32),
        ],
        compiler_params=pltpu.CompilerParams(
            dimension_semantics=("arbitrary", "arbitrary", "arbitrary"),
            vmem_limit_bytes=VMEM_LIMIT),
        name="diff_attention",
    )(lq1, lk1, lq2, lk2, subg, q, k, vt, bias)


def _outproj_kernel(x_ref, ys_ref, ya_ref, gt_ref, post_ref, w_ref, o_ref):
    k_ssd = ys_ref.shape[-1]
    m = _dot(ys_ref[...], w_ref[0:k_ssd, :]) + _dot(ya_ref[...], w_ref[k_ssd:, :])
    o_ref[...] = x_ref[...] + gt_ref[...] * _rms(m, post_ref[...])


def _outproj(x, y_ssd, y_attn, mod4, post_g, w_out, tm):
    bsz, s, d = x.shape

    def tok(width):
        return pl.BlockSpec((None, tm, width), lambda b, t: (b, t, 0))

    return pl.pallas_call(
        _outproj_kernel,
        grid=(bsz, s // tm),
        in_specs=[tok(d), tok(SSD_INNER), tok(ATTN_WIDTH), _mod_spec(5, d),
                  _const_spec((1, d)), _const_spec((SSD_INNER + ATTN_WIDTH, d))],
        out_specs=tok(d),
        out_shape=jax.ShapeDtypeStruct((bsz, s, d), F32),
        compiler_params=pltpu.CompilerParams(
            dimension_semantics=("arbitrary", "arbitrary"), vmem_limit_bytes=VMEM_LIMIT),
        name="mixer_out_proj",
    )(x, y_ssd, y_attn, mod4, post_g, w_out)


def _tiles(s):
    return dict(
        tm_ffn=min(512, s),
        ff_chunk=1408,
        t_attn=min(512, s),
        tn_mod=1152,
    )


def kernel(x, c, w_ada, b_ada, ffn1_pre_g, ffn1_post_g, ffn1_w_gate, ffn1_w_up, ffn1_w_down, mix_pre_g, mix_post_g, w_in, conv_w, conv_b, dt_bias, a_log, d_skip, ssd_norm_g, lambda_q1, lambda_k1, lambda_q2, lambda_k2, subln_g, w_out, ffn2_pre_g, ffn2_post_g, ffn2_w_gate, ffn2_w_up, ffn2_w_down, rel_bias):
    bsz, s, d = x.shape
    depth = w_ada.shape[0]
    cfg = _tiles(s)
    row = lambda v: v.reshape(1, -1)

    c_pad = jnp.zeros((8, d), F32).at[:bsz].set(c)
    bias = _bias_tiles(rel_bias, cfg["t_attn"])

    for l in range(depth):
        mod = _modulation(c_pad, w_ada[l], row(b_ada[l]), cfg["tn_mod"])
        mod4 = mod[:bsz].reshape(bsz, N_MOD, 1, d)

        x = _ffn(x, mod4, 0, row(ffn1_pre_g[l]), row(ffn1_post_g[l]),
                 ffn1_w_gate[l].astype(BF16), ffn1_w_up[l].astype(BF16),
                 ffn1_w_down[l].astype(BF16), cfg["tm_ffn"], cfg["ff_chunk"])

        w = w_in[l]
        sizes = [SSD_INNER, SSD_XBC, SSD_HEADS, ATTN_WIDTH, ATTN_WIDTH, ATTN_WIDTH]
        offs = [0]
        for sz in sizes:
            offs.append(offs[-1] + sz)
        w_pad = jnp.concatenate(
            [w[:, offs[0]:offs[3]], jnp.zeros((d, DT_PAD - SSD_HEADS), w.dtype), w[:, offs[3]:]],
            axis=1).astype(BF16)
        z, xbc, dt, q, k, vt = _inproj(x, mod4, row(mix_pre_g[l]), w_pad, cfg["t_attn"])

        pad16 = lambda vec: jnp.zeros((1, DT_PAD), F32).at[0, :SSD_HEADS].set(vec)
        y_ssd = _ssd(xbc, dt, z, conv_w[l], row(conv_b[l]), pad16(dt_bias[l]), pad16(a_log[l]),
                     row(jnp.repeat(d_skip[l], SSD_HEAD_DIM)), row(ssd_norm_g[l]))

        lam_init = 0.8 - 0.6 * math.exp(-0.3 * l)
        y_attn = _attention(q, k, vt, bias, row(lambda_q1[l]), row(lambda_k1[l]),
                            row(lambda_q2[l]), row(lambda_k2[l]), row(subln_g[l]),
                            cfg["t_attn"], lam_init)

        x = _outproj(x, y_ssd, y_attn, mod4, row(mix_post_g[l]), w_out[l].astype(BF16),
                     cfg["tm_ffn"])

        x = _ffn(x, mod4, 6, row(ffn2_pre_g[l]), row(ffn2_post_g[l]),
                 ffn2_w_gate[l].astype(BF16), ffn2_w_up[l].astype(BF16),
                 ffn2_w_down[l].astype(BF16), cfg["tm_ffn"], cfg["ff_chunk"])
    return x
```

```python
import functools
import math

import jax
import jax.numpy as jnp
from jax import lax
from jax.experimental import pallas as pl
from jax.experimental.pallas import tpu as pltpu

F32 = jnp.float32
BF16 = jnp.bfloat16

LANES = 128
EPS = 1e-6

SSD_HEADS = 16
SSD_HEAD_DIM = 64
SSD_INNER = SSD_HEADS * SSD_HEAD_DIM
SSD_GROUPS = 2
SSD_STATE = 128
SSD_CONV = 4
SSD_CHUNK = 128
SSD_XBC = SSD_INNER + 2 * SSD_GROUPS * SSD_STATE
ATTN_HEADS = 8
ATTN_QK_DIM = 64
ATTN_V_DIM = 128
ATTN_WIDTH = ATTN_HEADS * ATTN_V_DIM
REL_BUCKETS = 32
REL_MAX_DIST = 128
N_MOD = 9
DT_PAD = LANES
BF16_SUBLANES = 16
V_AUG = ATTN_V_DIM + BF16_SUBLANES
LOG2E = 1.4426950408889634

VMEM_LIMIT = 56 * 1024 * 1024


def _dot(a, b):
    return jnp.dot(a, b, preferred_element_type=F32)


def _dot_nt(a, b):
    return lax.dot_general(a, b, (((1,), (1,)), ((), ())), preferred_element_type=F32)


def _rms(x, g):
    return x * lax.rsqrt(jnp.mean(x * x, axis=-1, keepdims=True) + EPS) * g


def _silu(x):
    return x / (1.0 + jnp.exp(-x))


def _split_bf16(v):
    hi = v.astype(BF16)
    lo = (v - hi.astype(F32)).astype(BF16)
    return hi, lo


def _mod_kernel(c_ref, w_ref, b_ref, o_ref):
    cs = _silu(c_ref[...]).astype(BF16)
    o_ref[...] = _dot(cs, w_ref[...].astype(BF16)) + b_ref[...]


def _modulation(c_pad, w_ada, b_ada, tn):
    rows, d = c_pad.shape
    n = w_ada.shape[1]
    return pl.pallas_call(
        _mod_kernel,
        grid=(n // tn,),
        in_specs=[
            pl.BlockSpec((rows, d), lambda j: (0, 0)),
            pl.BlockSpec((d, tn), lambda j: (0, j)),
            pl.BlockSpec((1, tn), lambda j: (0, j)),
        ],
        out_specs=pl.BlockSpec((rows, tn), lambda j: (0, j)),
        out_shape=jax.ShapeDtypeStruct((rows, n), F32),
        compiler_params=pltpu.CompilerParams(
            dimension_semantics=("arbitrary",), vmem_limit_bytes=VMEM_LIMIT),
        name="adaln_mod",
    )(c_pad, w_ada, b_ada)


def _mod_spec(idx, d):
    return pl.BlockSpec((None, None, 1, d), lambda b, t: (b, idx, 0, 0))


def _const_spec(shape):
    return pl.BlockSpec(shape, lambda b, t: tuple(0 for _ in shape))


def _ffn_kernel(x_ref, sh_ref, sc_ref, gt_ref, pre_ref, post_ref, wg_ref, wu_ref, wd_ref,
                o_ref, *, ff_chunk):
    x = x_ref[...]
    h = (_rms(x, pre_ref[...]) * (1.0 + sc_ref[...]) + sh_ref[...]).astype(BF16)
    d_ff = wg_ref.shape[1]
    y = None
    for c0 in range(0, d_ff, ff_chunk):
        g = _dot(h, wg_ref[:, c0:c0 + ff_chunk])
        u = _dot(h, wu_ref[:, c0:c0 + ff_chunk])
        a = (_silu(g) * u).astype(BF16)
        part = _dot(a, wd_ref[c0:c0 + ff_chunk, :])
        y = part if y is None else y + part
    o_ref[...] = x + (0.5 * gt_ref[...]) * _rms(y, post_ref[...])


def _ffn(x, mod4, mod_base, pre_g, post_g, wg, wu, wd, tm, ff_chunk):
    bsz, s, d = x.shape
    d_ff = wg.shape[1]
    tok = pl.BlockSpec((None, tm, d), lambda b, t: (b, t, 0))
    return pl.pallas_call(
        functools.partial(_ffn_kernel, ff_chunk=ff_chunk),
        grid=(bsz, s // tm),
        in_specs=[
            tok,
            _mod_spec(mod_base, d), _mod_spec(mod_base + 1, d), _mod_spec(mod_base + 2, d),
            _const_spec((1, d)), _const_spec((1, d)),
            _const_spec((d, d_ff)), _const_spec((d, d_ff)), _const_spec((d_ff, d)),
        ],
        out_specs=tok,
        out_shape=jax.ShapeDtypeStruct((bsz, s, d), F32),
        compiler_params=pltpu.CompilerParams(
            dimension_semantics=("arbitrary", "arbitrary"), vmem_limit_bytes=VMEM_LIMIT),
        name="ffn_half_step",
    )(x, mod4, mod4, mod4, pre_g, post_g, wg, wu, wd)


_Z0 = 0
_XBC0 = _Z0 + SSD_INNER
_DT0 = _XBC0 + SSD_XBC
_Q0 = _DT0 + DT_PAD
_K0 = _Q0 + ATTN_WIDTH
_V0 = _K0 + ATTN_WIDTH
_IN_PAD = _V0 + ATTN_WIDTH


def _inproj_kernel(x_ref, sh_ref, sc_ref, pre_ref, w_ref,
                   z_ref, xbc_ref, dt_ref, q1_ref, q2_ref, k_ref, vt_ref):
    h = (_rms(x_ref[...], pre_ref[...]) * (1.0 + sc_ref[...]) + sh_ref[...]).astype(BF16)
    z_ref[...] = _dot(h, w_ref[:, _Z0:_XBC0])
    xbc_ref[...] = _dot(h, w_ref[:, _XBC0:_DT0])
    dt_ref[...] = _dot(h, w_ref[:, _DT0:_Q0])
    q = (_dot(h, w_ref[:, _Q0:_K0]) * (ATTN_QK_DIM ** -0.5 * LOG2E)).astype(BF16)
    lane = lax.broadcasted_iota(jnp.int32, q.shape, 1)
    is_map1 = lane % (2 * ATTN_QK_DIM) < ATTN_QK_DIM
    zero = jnp.zeros_like(q)
    q1_ref[...] = jnp.where(is_map1, q, zero)
    q2_ref[...] = jnp.where(is_map1, zero, q)
    k_ref[...] = _dot(h, w_ref[:, _K0:_V0]).astype(BF16)
    ones = jnp.ones((V_AUG - ATTN_V_DIM, x_ref.shape[0]), BF16)
    for hh in range(ATTN_HEADS):
        v_h = _dot(h, w_ref[:, _V0 + hh * ATTN_V_DIM:_V0 + (hh + 1) * ATTN_V_DIM])
        vt_ref[hh, 0:ATTN_V_DIM, :] = v_h.T.astype(BF16)
        vt_ref[hh, ATTN_V_DIM:V_AUG, :] = ones


def _inproj(x, mod4, pre_g, w_pad, tm):
    bsz, s, d = x.shape

    def tok(width):
        return pl.BlockSpec((None, tm, width), lambda b, t: (b, t, 0))

    def out(width, dtype):
        return jax.ShapeDtypeStruct((bsz, s, width), dtype)

    return pl.pallas_call(
        _inproj_kernel,
        grid=(bsz, s // tm),
        in_specs=[tok(d), _mod_spec(3, d), _mod_spec(4, d), _const_spec((1, d)),
                  _const_spec((d, _IN_PAD))],
        out_specs=[tok(SSD_INNER), tok(SSD_XBC), tok(DT_PAD),
                   tok(ATTN_WIDTH), tok(ATTN_WIDTH), tok(ATTN_WIDTH),
                   pl.BlockSpec((None, None, ATTN_HEADS, V_AUG, tm),
                                lambda b, t: (b, t, 0, 0, 0))],
        out_shape=[out(SSD_INNER, F32), out(SSD_XBC, F32), out(DT_PAD, F32),
                   out(ATTN_WIDTH, BF16), out(ATTN_WIDTH, BF16), out(ATTN_WIDTH, BF16),
                   jax.ShapeDtypeStruct((bsz, s // tm, ATTN_HEADS, V_AUG, tm), BF16)],
        compiler_params=pltpu.CompilerParams(
            dimension_semantics=("arbitrary", "arbitrary"), vmem_limit_bytes=VMEM_LIMIT),
        name="mixer_in_proj",
    )(x, mod4, mod4, pre_g, w_pad)


_CONV_HALO = 8


def _ssd_kernel(xbc_ref, dt_ref, z_ref, cw_ref, cb_ref, dtb_ref, alog_ref, dskip_ref, ng_ref,
                y_ref, convbuf, hstate, e64, e128, tril):
    q = SSD_CHUNK
    gw = SSD_INNER // SSD_GROUPS
    hpg = SSD_HEADS // SSD_GROUPS

    @pl.when(pl.program_id(1) == 0)
    def _init():
        convbuf[0:_CONV_HALO, :] = jnp.zeros((_CONV_HALO, SSD_XBC), F32)
        hstate[...] = jnp.zeros(hstate.shape, F32)
        r = lax.broadcasted_iota(jnp.int32, (LANES, SSD_INNER), 0)
        c = lax.broadcasted_iota(jnp.int32, (LANES, SSD_INNER), 1)
        e64[...] = (r == c // SSD_HEAD_DIM).astype(BF16)
        r = lax.broadcasted_iota(jnp.int32, (LANES, SSD_HEADS * LANES), 0)
        c = lax.broadcasted_iota(jnp.int32, (LANES, SSD_HEADS * LANES), 1)
        e128[...] = (r == c // LANES).astype(BF16)
        r = lax.broadcasted_iota(jnp.int32, (q, q), 0)
        c = lax.broadcasted_iota(jnp.int32, (q, q), 1)
        tril[...] = (r >= c).astype(BF16)

    u = xbc_ref[...]
    convbuf[_CONV_HALO:_CONV_HALO + q, :] = u
    acc = cb_ref[...] + cw_ref[0:1, :] * convbuf[pl.ds(_CONV_HALO - SSD_CONV + 1, q), :]
    for k in range(1, SSD_CONV):
        acc = acc + cw_ref[k:k + 1, :] * convbuf[pl.ds(_CONV_HALO - SSD_CONV + 1 + k, q), :]
    convbuf[0:_CONV_HALO, :] = u[q - _CONV_HALO:q, :]
    xc = _silu(acc)
    xs = xc[:, :SSD_INNER]
    bm = xc[:, SSD_INNER:SSD_INNER + SSD_GROUPS * SSD_STATE]
    cm = xc[:, SSD_INNER + SSD_GROUPS * SSD_STATE:]

    dt_in = dt_ref[...] + dtb_ref[...]
    dtv = jnp.maximum(dt_in, 0.0) + jnp.log1p(jnp.exp(-jnp.abs(dt_in)))
    d_a = dtv * (-jnp.exp(alog_ref[...]))
    da_hi, da_lo = _split_bf16(d_a)
    acum = _dot(tril[...], da_hi) + _dot(tril[...], da_lo)
    acum_t = acum.T

    def expand(v, e_ref):
        hi, lo = _split_bf16(v)
        return _dot(hi, e_ref[...]) + _dot(lo, e_ref[...])

    dt_x = expand(dtv, e64)
    acum_x = expand(acum, e64)
    acum_x128 = expand(acum, e128)
    alast_x = acum_x[q - 1:q, :]
    ea_x = jnp.exp(acum_x)
    dte_x = jnp.exp(alast_x - acum_x)
    cd_x = jnp.exp(alast_x)

    xdt = xs * dt_x
    xw = (xdt * dte_x).astype(BF16)
    causal = (lax.broadcasted_iota(jnp.int32, (q, q), 0)
              >= lax.broadcasted_iota(jnp.int32, (q, q), 1))
    lane = lax.broadcasted_iota(jnp.int32, (q, LANES), 1)
    low_half = lane < SSD_HEAD_DIM

    y_diag = []
    y_off = []
    for g in range(SSD_GROUPS):
        bg = bm[:, g * SSD_STATE:(g + 1) * SSD_STATE]
        cg = cm[:, g * SSD_STATE:(g + 1) * SSD_STATE].astype(BF16)
        cb = _dot_nt(cg, bg.astype(BF16))
        for pr in range(hpg // 2):
            blk = g * (hpg // 2) + pr
            ms = []
            for hh in (2 * blk, 2 * blk + 1):
                seg = acum_x128[:, hh * LANES:(hh + 1) * LANES] - acum_t[hh:hh + 1, :]
                decay = jnp.exp(jnp.where(causal, seg, -jnp.inf))
                ms.append((cb * decay).astype(BF16))
            xb = xdt[:, blk * LANES:(blk + 1) * LANES]
            x_lo = jnp.where(low_half, xb, 0.0).astype(BF16)
            x_hi = jnp.where(low_half, 0.0, xb).astype(BF16)
            y_diag.append(_dot(jnp.concatenate(ms, axis=1),
                               jnp.concatenate([x_lo, x_hi], axis=0)))
        h_old = hstate[g]
        y_off.append(_dot(cg, h_old.astype(BF16)))
        new_state = _dot(bg.T.astype(BF16), xw[:, g * gw:(g + 1) * gw])
        hstate[g] = h_old * cd_x[:, g * gw:(g + 1) * gw] + new_state

    y = (jnp.concatenate(y_diag, axis=1) + jnp.concatenate(y_off, axis=1) * ea_x
         + dskip_ref[...] * xs)
    y = y * _silu(z_ref[...])
    outs = []
    for g in range(SSD_GROUPS):
        outs.append(_rms(y[:, g * gw:(g + 1) * gw], ng_ref[:, g * gw:(g + 1) * gw]))
    y_ref[...] = jnp.concatenate(outs, axis=1).astype(y_ref.dtype)


def _ssd(xbc, dt, z, conv_w, conv_b, dt_bias, a_log, d_skip, norm_g):
    bsz, s, _ = xbc.shape
    q = SSD_CHUNK

    def tok(width):
        return pl.BlockSpec((None, q, width), lambda b, t: (b, t, 0))

    return pl.pallas_call(
        _ssd_kernel,
        grid=(bsz, s // q),
        in_specs=[tok(SSD_XBC), tok(DT_PAD), tok(SSD_INNER),
                  _const_spec((SSD_CONV, SSD_XBC)), _const_spec((1, SSD_XBC)),
                  _const_spec((1, DT_PAD)), _const_spec((1, DT_PAD)),
                  _const_spec((1, SSD_INNER)), _const_spec((1, SSD_INNER))],
        out_specs=tok(SSD_INNER),
        out_shape=jax.ShapeDtypeStruct((bsz, s, SSD_INNER), BF16),
        scratch_shapes=[
            pltpu.VMEM((_CONV_HALO + q, SSD_XBC), F32),
            pltpu.VMEM((SSD_GROUPS, SSD_STATE, SSD_INNER // SSD_GROUPS), F32),
            pltpu.VMEM((LANES, SSD_INNER), BF16),
            pltpu.VMEM((LANES, SSD_HEADS * LANES), BF16),
            pltpu.VMEM((q, q), BF16),
        ],
        compiler_params=pltpu.CompilerParams(
            dimension_semantics=("arbitrary", "arbitrary"), vmem_limit_bytes=VMEM_LIMIT),
        name="ssd_scan",
    )(xbc, dt, z, conv_w, conv_b, dt_bias, a_log, d_skip, norm_g)


def _bias_kernel(tab_ref, o_ref, *, t):
    h = pl.program_id(0)
    kk = lax.broadcasted_iota(jnp.int32, (2 * t, t), 0)
    qq = lax.broadcasted_iota(jnp.int32, (2 * t, t), 1)
    rel = qq - kk + t
    n = jnp.maximum(rel, 0)
    max_exact = REL_BUCKETS // 2
    nf = jnp.maximum(n, 1).astype(F32)
    large = max_exact + (jnp.log(nf / max_exact) / math.log(REL_MAX_DIST / max_exact)
                         * (REL_BUCKETS - max_exact)).astype(jnp.int32)
    large = jnp.minimum(large, REL_BUCKETS - 1)
    bucket = jnp.where(n < max_exact, n, large)
    far = tab_ref[REL_BUCKETS - 1, h]
    acc = jnp.zeros((2 * t, t), F32)
    for b in range(REL_BUCKETS - 1):
        acc = jnp.where(bucket == b, (tab_ref[b, h] - far) * LOG2E, acc)
    o_ref[...] = jnp.where(rel >= 0, acc, -jnp.inf)


def _bias_tiles(rel_bias, t):
    return pl.pallas_call(
        functools.partial(_bias_kernel, t=t),
        grid=(ATTN_HEADS,),
        in_specs=[pl.BlockSpec(memory_space=pltpu.SMEM)],
        out_specs=pl.BlockSpec((None, 2 * t, t), lambda h: (h, 0, 0)),
        out_shape=jax.ShapeDtypeStruct((ATTN_HEADS, 2 * t, t), F32),
        compiler_params=pltpu.CompilerParams(
            dimension_semantics=("arbitrary",), vmem_limit_bytes=VMEM_LIMIT),
        name="rel_bias_tiles",
    )(rel_bias)


def _attn_kernel(lq1_ref, lk1_ref, lq2_ref, lk2_ref, subg_ref, q1_ref, q2_ref, k_ref, vt_ref,
                 band_ref, o_ref, m_sc, acc_sc, s_a, s_b, bias_sc, *, tq, lam_init):
    tk = tq // 2
    nq = q1_ref.shape[0] // tq
    q_refs = (q1_ref, q2_ref)
    lam = (jnp.exp(jnp.sum(lq1_ref[...] * lk1_ref[...], axis=-1, keepdims=True))
           - jnp.exp(jnp.sum(lq2_ref[...] * lk2_ref[...], axis=-1, keepdims=True))
           + lam_init)

    for r0 in range(0, 3 * tk, LANES):
        for c0 in range(0, tq, LANES):
            rel0 = c0 - r0 + tk
            if rel0 < 0:
                blk = jnp.full((LANES, LANES), -jnp.inf, F32)
            elif rel0 == 0:
                blk = band_ref[LANES:2 * LANES, :]
            elif rel0 == LANES:
                blk = band_ref[0:LANES, :]
            else:
                blk = jnp.zeros((LANES, LANES), F32)
            bias_sc[r0:r0 + LANES, c0:c0 + LANES] = blk

    def qk(i, j, dst, c0=0):
        kt = k_ref[pl.ds(pl.multiple_of(j * tk, tk), tk), :]
        for mp in range(2):
            qt = q_refs[mp][pl.ds(pl.multiple_of(i * tq + c0, tq - c0), tq - c0), :]
            dst[mp, :, c0:tq] = _dot_nt(kt, qt)

    def softmax_pv(src, j, bias, c0=0):
        for mp in range(2):
            s = src[mp, :, c0:tq]
            if bias is not None:
                s = s + bias[:, c0:tq]
            m_prev = m_sc[mp, :, c0:tq]
            m_new = jnp.maximum(m_prev, jnp.max(s, axis=0, keepdims=True))
            alpha = jnp.exp2(m_prev - m_new)
            p = jnp.exp2(s - m_new).astype(BF16)
            acc_sc[mp, :, c0:tq] = alpha * acc_sc[mp, :, c0:tq] + _dot(vt_ref[j], p)
            m_sc[mp, :, c0:tq] = m_new

    qk(0, 0, s_a)

    def q_tile(i, carry):
        m_sc[...] = jnp.full(m_sc.shape, -jnp.inf, F32)
        acc_sc[...] = jnp.zeros(acc_sc.shape, F32)

        def far_pair(jj, c):
            j = 2 * jj
            qk(i, j + 1, s_b)
            softmax_pv(s_a, j, None)
            qk(i, j + 2, s_a)
            softmax_pv(s_b, j + 1, None)
            return c

        lax.fori_loop(0, i - 1, far_pair, 0)

        @pl.when(i > 0)
        def _far_prev():
            qk(i, 2 * i - 1, s_b)
            softmax_pv(s_a, 2 * i - 2, None)
            qk(i, 2 * i, s_a)
            softmax_pv(s_b, 2 * i - 1, bias_sc[0:tk, :])

        qk(i, 2 * i + 1, s_b, c0=tk)
        softmax_pv(s_a, 2 * i, bias_sc[tk:2 * tk, :])
        qk(jnp.minimum(i + 1, nq - 1), 0, s_a)
        softmax_pv(s_b, 2 * i + 1, bias_sc[2 * tk:3 * tk, :], c0=tk)

        o1 = acc_sc[0, 0:ATTN_V_DIM, :] / acc_sc[0, ATTN_V_DIM:ATTN_V_DIM + 1, :]
        o2 = acc_sc[1, 0:ATTN_V_DIM, :] / acc_sc[1, ATTN_V_DIM:ATTN_V_DIM + 1, :]
        o = (o1 - lam * o2).T
        o = _rms(o, subg_ref[...]) * (1.0 - lam_init)
        o_ref[pl.ds(pl.multiple_of(i * tq, tq), tq), :] = o.astype(o_ref.dtype)
        return carry

    lax.fori_loop(0, nq, q_tile, 0)


def _attention(q1, q2, k, vt, band, lq1, lk1, lq2, lk2, subg, tq, lam_init):
    bsz, s, _ = q1.shape
    tk = tq // 2
    small = lambda shape: pl.BlockSpec(shape, lambda b, h: (0, 0))
    head_cols = pl.BlockSpec((None, s, LANES), lambda b, h: (b, 0, h))
    return pl.pallas_call(
        functools.partial(_attn_kernel, tq=tq, lam_init=lam_init),
        grid=(bsz, ATTN_HEADS),
        in_specs=[
            small((1, ATTN_QK_DIM)), small((1, ATTN_QK_DIM)),
            small((1, ATTN_QK_DIM)), small((1, ATTN_QK_DIM)),
            small((1, ATTN_V_DIM)),
            head_cols, head_cols, head_cols,
            pl.BlockSpec((None, s // tk, None, V_AUG, tk), lambda b, h: (b, 0, h, 0, 0)),
            pl.BlockSpec((None, 2 * LANES, LANES), lambda b, h: (h, 0, 0)),
        ],
        out_specs=head_cols,
        out_shape=jax.ShapeDtypeStruct((bsz, s, ATTN_WIDTH), BF16),
        scratch_shapes=[
            pltpu.VMEM((2, 1, tq), F32),
            pltpu.VMEM((2, V_AUG, tq), F32),
            pltpu.VMEM((2, tk, tq), F32),
            pltpu.VMEM((2, tk, tq), F32),
            pltpu.VMEM((3 * tk, tq), F32),
        ],
        compiler_params=pltpu.CompilerParams(
            dimension_semantics=("arbitrary", "arbitrary"),
            vmem_limit_bytes=VMEM_LIMIT),
        name="diff_attention",
    )(lq1, lk1, lq2, lk2, subg, q1, q2, k, vt, band)


def _outproj_kernel(x_ref, ys_ref, ya_ref, gt_ref, post_ref, w_ref, o_ref):
    k_ssd = ys_ref.shape[-1]
    m = _dot(ys_ref[...], w_ref[0:k_ssd, :]) + _dot(ya_ref[...], w_ref[k_ssd:, :])
    o_ref[...] = x_ref[...] + gt_ref[...] * _rms(m, post_ref[...])


def _outproj(x, y_ssd, y_attn, mod4, post_g, w_out, tm):
    bsz, s, d = x.shape

    def tok(width):
        return pl.BlockSpec((None, tm, width), lambda b, t: (b, t, 0))

    return pl.pallas_call(
        _outproj_kernel,
        grid=(bsz, s // tm),
        in_specs=[tok(d), tok(SSD_INNER), tok(ATTN_WIDTH), _mod_spec(5, d),
                  _const_spec((1, d)), _const_spec((SSD_INNER + ATTN_WIDTH, d))],
        out_specs=tok(d),
        out_shape=jax.ShapeDtypeStruct((bsz, s, d), F32),
        compiler_params=pltpu.CompilerParams(
            dimension_semantics=("arbitrary", "arbitrary"), vmem_limit_bytes=VMEM_LIMIT),
        name="mixer_out_proj",
    )(x, y_ssd, y_attn, mod4, post_g, w_out)


def _tiles(s):
    return dict(
        tm_ffn=min(512, s),
        ff_chunk=1408,
        tq_attn=min(1024, s),
        tn_mod=1152,
    )


def kernel(x, c, w_ada, b_ada, ffn1_pre_g, ffn1_post_g, ffn1_w_gate, ffn1_w_up, ffn1_w_down, mix_pre_g, mix_post_g, w_in, conv_w, conv_b, dt_bias, a_log, d_skip, ssd_norm_g, lambda_q1, lambda_k1, lambda_q2, lambda_k2, subln_g, w_out, ffn2_pre_g, ffn2_post_g, ffn2_w_gate, ffn2_w_up, ffn2_w_down, rel_bias):
    bsz, s, d = x.shape
    depth = w_ada.shape[0]
    cfg = _tiles(s)
    row = lambda v: v.reshape(1, -1)

    c_pad = jnp.zeros((8, d), F32).at[:bsz].set(c)
    band = _bias_tiles(rel_bias, LANES)

    for l in range(depth):
        mod = _modulation(c_pad, w_ada[l], row(b_ada[l]), cfg["tn_mod"])
        mod4 = mod[:bsz].reshape(bsz, N_MOD, 1, d)

        x = _ffn(x, mod4, 0, row(ffn1_pre_g[l]), row(ffn1_post_g[l]),
                 ffn1_w_gate[l].astype(BF16), ffn1_w_up[l].astype(BF16),
                 ffn1_w_down[l].astype(BF16), cfg["tm_ffn"], cfg["ff_chunk"])

        w = w_in[l]
        sizes = [SSD_INNER, SSD_XBC, SSD_HEADS, ATTN_WIDTH, ATTN_WIDTH, ATTN_WIDTH]
        offs = [0]
        for sz in sizes:
            offs.append(offs[-1] + sz)
        w_pad = jnp.concatenate(
            [w[:, offs[0]:offs[3]], jnp.zeros((d, DT_PAD - SSD_HEADS), w.dtype), w[:, offs[3]:]],
            axis=1).astype(BF16)
        z, xbc, dt, q1, q2, k, vt = _inproj(x, mod4, row(mix_pre_g[l]), w_pad,
                                            cfg["tq_attn"] // 2)

        pad16 = lambda vec: jnp.zeros((1, DT_PAD), F32).at[0, :SSD_HEADS].set(vec)
        y_ssd = _ssd(xbc, dt, z, conv_w[l], row(conv_b[l]), pad16(dt_bias[l]), pad16(a_log[l]),
                     row(jnp.repeat(d_skip[l], SSD_HEAD_DIM)), row(ssd_norm_g[l]))

        lam_init = 0.8 - 0.6 * math.exp(-0.3 * l)
        y_attn = _attention(q1, q2, k, vt, band, row(lambda_q1[l]), row(lambda_k1[l]),
                            row(lambda_q2[l]), row(lambda_k2[l]), row(subln_g[l]),
                            cfg["tq_attn"], lam_init)

        x = _outproj(x, y_ssd, y_attn, mod4, row(mix_post_g[l]), w_out[l].astype(BF16),
                     cfg["tm_ffn"])

        x = _ffn(x, mod4, 6, row(ffn2_pre_g[l]), row(ffn2_post_g[l]),
                 ffn2_w_gate[l].astype(BF16), ffn2_w_up[l].astype(BF16),
                 ffn2_w_down[l].astype(BF16), cfg["tm_ffn"], cfg["ff_chunk"])
    return x
```

```python
import functools
import math

import jax
import jax.numpy as jnp
from jax import lax
from jax.experimental import pallas as pl
from jax.experimental.pallas import tpu as pltpu

F32 = jnp.float32
BF16 = jnp.bfloat16

LANES = 128
EPS = 1e-6

SSD_HEADS = 16
SSD_HEAD_DIM = 64
SSD_INNER = SSD_HEADS * SSD_HEAD_DIM
SSD_GROUPS = 2
SSD_STATE = 128
SSD_CONV = 4
SSD_CHUNK = 128
SSD_XBC = SSD_INNER + 2 * SSD_GROUPS * SSD_STATE
ATTN_HEADS = 8
ATTN_QK_DIM = 64
ATTN_V_DIM = 128
ATTN_WIDTH = ATTN_HEADS * ATTN_V_DIM
REL_BUCKETS = 32
REL_MAX_DIST = 128
N_MOD = 9
DT_PAD = LANES
BF16_SUBLANES = 16
V_AUG = ATTN_V_DIM + BF16_SUBLANES
LOG2E = 1.4426950408889634

VMEM_LIMIT = 56 * 1024 * 1024


def _dot(a, b):
    return jnp.dot(a, b, preferred_element_type=F32)


def _dot_nt(a, b):
    return lax.dot_general(a, b, (((1,), (1,)), ((), ())), preferred_element_type=F32)


def _rms(x, g):
    return x * lax.rsqrt(jnp.mean(x * x, axis=-1, keepdims=True) + EPS) * g


def _silu(x):
    half = 0.5 * x
    return half + half * jnp.tanh(half)


def _split_bf16(v):
    hi = v.astype(BF16)
    lo = (v - hi.astype(F32)).astype(BF16)
    return hi, lo


def _mod_kernel(c_ref, w_ref, b_ref, o_ref):
    cs = _silu(c_ref[...]).astype(BF16)
    o_ref[...] = _dot(cs, w_ref[...].astype(BF16)) + b_ref[...]


def _modulation(c_pad, w_ada, b_ada, tn):
    rows, d = c_pad.shape
    n = w_ada.shape[1]
    return pl.pallas_call(
        _mod_kernel,
        grid=(n // tn,),
        in_specs=[
            pl.BlockSpec((rows, d), lambda j: (0, 0)),
            pl.BlockSpec((d, tn), lambda j: (0, j)),
            pl.BlockSpec((1, tn), lambda j: (0, j)),
        ],
        out_specs=pl.BlockSpec((rows, tn), lambda j: (0, j)),
        out_shape=jax.ShapeDtypeStruct((rows, n), F32),
        compiler_params=pltpu.CompilerParams(
            dimension_semantics=("arbitrary",), vmem_limit_bytes=VMEM_LIMIT),
        name="adaln_mod",
    )(c_pad, w_ada, b_ada)


def _mod_spec(idx, d):
    return pl.BlockSpec((None, None, 1, d), lambda b, t: (b, idx, 0, 0))


def _const_spec(shape):
    return pl.BlockSpec(shape, lambda b, t: tuple(0 for _ in shape))


def _ffn_kernel(x_ref, sh_ref, sc_ref, gt_ref, pre_ref, post_ref, wg_ref, wu_ref, wd_ref,
                o_ref, *, ff_chunk, row_chains):
    tm = x_ref.shape[0]
    rows = tm // row_chains
    d_ff = wg_ref.shape[1]
    spans = [slice(r * rows, (r + 1) * rows) for r in range(row_chains)]
    hs = [(_rms(x_ref[sp, :], pre_ref[...]) * (1.0 + sc_ref[...]) + sh_ref[...]).astype(BF16)
          for sp in spans]
    ys = [None] * row_chains
    for c0 in range(0, d_ff, ff_chunk):
        for r in range(row_chains):
            g = _dot(hs[r], wg_ref[:, c0:c0 + ff_chunk])
            u = _dot(hs[r], wu_ref[:, c0:c0 + ff_chunk])
            a = (_silu(g) * u).astype(BF16)
            part = _dot(a, wd_ref[c0:c0 + ff_chunk, :])
            ys[r] = part if ys[r] is None else ys[r] + part
    for r, sp in enumerate(spans):
        o_ref[sp, :] = x_ref[sp, :] + (0.5 * gt_ref[...]) * _rms(ys[r], post_ref[...])


def _resident_spec(shape):
    return pl.BlockSpec(shape, lambda b, t: tuple(0 for _ in shape),
                        pipeline_mode=pl.Buffered(1))


def _ffn(x, mod4, mod_base, pre_g, post_g, wg, wu, wd, tm, ff_chunk, row_chains):
    bsz, s, d = x.shape
    d_ff = wg.shape[1]
    tok = pl.BlockSpec((None, tm, d), lambda b, t: (b, t, 0))
    return pl.pallas_call(
        functools.partial(_ffn_kernel, ff_chunk=ff_chunk, row_chains=row_chains),
        grid=(bsz, s // tm),
        in_specs=[
            tok,
            _mod_spec(mod_base, d), _mod_spec(mod_base + 1, d), _mod_spec(mod_base + 2, d),
            _const_spec((1, d)), _const_spec((1, d)),
            _resident_spec((d, d_ff)), _resident_spec((d, d_ff)), _resident_spec((d_ff, d)),
        ],
        out_specs=tok,
        out_shape=jax.ShapeDtypeStruct((bsz, s, d), F32),
        compiler_params=pltpu.CompilerParams(
            dimension_semantics=("arbitrary", "arbitrary"), vmem_limit_bytes=VMEM_LIMIT),
        name="ffn_half_step",
    )(x, mod4, mod4, mod4, pre_g, post_g, wg, wu, wd)


_Z0 = 0
_XBC0 = _Z0 + SSD_INNER
_DT0 = _XBC0 + SSD_XBC
_Q0 = _DT0 + DT_PAD
_K0 = _Q0 + ATTN_WIDTH
_V0 = _K0 + ATTN_WIDTH
_IN_PAD = _V0 + ATTN_WIDTH


def _inproj_kernel(x_ref, sh_ref, sc_ref, pre_ref, w_ref,
                   z_ref, xbc_ref, dt_ref, q1_ref, q2_ref, k_ref, vt_ref):
    h = (_rms(x_ref[...], pre_ref[...]) * (1.0 + sc_ref[...]) + sh_ref[...]).astype(BF16)
    z_ref[...] = _dot(h, w_ref[:, _Z0:_XBC0])
    xbc_ref[...] = _dot(h, w_ref[:, _XBC0:_DT0])
    dt_ref[...] = _dot(h, w_ref[:, _DT0:_Q0])
    q = (_dot(h, w_ref[:, _Q0:_K0]) * (ATTN_QK_DIM ** -0.5 * LOG2E)).astype(BF16)
    lane = lax.broadcasted_iota(jnp.int32, q.shape, 1)
    is_map1 = lane % (2 * ATTN_QK_DIM) < ATTN_QK_DIM
    zero = jnp.zeros_like(q)
    q1_ref[...] = jnp.where(is_map1, q, zero)
    q2_ref[...] = jnp.where(is_map1, zero, q)
    k_ref[...] = _dot(h, w_ref[:, _K0:_V0]).astype(BF16)
    ones = jnp.ones((V_AUG - ATTN_V_DIM, x_ref.shape[0]), BF16)
    for hh in range(ATTN_HEADS):
        v_h = _dot(h, w_ref[:, _V0 + hh * ATTN_V_DIM:_V0 + (hh + 1) * ATTN_V_DIM])
        vt_ref[hh, 0:ATTN_V_DIM, :] = v_h.T.astype(BF16)
        vt_ref[hh, ATTN_V_DIM:V_AUG, :] = ones


def _inproj(x, mod4, pre_g, w_pad, tm):
    bsz, s, d = x.shape

    def tok(width):
        return pl.BlockSpec((None, tm, width), lambda b, t: (b, t, 0))

    def out(width, dtype):
        return jax.ShapeDtypeStruct((bsz, s, width), dtype)

    return pl.pallas_call(
        _inproj_kernel,
        grid=(bsz, s // tm),
        in_specs=[tok(d), _mod_spec(3, d), _mod_spec(4, d), _const_spec((1, d)),
                  _const_spec((d, _IN_PAD))],
        out_specs=[tok(SSD_INNER), tok(SSD_XBC), tok(DT_PAD),
                   tok(ATTN_WIDTH), tok(ATTN_WIDTH), tok(ATTN_WIDTH),
                   pl.BlockSpec((None, None, ATTN_HEADS, V_AUG, tm),
                                lambda b, t: (b, t, 0, 0, 0))],
        out_shape=[out(SSD_INNER, F32), out(SSD_XBC, F32), out(DT_PAD, F32),
                   out(ATTN_WIDTH, BF16), out(ATTN_WIDTH, BF16), out(ATTN_WIDTH, BF16),
                   jax.ShapeDtypeStruct((bsz, s // tm, ATTN_HEADS, V_AUG, tm), BF16)],
        compiler_params=pltpu.CompilerParams(
            dimension_semantics=("arbitrary", "arbitrary"), vmem_limit_bytes=VMEM_LIMIT),
        name="mixer_in_proj",
    )(x, mod4, mod4, pre_g, w_pad)


_CONV_HALO = 8


def _ssd_kernel(xbc_ref, dt_ref, z_ref, cw_ref, cb_ref, dtb_ref, alog_ref, dskip_ref, ng_ref,
                y_ref, halo_ref, hstate, e64, e128, tril):
    q = SSD_CHUNK
    gw = SSD_INNER // SSD_GROUPS
    hpg = SSD_HEADS // SSD_GROUPS

    @pl.when(pl.program_id(1) == 0)
    def _init():
        halo_ref[...] = jnp.zeros((_CONV_HALO, SSD_XBC), F32)
        hstate[...] = jnp.zeros(hstate.shape, F32)
        r = lax.broadcasted_iota(jnp.int32, (LANES, SSD_INNER), 0)
        c = lax.broadcasted_iota(jnp.int32, (LANES, SSD_INNER), 1)
        e64[...] = (r == c // SSD_HEAD_DIM).astype(BF16)
        r = lax.broadcasted_iota(jnp.int32, (LANES, SSD_HEADS * LANES), 0)
        c = lax.broadcasted_iota(jnp.int32, (LANES, SSD_HEADS * LANES), 1)
        e128[...] = (r == c // LANES).astype(BF16)
        r = lax.broadcasted_iota(jnp.int32, (q, q), 0)
        c = lax.broadcasted_iota(jnp.int32, (q, q), 1)
        tril[...] = (r >= c).astype(BF16)

    u = xbc_ref[...]
    halo = halo_ref[...]
    row = lax.broadcasted_iota(jnp.int32, (_CONV_HALO, SSD_XBC), 0)
    acc = cb_ref[...] + cw_ref[SSD_CONV - 1:SSD_CONV, :] * u
    for shift in range(1, SSD_CONV):
        rolled = pltpu.roll(u, shift, 0)
        head = jnp.where(row < shift, pltpu.roll(halo, shift, 0), rolled[0:_CONV_HALO, :])
        shifted = jnp.concatenate([head, rolled[_CONV_HALO:, :]], axis=0)
        acc = acc + cw_ref[SSD_CONV - 1 - shift:SSD_CONV - shift, :] * shifted
    halo_ref[...] = u[q - _CONV_HALO:q, :]
    xc = _silu(acc)
    xs = xc[:, :SSD_INNER]
    bm = xc[:, SSD_INNER:SSD_INNER + SSD_GROUPS * SSD_STATE]
    cm = xc[:, SSD_INNER + SSD_GROUPS * SSD_STATE:]

    dt_in = dt_ref[...] + dtb_ref[...]
    dtv = jnp.maximum(dt_in, 0.0) + jnp.log1p(jnp.exp(-jnp.abs(dt_in)))
    d_a = dtv * (-jnp.exp(alog_ref[...]) * LOG2E)
    da_hi, da_lo = _split_bf16(d_a)
    acum = _dot(tril[...], da_hi) + _dot(tril[...], da_lo)
    acum_t = acum.T

    def expand(v, e_ref):
        hi, lo = _split_bf16(v)
        return _dot(hi, e_ref[...]) + _dot(lo, e_ref[...])

    dt_x = expand(dtv, e64)
    acum_x = expand(acum, e64)
    acum_x128 = expand(acum, e128)
    alast_x = acum_x[q - 1:q, :]
    ea_x = jnp.exp2(acum_x)
    dte_x = jnp.exp2(alast_x - acum_x)
    cd_x = jnp.exp2(alast_x)

    xdt = xs * dt_x
    xw = (xdt * dte_x).astype(BF16)
    causal = (lax.broadcasted_iota(jnp.int32, (q, q), 0)
              >= lax.broadcasted_iota(jnp.int32, (q, q), 1))
    lane = lax.broadcasted_iota(jnp.int32, (q, LANES), 1)
    low_half = lane < SSD_HEAD_DIM

    y_diag = []
    y_off = []
    for g in range(SSD_GROUPS):
        bg = bm[:, g * SSD_STATE:(g + 1) * SSD_STATE]
        cg = cm[:, g * SSD_STATE:(g + 1) * SSD_STATE].astype(BF16)
        cb = _dot_nt(cg, bg.astype(BF16))
        for pr in range(hpg // 2):
            blk = g * (hpg // 2) + pr
            ms = []
            for hh in (2 * blk, 2 * blk + 1):
                seg = acum_x128[:, hh * LANES:(hh + 1) * LANES] - acum_t[hh:hh + 1, :]
                decay = jnp.exp2(jnp.where(causal, seg, -jnp.inf))
                ms.append((cb * decay).astype(BF16))
            xb = xdt[:, blk * LANES:(blk + 1) * LANES]
            x_lo = jnp.where(low_half, xb, 0.0).astype(BF16)
            x_hi = jnp.where(low_half, 0.0, xb).astype(BF16)
            y_diag.append(_dot(jnp.concatenate(ms, axis=1),
                               jnp.concatenate([x_lo, x_hi], axis=0)))
        h_old = hstate[g]
        y_off.append(_dot(cg, h_old.astype(BF16)))
        new_state = _dot(bg.T.astype(BF16), xw[:, g * gw:(g + 1) * gw])
        hstate[g] = h_old * cd_x[:, g * gw:(g + 1) * gw] + new_state

    y = (jnp.concatenate(y_diag, axis=1) + jnp.concatenate(y_off, axis=1) * ea_x
         + dskip_ref[...] * xs)
    y = y * _silu(z_ref[...])
    outs = []
    for g in range(SSD_GROUPS):
        outs.append(_rms(y[:, g * gw:(g + 1) * gw], ng_ref[:, g * gw:(g + 1) * gw]))
    y_ref[...] = jnp.concatenate(outs, axis=1).astype(y_ref.dtype)


def _ssd(xbc, dt, z, conv_w, conv_b, dt_bias, a_log, d_skip, norm_g):
    bsz, s, _ = xbc.shape
    q = SSD_CHUNK

    def tok(width):
        return pl.BlockSpec((None, q, width), lambda b, t: (b, t, 0))

    return pl.pallas_call(
        _ssd_kernel,
        grid=(bsz, s // q),
        in_specs=[tok(SSD_XBC), tok(DT_PAD), tok(SSD_INNER),
                  _const_spec((SSD_CONV, SSD_XBC)), _const_spec((1, SSD_XBC)),
                  _const_spec((1, DT_PAD)), _const_spec((1, DT_PAD)),
                  _const_spec((1, SSD_INNER)), _const_spec((1, SSD_INNER))],
        out_specs=tok(SSD_INNER),
        out_shape=jax.ShapeDtypeStruct((bsz, s, SSD_INNER), BF16),
        scratch_shapes=[
            pltpu.VMEM((_CONV_HALO, SSD_XBC), F32),
            pltpu.VMEM((SSD_GROUPS, SSD_STATE, SSD_INNER // SSD_GROUPS), F32),
            pltpu.VMEM((LANES, SSD_INNER), BF16),
            pltpu.VMEM((LANES, SSD_HEADS * LANES), BF16),
            pltpu.VMEM((q, q), BF16),
        ],
        compiler_params=pltpu.CompilerParams(
            dimension_semantics=("arbitrary", "arbitrary"), vmem_limit_bytes=VMEM_LIMIT),
        name="ssd_scan",
    )(xbc, dt, z, conv_w, conv_b, dt_bias, a_log, d_skip, norm_g)


def _bias_kernel(tab_ref, o_ref, *, t):
    h = pl.program_id(0)
    kk = lax.broadcasted_iota(jnp.int32, (2 * t, t), 0)
    qq = lax.broadcasted_iota(jnp.int32, (2 * t, t), 1)
    rel = qq - kk + t
    n = jnp.maximum(rel, 0)
    max_exact = REL_BUCKETS // 2
    nf = jnp.maximum(n, 1).astype(F32)
    large = max_exact + (jnp.log(nf / max_exact) / math.log(REL_MAX_DIST / max_exact)
                         * (REL_BUCKETS - max_exact)).astype(jnp.int32)
    large = jnp.minimum(large, REL_BUCKETS - 1)
    bucket = jnp.where(n < max_exact, n, large)
    far = tab_ref[REL_BUCKETS - 1, h]
    acc = jnp.zeros((2 * t, t), F32)
    for b in range(REL_BUCKETS - 1):
        acc = jnp.where(bucket == b, (tab_ref[b, h] - far) * LOG2E, acc)
    o_ref[...] = jnp.where(rel >= 0, acc, -jnp.inf)


def _bias_tiles(rel_bias, t):
    return pl.pallas_call(
        functools.partial(_bias_kernel, t=t),
        grid=(ATTN_HEADS,),
        in_specs=[pl.BlockSpec(memory_space=pltpu.SMEM)],
        out_specs=pl.BlockSpec((None, 2 * t, t), lambda h: (h, 0, 0)),
        out_shape=jax.ShapeDtypeStruct((ATTN_HEADS, 2 * t, t), F32),
        compiler_params=pltpu.CompilerParams(
            dimension_semantics=("arbitrary",), vmem_limit_bytes=VMEM_LIMIT),
        name="rel_bias_tiles",
    )(rel_bias)


def _attn_kernel(lq1_ref, lk1_ref, lq2_ref, lk2_ref, subg_ref, q1_ref, q2_ref, k_ref, vt_ref,
                 band_ref, o_ref, m_sc, acc_sc, s_a, s_b, bias_sc, *, tq, lam_init):
    tk = tq // 2
    nq = q1_ref.shape[0] // tq
    q_refs = (q1_ref, q2_ref)
    lam = (jnp.exp(jnp.sum(lq1_ref[...] * lk1_ref[...], axis=-1, keepdims=True))
           - jnp.exp(jnp.sum(lq2_ref[...] * lk2_ref[...], axis=-1, keepdims=True))
           + lam_init)

    for r0 in range(0, 3 * tk, LANES):
        for c0 in range(0, tq, LANES):
            rel0 = c0 - r0 + tk
            if rel0 < 0:
                blk = jnp.full((LANES, LANES), -jnp.inf, F32)
            elif rel0 == 0:
                blk = band_ref[LANES:2 * LANES, :]
            elif rel0 == LANES:
                blk = band_ref[0:LANES, :]
            else:
                blk = jnp.zeros((LANES, LANES), F32)
            bias_sc[r0:r0 + LANES, c0:c0 + LANES] = blk

    def qk(i, j, dst, c0=0):
        kt = k_ref[pl.ds(pl.multiple_of(j * tk, tk), tk), :]
        for mp in range(2):
            qt = q_refs[mp][pl.ds(pl.multiple_of(i * tq + c0, tq - c0), tq - c0), :]
            dst[mp, :, c0:tq] = _dot_nt(kt, qt)

    def softmax_pv(src, j, bias, c0=0):
        for mp in range(2):
            s = src[mp, :, c0:tq]
            if bias is not None:
                s = s + bias[:, c0:tq]
            m_prev = m_sc[mp, :, c0:tq]
            m_new = jnp.maximum(m_prev, jnp.max(s, axis=0, keepdims=True))
            alpha = jnp.exp2(m_prev - m_new)
            p = jnp.exp2(s - m_new).astype(BF16)
            acc_sc[mp, :, c0:tq] = alpha * acc_sc[mp, :, c0:tq] + _dot(vt_ref[j], p)
            m_sc[mp, :, c0:tq] = m_new

    qk(0, 0, s_a)

    def q_tile(i, carry):
        m_sc[...] = jnp.full(m_sc.shape, -jnp.inf, F32)
        acc_sc[...] = jnp.zeros(acc_sc.shape, F32)

        def far_pair(jj, c):
            j = 2 * jj
            qk(i, j + 1, s_b)
            softmax_pv(s_a, j, None)
            qk(i, j + 2, s_a)
            softmax_pv(s_b, j + 1, None)
            return c

        lax.fori_loop(0, i - 1, far_pair, 0)

        @pl.when(i > 0)
        def _far_prev():
            qk(i, 2 * i - 1, s_b)
            softmax_pv(s_a, 2 * i - 2, None)
            qk(i, 2 * i, s_a)
            softmax_pv(s_b, 2 * i - 1, bias_sc[0:tk, :])

        qk(i, 2 * i + 1, s_b, c0=tk)
        softmax_pv(s_a, 2 * i, bias_sc[tk:2 * tk, :])
        qk(jnp.minimum(i + 1, nq - 1), 0, s_a)
        softmax_pv(s_b, 2 * i + 1, bias_sc[2 * tk:3 * tk, :], c0=tk)

        o1 = acc_sc[0, 0:ATTN_V_DIM, :] / acc_sc[0, ATTN_V_DIM:ATTN_V_DIM + 1, :]
        o2 = acc_sc[1, 0:ATTN_V_DIM, :] / acc_sc[1, ATTN_V_DIM:ATTN_V_DIM + 1, :]
        o = (o1 - lam * o2).T
        o = _rms(o, subg_ref[...]) * (1.0 - lam_init)
        o_ref[pl.ds(pl.multiple_of(i * tq, tq), tq), :] = o.astype(o_ref.dtype)
        return carry

    lax.fori_loop(0, nq, q_tile, 0)


def _attention(q1, q2, k, vt, band, lq1, lk1, lq2, lk2, subg, tq, lam_init):
    bsz, s, _ = q1.shape
    tk = tq // 2
    small = lambda shape: pl.BlockSpec(shape, lambda b, h: (0, 0))
    head_cols = pl.BlockSpec((None, s, LANES), lambda b, h: (b, 0, h))
    return pl.pallas_call(
        functools.partial(_attn_kernel, tq=tq, lam_init=lam_init),
        grid=(bsz, ATTN_HEADS),
        in_specs=[
            small((1, ATTN_QK_DIM)), small((1, ATTN_QK_DIM)),
            small((1, ATTN_QK_DIM)), small((1, ATTN_QK_DIM)),
            small((1, ATTN_V_DIM)),
            head_cols, head_cols, head_cols,
            pl.BlockSpec((None, s // tk, None, V_AUG, tk), lambda b, h: (b, 0, h, 0, 0)),
            pl.BlockSpec((None, 2 * LANES, LANES), lambda b, h: (h, 0, 0)),
        ],
        out_specs=head_cols,
        out_shape=jax.ShapeDtypeStruct((bsz, s, ATTN_WIDTH), BF16),
        scratch_shapes=[
            pltpu.VMEM((2, 1, tq), F32),
            pltpu.VMEM((2, V_AUG, tq), F32),
            pltpu.VMEM((2, tk, tq), F32),
            pltpu.VMEM((2, tk, tq), F32),
            pltpu.VMEM((3 * tk, tq), F32),
        ],
        compiler_params=pltpu.CompilerParams(
            dimension_semantics=("arbitrary", "arbitrary"),
            vmem_limit_bytes=VMEM_LIMIT),
        name="diff_attention",
    )(lq1, lk1, lq2, lk2, subg, q1, q2, k, vt, band)


def _outproj_kernel(x_ref, ys_ref, ya_ref, gt_ref, post_ref, w_ref, o_ref):
    k_ssd = ys_ref.shape[-1]
    m = _dot(ys_ref[...], w_ref[0:k_ssd, :]) + _dot(ya_ref[...], w_ref[k_ssd:, :])
    o_ref[...] = x_ref[...] + gt_ref[...] * _rms(m, post_ref[...])


def _outproj(x, y_ssd, y_attn, mod4, post_g, w_out, tm):
    bsz, s, d = x.shape

    def tok(width):
        return pl.BlockSpec((None, tm, width), lambda b, t: (b, t, 0))

    return pl.pallas_call(
        _outproj_kernel,
        grid=(bsz, s // tm),
        in_specs=[tok(d), tok(SSD_INNER), tok(ATTN_WIDTH), _mod_spec(5, d),
                  _const_spec((1, d)), _const_spec((SSD_INNER + ATTN_WIDTH, d))],
        out_specs=tok(d),
        out_shape=jax.ShapeDtypeStruct((bsz, s, d), F32),
        compiler_params=pltpu.CompilerParams(
            dimension_semantics=("arbitrary", "arbitrary"), vmem_limit_bytes=VMEM_LIMIT),
        name="mixer_out_proj",
    )(x, y_ssd, y_attn, mod4, post_g, w_out)


def _tiles(s):
    return dict(
        tm_ffn=min(1024, s),
        ffn_chains=2,
        tm_proj=min(512, s),
        ff_chunk=1408,
        tq_attn=min(1024, s),
        tn_mod=1152,
    )


def kernel(x, c, w_ada, b_ada, ffn1_pre_g, ffn1_post_g, ffn1_w_gate, ffn1_w_up, ffn1_w_down, mix_pre_g, mix_post_g, w_in, conv_w, conv_b, dt_bias, a_log, d_skip, ssd_norm_g, lambda_q1, lambda_k1, lambda_q2, lambda_k2, subln_g, w_out, ffn2_pre_g, ffn2_post_g, ffn2_w_gate, ffn2_w_up, ffn2_w_down, rel_bias):
    bsz, s, d = x.shape
    depth = w_ada.shape[0]
    cfg = _tiles(s)
    row = lambda v: v.reshape(1, -1)

    c_pad = jnp.zeros((8, d), F32).at[:bsz].set(c)
    band = _bias_tiles(rel_bias, LANES)

    for l in range(depth):
        mod = _modulation(c_pad, w_ada[l], row(b_ada[l]), cfg["tn_mod"])
        mod4 = mod[:bsz].reshape(bsz, N_MOD, 1, d)

        x = _ffn(x, mod4, 0, row(ffn1_pre_g[l]), row(ffn1_post_g[l]),
                 ffn1_w_gate[l].astype(BF16), ffn1_w_up[l].astype(BF16),
                 ffn1_w_down[l].astype(BF16), cfg["tm_ffn"], cfg["ff_chunk"],
                 cfg["ffn_chains"])

        w = w_in[l]
        sizes = [SSD_INNER, SSD_XBC, SSD_HEADS, ATTN_WIDTH, ATTN_WIDTH, ATTN_WIDTH]
        offs = [0]
        for sz in sizes:
            offs.append(offs[-1] + sz)
        w_pad = jnp.concatenate(
            [w[:, offs[0]:offs[3]], jnp.zeros((d, DT_PAD - SSD_HEADS), w.dtype), w[:, offs[3]:]],
            axis=1).astype(BF16)
        z, xbc, dt, q1, q2, k, vt = _inproj(x, mod4, row(mix_pre_g[l]), w_pad,
                                            cfg["tq_attn"] // 2)

        pad16 = lambda vec: jnp.zeros((1, DT_PAD), F32).at[0, :SSD_HEADS].set(vec)
        y_ssd = _ssd(xbc, dt, z, conv_w[l], row(conv_b[l]), pad16(dt_bias[l]), pad16(a_log[l]),
                     row(jnp.repeat(d_skip[l], SSD_HEAD_DIM)), row(ssd_norm_g[l]))

        lam_init = 0.8 - 0.6 * math.exp(-0.3 * l)
        y_attn = _attention(q1, q2, k, vt, band, row(lambda_q1[l]), row(lambda_k1[l]),
                            row(lambda_q2[l]), row(lambda_k2[l]), row(subln_g[l]),
                            cfg["tq_attn"], lam_init)

        x = _outproj(x, y_ssd, y_attn, mod4, row(mix_post_g[l]), w_out[l].astype(BF16),
                     cfg["tm_proj"])

        x = _ffn(x, mod4, 6, row(ffn2_pre_g[l]), row(ffn2_post_g[l]),
                 ffn2_w_gate[l].astype(BF16), ffn2_w_up[l].astype(BF16),
                 ffn2_w_down[l].astype(BF16), cfg["tm_ffn"], cfg["ff_chunk"],
                 cfg["ffn_chains"])
    return x
```

```python
import functools
import math

import jax
import jax.numpy as jnp
from jax import lax
from jax.experimental import pallas as pl
from jax.experimental.pallas import tpu as pltpu

F32 = jnp.float32
BF16 = jnp.bfloat16

LANES = 128
EPS = 1e-6

SSD_HEADS = 16
SSD_HEAD_DIM = 64
SSD_INNER = SSD_HEADS * SSD_HEAD_DIM
SSD_GROUPS = 2
SSD_STATE = 128
SSD_CONV = 4
SSD_CHUNK = 128
SSD_XBC = SSD_INNER + 2 * SSD_GROUPS * SSD_STATE
ATTN_HEADS = 8
ATTN_QK_DIM = 64
ATTN_V_DIM = 128
ATTN_WIDTH = ATTN_HEADS * ATTN_V_DIM
REL_BUCKETS = 32
REL_MAX_DIST = 128
N_MOD = 9
DT_PAD = LANES
BF16_SUBLANES = 16
V_AUG = ATTN_V_DIM + BF16_SUBLANES
LOG2E = 1.4426950408889634

VMEM_LIMIT = 56 * 1024 * 1024


def _dot(a, b):
    return jnp.dot(a, b, preferred_element_type=F32)


def _dot_nt(a, b):
    return lax.dot_general(a, b, (((1,), (1,)), ((), ())), preferred_element_type=F32)


def _rms(x, g):
    return x * lax.rsqrt(jnp.mean(x * x, axis=-1, keepdims=True) + EPS) * g


def _silu(x):
    half = 0.5 * x
    return half + half * jnp.tanh(half)


def _split_bf16(v):
    hi = v.astype(BF16)
    lo = (v - hi.astype(F32)).astype(BF16)
    return hi, lo


def _mod_kernel(c_ref, w_ref, b_ref, o_ref):
    cs = _silu(c_ref[...]).astype(BF16)
    o_ref[...] = _dot(cs, w_ref[...].astype(BF16)) + b_ref[...]


def _modulation(c_pad, w_ada, b_ada, tn):
    rows, d = c_pad.shape
    n = w_ada.shape[1]
    return pl.pallas_call(
        _mod_kernel,
        grid=(n // tn,),
        in_specs=[
            pl.BlockSpec((rows, d), lambda j: (0, 0)),
            pl.BlockSpec((d, tn), lambda j: (0, j)),
            pl.BlockSpec((1, tn), lambda j: (0, j)),
        ],
        out_specs=pl.BlockSpec((rows, tn), lambda j: (0, j)),
        out_shape=jax.ShapeDtypeStruct((rows, n), F32),
        compiler_params=pltpu.CompilerParams(
            dimension_semantics=("arbitrary",), vmem_limit_bytes=VMEM_LIMIT),
        name="adaln_mod",
    )(c_pad, w_ada, b_ada)


def _mod_spec(idx, d):
    return pl.BlockSpec((None, None, 1, d), lambda b, t: (b, idx, 0, 0))


def _const_spec(shape):
    return pl.BlockSpec(shape, lambda b, t: tuple(0 for _ in shape))


def _ffn_kernel(x_ref, sh_ref, sc_ref, gt_ref, pre_ref, post_ref, wg_ref, wu_ref, wd_ref,
                o_ref, *, ff_chunk, row_chains):
    tm = x_ref.shape[0]
    rows = tm // row_chains
    d_ff = wg_ref.shape[1]
    spans = [slice(r * rows, (r + 1) * rows) for r in range(row_chains)]
    hs = [(_rms(x_ref[sp, :], pre_ref[...]) * (1.0 + sc_ref[...]) + sh_ref[...]).astype(BF16)
          for sp in spans]
    ys = [None] * row_chains
    for c0 in range(0, d_ff, ff_chunk):
        for r in range(row_chains):
            g = _dot(hs[r], wg_ref[:, c0:c0 + ff_chunk])
            u = _dot(hs[r], wu_ref[:, c0:c0 + ff_chunk])
            a = (_silu(g) * u).astype(BF16)
            part = _dot(a, wd_ref[c0:c0 + ff_chunk, :])
            ys[r] = part if ys[r] is None else ys[r] + part
    for r, sp in enumerate(spans):
        o_ref[sp, :] = x_ref[sp, :] + (0.5 * gt_ref[...]) * _rms(ys[r], post_ref[...])


def _resident_spec(shape):
    return pl.BlockSpec(shape, lambda b, t: tuple(0 for _ in shape),
                        pipeline_mode=pl.Buffered(1))


def _ffn(x, mod4, mod_base, pre_g, post_g, wg, wu, wd, tm, ff_chunk, row_chains):
    bsz, s, d = x.shape
    d_ff = wg.shape[1]
    tok = pl.BlockSpec((None, tm, d), lambda b, t: (b, t, 0))
    return pl.pallas_call(
        functools.partial(_ffn_kernel, ff_chunk=ff_chunk, row_chains=row_chains),
        grid=(bsz, s // tm),
        in_specs=[
            tok,
            _mod_spec(mod_base, d), _mod_spec(mod_base + 1, d), _mod_spec(mod_base + 2, d),
            _const_spec((1, d)), _const_spec((1, d)),
            _resident_spec((d, d_ff)), _resident_spec((d, d_ff)), _resident_spec((d_ff, d)),
        ],
        out_specs=tok,
        out_shape=jax.ShapeDtypeStruct((bsz, s, d), F32),
        compiler_params=pltpu.CompilerParams(
            dimension_semantics=("arbitrary", "arbitrary"), vmem_limit_bytes=VMEM_LIMIT),
        name="ffn_half_step",
    )(x, mod4, mod4, mod4, pre_g, post_g, wg, wu, wd)


_Z0 = 0
_XBC0 = _Z0 + SSD_INNER
_DT0 = _XBC0 + SSD_XBC
_Q0 = _DT0 + DT_PAD
_K0 = _Q0 + ATTN_WIDTH
_V0 = _K0 + ATTN_WIDTH
_IN_PAD = _V0 + ATTN_WIDTH


def _inproj_kernel(x_ref, sh_ref, sc_ref, pre_ref, w_ref,
                   z_ref, xbc_ref, dt_ref, q1_ref, q2_ref, k_ref, vt_ref):
    h = (_rms(x_ref[...], pre_ref[...]) * (1.0 + sc_ref[...]) + sh_ref[...]).astype(BF16)
    z_ref[...] = _dot(h, w_ref[:, _Z0:_XBC0])
    xbc_ref[...] = _dot(h, w_ref[:, _XBC0:_DT0])
    dt_ref[...] = _dot(h, w_ref[:, _DT0:_Q0])
    q = (_dot(h, w_ref[:, _Q0:_K0]) * (ATTN_QK_DIM ** -0.5 * LOG2E)).astype(BF16)
    lane = lax.broadcasted_iota(jnp.int32, q.shape, 1)
    is_map1 = lane % (2 * ATTN_QK_DIM) < ATTN_QK_DIM
    zero = jnp.zeros_like(q)
    q1_ref[...] = jnp.where(is_map1, q, zero)
    q2_ref[...] = jnp.where(is_map1, zero, q)
    k_ref[...] = _dot(h, w_ref[:, _K0:_V0]).astype(BF16)
    ones = jnp.ones((V_AUG - ATTN_V_DIM, x_ref.shape[0]), BF16)
    for hh in range(ATTN_HEADS):
        v_h = _dot(h, w_ref[:, _V0 + hh * ATTN_V_DIM:_V0 + (hh + 1) * ATTN_V_DIM])
        vt_ref[hh, 0:ATTN_V_DIM, :] = v_h.T.astype(BF16)
        vt_ref[hh, ATTN_V_DIM:V_AUG, :] = ones


def _inproj(x, mod4, pre_g, w_pad, tm):
    bsz, s, d = x.shape

    def tok(width):
        return pl.BlockSpec((None, tm, width), lambda b, t: (b, t, 0))

    def out(width, dtype):
        return jax.ShapeDtypeStruct((bsz, s, width), dtype)

    return pl.pallas_call(
        _inproj_kernel,
        grid=(bsz, s // tm),
        in_specs=[tok(d), _mod_spec(3, d), _mod_spec(4, d), _const_spec((1, d)),
                  _const_spec((d, _IN_PAD))],
        out_specs=[tok(SSD_INNER), tok(SSD_XBC), tok(DT_PAD),
                   tok(ATTN_WIDTH), tok(ATTN_WIDTH), tok(ATTN_WIDTH),
                   pl.BlockSpec((None, None, ATTN_HEADS, V_AUG, tm),
                                lambda b, t: (b, t, 0, 0, 0))],
        out_shape=[out(SSD_INNER, F32), out(SSD_XBC, F32), out(DT_PAD, F32),
                   out(ATTN_WIDTH, BF16), out(ATTN_WIDTH, BF16), out(ATTN_WIDTH, BF16),
                   jax.ShapeDtypeStruct((bsz, s // tm, ATTN_HEADS, V_AUG, tm), BF16)],
        compiler_params=pltpu.CompilerParams(
            dimension_semantics=("arbitrary", "arbitrary"), vmem_limit_bytes=VMEM_LIMIT),
        name="mixer_in_proj",
    )(x, mod4, mod4, pre_g, w_pad)


_CONV_HALO = 8


def _ssd_kernel(xbc_ref, dt_ref, z_ref, cw_ref, cb_ref, dtb_ref, alog_ref, dskip_ref, ng_ref,
                y_ref, halo_ref, hstate, e64, e128, tril):
    q = SSD_CHUNK
    gw = SSD_INNER // SSD_GROUPS
    hpg = SSD_HEADS // SSD_GROUPS

    @pl.when(pl.program_id(1) == 0)
    def _init():
        halo_ref[...] = jnp.zeros((_CONV_HALO, SSD_XBC), F32)
        hstate[...] = jnp.zeros(hstate.shape, F32)
        r = lax.broadcasted_iota(jnp.int32, (LANES, SSD_INNER), 0)
        c = lax.broadcasted_iota(jnp.int32, (LANES, SSD_INNER), 1)
        e64[...] = (r == c // SSD_HEAD_DIM).astype(BF16)
        r = lax.broadcasted_iota(jnp.int32, (LANES, SSD_HEADS * LANES), 0)
        c = lax.broadcasted_iota(jnp.int32, (LANES, SSD_HEADS * LANES), 1)
        e128[...] = (r == c // LANES).astype(BF16)
        r = lax.broadcasted_iota(jnp.int32, (q, q), 0)
        c = lax.broadcasted_iota(jnp.int32, (q, q), 1)
        tril[...] = (r >= c).astype(BF16)

    u = xbc_ref[...]
    halo = halo_ref[...]
    row = lax.broadcasted_iota(jnp.int32, (_CONV_HALO, SSD_XBC), 0)
    acc = cb_ref[...] + cw_ref[SSD_CONV - 1:SSD_CONV, :] * u
    for shift in range(1, SSD_CONV):
        rolled = pltpu.roll(u, shift, 0)
        head = jnp.where(row < shift, pltpu.roll(halo, shift, 0), rolled[0:_CONV_HALO, :])
        shifted = jnp.concatenate([head, rolled[_CONV_HALO:, :]], axis=0)
        acc = acc + cw_ref[SSD_CONV - 1 - shift:SSD_CONV - shift, :] * shifted
    halo_ref[...] = u[q - _CONV_HALO:q, :]
    xc = _silu(acc)
    xs = xc[:, :SSD_INNER]
    bm = xc[:, SSD_INNER:SSD_INNER + SSD_GROUPS * SSD_STATE]
    cm = xc[:, SSD_INNER + SSD_GROUPS * SSD_STATE:]

    dt_in = dt_ref[...] + dtb_ref[...]
    dtv = jnp.maximum(dt_in, 0.0) + jnp.log1p(jnp.exp(-jnp.abs(dt_in)))
    d_a = dtv * (-jnp.exp(alog_ref[...]) * LOG2E)
    da_hi, da_lo = _split_bf16(d_a)
    acum = _dot(tril[...], da_hi) + _dot(tril[...], da_lo)
    acum_t = acum.T

    def expand(v, e_ref):
        hi, lo = _split_bf16(v)
        return _dot(hi, e_ref[...]) + _dot(lo, e_ref[...])

    dt_x = expand(dtv, e64)
    acum_x = expand(acum, e64)
    acum_x128 = expand(acum, e128)
    alast_x = acum_x[q - 1:q, :]
    ea_x = jnp.exp2(acum_x)
    dte_x = jnp.exp2(alast_x - acum_x)
    cd_x = jnp.exp2(alast_x)

    xdt = xs * dt_x
    xw = (xdt * dte_x).astype(BF16)
    causal = (lax.broadcasted_iota(jnp.int32, (q, q), 0)
              >= lax.broadcasted_iota(jnp.int32, (q, q), 1))
    lane = lax.broadcasted_iota(jnp.int32, (q, LANES), 1)
    low_half = lane < SSD_HEAD_DIM

    y_diag = []
    y_off = []
    for g in range(SSD_GROUPS):
        bg = bm[:, g * SSD_STATE:(g + 1) * SSD_STATE]
        cg = cm[:, g * SSD_STATE:(g + 1) * SSD_STATE].astype(BF16)
        cb = _dot_nt(cg, bg.astype(BF16))
        for pr in range(hpg // 2):
            blk = g * (hpg // 2) + pr
            ms = []
            for hh in (2 * blk, 2 * blk + 1):
                seg = acum_x128[:, hh * LANES:(hh + 1) * LANES] - acum_t[hh:hh + 1, :]
                decay = jnp.exp2(jnp.where(causal, seg, -jnp.inf))
                ms.append((cb * decay).astype(BF16))
            xb = xdt[:, blk * LANES:(blk + 1) * LANES]
            x_lo = jnp.where(low_half, xb, 0.0).astype(BF16)
            x_hi = jnp.where(low_half, 0.0, xb).astype(BF16)
            y_diag.append(_dot(jnp.concatenate(ms, axis=1),
                               jnp.concatenate([x_lo, x_hi], axis=0)))
        h_old = hstate[g]
        y_off.append(_dot(cg, h_old.astype(BF16)))
        new_state = _dot(bg.T.astype(BF16), xw[:, g * gw:(g + 1) * gw])
        hstate[g] = h_old * cd_x[:, g * gw:(g + 1) * gw] + new_state

    y = (jnp.concatenate(y_diag, axis=1) + jnp.concatenate(y_off, axis=1) * ea_x
         + dskip_ref[...] * xs)
    y = y * _silu(z_ref[...])
    outs = []
    for g in range(SSD_GROUPS):
        outs.append(_rms(y[:, g * gw:(g + 1) * gw], ng_ref[:, g * gw:(g + 1) * gw]))
    y_ref[...] = jnp.concatenate(outs, axis=1).astype(y_ref.dtype)


def _ssd(xbc, dt, z, conv_w, conv_b, dt_bias, a_log, d_skip, norm_g):
    bsz, s, _ = xbc.shape
    q = SSD_CHUNK

    def tok(width):
        return pl.BlockSpec((None, q, width), lambda b, t: (b, t, 0))

    return pl.pallas_call(
        _ssd_kernel,
        grid=(bsz, s // q),
        in_specs=[tok(SSD_XBC), tok(DT_PAD), tok(SSD_INNER),
                  _const_spec((SSD_CONV, SSD_XBC)), _const_spec((1, SSD_XBC)),
                  _const_spec((1, DT_PAD)), _const_spec((1, DT_PAD)),
                  _const_spec((1, SSD_INNER)), _const_spec((1, SSD_INNER))],
        out_specs=tok(SSD_INNER),
        out_shape=jax.ShapeDtypeStruct((bsz, s, SSD_INNER), BF16),
        scratch_shapes=[
            pltpu.VMEM((_CONV_HALO, SSD_XBC), F32),
            pltpu.VMEM((SSD_GROUPS, SSD_STATE, SSD_INNER // SSD_GROUPS), F32),
            pltpu.VMEM((LANES, SSD_INNER), BF16),
            pltpu.VMEM((LANES, SSD_HEADS * LANES), BF16),
            pltpu.VMEM((q, q), BF16),
        ],
        compiler_params=pltpu.CompilerParams(
            dimension_semantics=("arbitrary", "arbitrary"), vmem_limit_bytes=VMEM_LIMIT),
        name="ssd_scan",
    )(xbc, dt, z, conv_w, conv_b, dt_bias, a_log, d_skip, norm_g)


def _bias_kernel(tab_ref, o_ref, *, t):
    h = pl.program_id(0)
    kk = lax.broadcasted_iota(jnp.int32, (2 * t, t), 0)
    qq = lax.broadcasted_iota(jnp.int32, (2 * t, t), 1)
    rel = qq - kk + t
    n = jnp.maximum(rel, 0)
    max_exact = REL_BUCKETS // 2
    nf = jnp.maximum(n, 1).astype(F32)
    large = max_exact + (jnp.log(nf / max_exact) / math.log(REL_MAX_DIST / max_exact)
                         * (REL_BUCKETS - max_exact)).astype(jnp.int32)
    large = jnp.minimum(large, REL_BUCKETS - 1)
    bucket = jnp.where(n < max_exact, n, large)
    far = tab_ref[REL_BUCKETS - 1, h]
    acc = jnp.zeros((2 * t, t), F32)
    for b in range(REL_BUCKETS - 1):
        acc = jnp.where(bucket == b, (tab_ref[b, h] - far) * LOG2E, acc)
    o_ref[...] = jnp.where(rel >= 0, acc, -jnp.inf)


def _bias_tiles(rel_bias, t):
    return pl.pallas_call(
        functools.partial(_bias_kernel, t=t),
        grid=(ATTN_HEADS,),
        in_specs=[pl.BlockSpec(memory_space=pltpu.SMEM)],
        out_specs=pl.BlockSpec((None, 2 * t, t), lambda h: (h, 0, 0)),
        out_shape=jax.ShapeDtypeStruct((ATTN_HEADS, 2 * t, t), F32),
        compiler_params=pltpu.CompilerParams(
            dimension_semantics=("arbitrary",), vmem_limit_bytes=VMEM_LIMIT),
        name="rel_bias_tiles",
    )(rel_bias)


def _attn_kernel(lq1_ref, lk1_ref, lq2_ref, lk2_ref, subg_ref, q1_ref, q2_ref, k_ref, vt_ref,
                 band_ref, o_ref, m_sc, acc_sc, s_a, s_b, mx_a, mx_b, *, tq, lam_init):
    tk = tq // 2
    nq = q1_ref.shape[0] // tq
    q_refs = (q1_ref, q2_ref)
    lam = (jnp.exp(jnp.sum(lq1_ref[...] * lk1_ref[...], axis=-1, keepdims=True))
           - jnp.exp(jnp.sum(lq2_ref[...] * lk2_ref[...], axis=-1, keepdims=True))
           + lam_init)

    def add_bias(buf, key_off, c0=0):
        for mp in range(2):
            for r0 in range(0, tk, LANES):
                for cb in range(c0, tq, LANES):
                    rel0 = cb - r0 - key_off
                    blk = (slice(r0, r0 + LANES), slice(cb, cb + LANES))
                    if rel0 < 0:
                        buf[mp, blk[0], blk[1]] = jnp.full((LANES, LANES), -jnp.inf, F32)
                    elif rel0 == 0:
                        buf[mp, blk[0], blk[1]] += band_ref[LANES:2 * LANES, :]
                    elif rel0 == LANES:
                        buf[mp, blk[0], blk[1]] += band_ref[0:LANES, :]

    def qk(i, j, dst, c0=0, mx_dst=None):
        kt = k_ref[pl.ds(pl.multiple_of(j * tk, tk), tk), :]
        for mp in range(2):
            qt = q_refs[mp][pl.ds(pl.multiple_of(i * tq + c0, tq - c0), tq - c0), :]
            sc = _dot_nt(kt, qt)
            dst[mp, :, c0:tq] = sc
            if mx_dst is not None:
                mx_dst[mp, :, c0:tq] = jnp.max(sc, axis=0, keepdims=True)

    def softmax_pv(src, j, c0=0, mx_src=None):
        for mp in range(2):
            s = src[mp, :, c0:tq]
            m_prev = m_sc[mp, :, c0:tq]
            if mx_src is not None:
                tile_max = mx_src[mp, :, c0:tq]
            else:
                tile_max = jnp.max(s, axis=0, keepdims=True)
            m_new = jnp.maximum(m_prev, tile_max)
            alpha = jnp.exp2(m_prev - m_new)
            p = jnp.exp2(s - m_new).astype(BF16)
            acc_sc[mp, :, c0:tq] = alpha * acc_sc[mp, :, c0:tq] + _dot(vt_ref[j], p)
            m_sc[mp, :, c0:tq] = m_new

    qk(0, 0, s_a, mx_dst=mx_a)

    def q_tile(i, carry):
        m_sc[...] = jnp.full(m_sc.shape, -jnp.inf, F32)
        acc_sc[...] = jnp.zeros(acc_sc.shape, F32)

        def far_pair(jj, c):
            j = 2 * jj
            qk(i, j + 1, s_b, mx_dst=mx_b)
            softmax_pv(s_a, j, mx_src=mx_a)
            qk(i, j + 2, s_a, mx_dst=mx_a)
            softmax_pv(s_b, j + 1, mx_src=mx_b)
            return c

        lax.fori_loop(0, i - 1, far_pair, 0)

        @pl.when(i > 0)
        def _far_prev():
            qk(i, 2 * i - 1, s_b)
            softmax_pv(s_a, 2 * i - 2, mx_src=mx_a)
            qk(i, 2 * i, s_a)
            add_bias(s_b, -tk)
            softmax_pv(s_b, 2 * i - 1)

        qk(i, 2 * i + 1, s_b, c0=tk)
        add_bias(s_a, 0)
        softmax_pv(s_a, 2 * i)
        qk(jnp.minimum(i + 1, nq - 1), 0, s_a, mx_dst=mx_a)
        add_bias(s_b, tk, c0=tk)
        softmax_pv(s_b, 2 * i + 1, c0=tk)

        inv1 = 1.0 / acc_sc[0, ATTN_V_DIM:ATTN_V_DIM + 1, :]
        inv2 = lam / acc_sc[1, ATTN_V_DIM:ATTN_V_DIM + 1, :]
        o = (acc_sc[0, 0:ATTN_V_DIM, :] * inv1 - acc_sc[1, 0:ATTN_V_DIM, :] * inv2).T
        o = _rms(o, subg_ref[...]) * (1.0 - lam_init)
        o_ref[pl.ds(pl.multiple_of(i * tq, tq), tq), :] = o.astype(o_ref.dtype)
        return carry

    lax.fori_loop(0, nq, q_tile, 0)


def _attention(q1, q2, k, vt, band, lq1, lk1, lq2, lk2, subg, tq, lam_init):
    bsz, s, _ = q1.shape
    tk = tq // 2
    small = lambda shape: pl.BlockSpec(shape, lambda b, h: (0, 0))
    head_cols = pl.BlockSpec((None, s, LANES), lambda b, h: (b, 0, h))
    return pl.pallas_call(
        functools.partial(_attn_kernel, tq=tq, lam_init=lam_init),
        grid=(bsz, ATTN_HEADS),
        in_specs=[
            small((1, ATTN_QK_DIM)), small((1, ATTN_QK_DIM)),
            small((1, ATTN_QK_DIM)), small((1, ATTN_QK_DIM)),
            small((1, ATTN_V_DIM)),
            head_cols, head_cols, head_cols,
            pl.BlockSpec((None, s // tk, None, V_AUG, tk), lambda b, h: (b, 0, h, 0, 0)),
            pl.BlockSpec((None, 2 * LANES, LANES), lambda b, h: (h, 0, 0)),
        ],
        out_specs=head_cols,
        out_shape=jax.ShapeDtypeStruct((bsz, s, ATTN_WIDTH), BF16),
        scratch_shapes=[
            pltpu.VMEM((2, 1, tq), F32),
            pltpu.VMEM((2, V_AUG, tq), F32),
            pltpu.VMEM((2, tk, tq), F32),
            pltpu.VMEM((2, tk, tq), F32),
            pltpu.VMEM((2, 1, tq), F32),
            pltpu.VMEM((2, 1, tq), F32),
        ],
        compiler_params=pltpu.CompilerParams(
            dimension_semantics=("arbitrary", "arbitrary"),
            vmem_limit_bytes=VMEM_LIMIT),
        name="diff_attention",
    )(lq1, lk1, lq2, lk2, subg, q1, q2, k, vt, band)


def _outproj_kernel(x_ref, ys_ref, ya_ref, gt_ref, post_ref, w_ref, o_ref):
    k_ssd = ys_ref.shape[-1]
    m = _dot(ys_ref[...], w_ref[0:k_ssd, :]) + _dot(ya_ref[...], w_ref[k_ssd:, :])
    o_ref[...] = x_ref[...] + gt_ref[...] * _rms(m, post_ref[...])


def _outproj(x, y_ssd, y_attn, mod4, post_g, w_out, tm):
    bsz, s, d = x.shape

    def tok(width):
        return pl.BlockSpec((None, tm, width), lambda b, t: (b, t, 0))

    return pl.pallas_call(
        _outproj_kernel,
        grid=(bsz, s // tm),
        in_specs=[tok(d), tok(SSD_INNER), tok(ATTN_WIDTH), _mod_spec(5, d),
                  _const_spec((1, d)), _const_spec((SSD_INNER + ATTN_WIDTH, d))],
        out_specs=tok(d),
        out_shape=jax.ShapeDtypeStruct((bsz, s, d), F32),
        compiler_params=pltpu.CompilerParams(
            dimension_semantics=("arbitrary", "arbitrary"), vmem_limit_bytes=VMEM_LIMIT),
        name="mixer_out_proj",
    )(x, y_ssd, y_attn, mod4, post_g, w_out)


def _tiles(s):
    return dict(
        tm_ffn=min(1024, s),
        ffn_chains=2,
        tm_proj=min(512, s),
        ff_chunk=1408,
        tq_attn=min(1024, s),
        tn_mod=1152,
    )


def kernel(x, c, w_ada, b_ada, ffn1_pre_g, ffn1_post_g, ffn1_w_gate, ffn1_w_up, ffn1_w_down, mix_pre_g, mix_post_g, w_in, conv_w, conv_b, dt_bias, a_log, d_skip, ssd_norm_g, lambda_q1, lambda_k1, lambda_q2, lambda_k2, subln_g, w_out, ffn2_pre_g, ffn2_post_g, ffn2_w_gate, ffn2_w_up, ffn2_w_down, rel_bias):
    bsz, s, d = x.shape
    depth = w_ada.shape[0]
    cfg = _tiles(s)
    row = lambda v: v.reshape(1, -1)

    c_pad = jnp.zeros((8, d), F32).at[:bsz].set(c)
    band = _bias_tiles(rel_bias, LANES)

    for l in range(depth):
        mod = _modulation(c_pad, w_ada[l], row(b_ada[l]), cfg["tn_mod"])
        mod4 = mod[:bsz].reshape(bsz, N_MOD, 1, d)

        x = _ffn(x, mod4, 0, row(ffn1_pre_g[l]), row(ffn1_post_g[l]),
                 ffn1_w_gate[l].astype(BF16), ffn1_w_up[l].astype(BF16),
                 ffn1_w_down[l].astype(BF16), cfg["tm_ffn"], cfg["ff_chunk"],
                 cfg["ffn_chains"])

        w = w_in[l]
        sizes = [SSD_INNER, SSD_XBC, SSD_HEADS, ATTN_WIDTH, ATTN_WIDTH, ATTN_WIDTH]
        offs = [0]
        for sz in sizes:
            offs.append(offs[-1] + sz)
        w_pad = jnp.concatenate(
            [w[:, offs[0]:offs[3]], jnp.zeros((d, DT_PAD - SSD_HEADS), w.dtype), w[:, offs[3]:]],
            axis=1).astype(BF16)
        z, xbc, dt, q1, q2, k, vt = _inproj(x, mod4, row(mix_pre_g[l]), w_pad,
                                            cfg["tq_attn"] // 2)

        pad16 = lambda vec: jnp.zeros((1, DT_PAD), F32).at[0, :SSD_HEADS].set(vec)
        y_ssd = _ssd(xbc, dt, z, conv_w[l], row(conv_b[l]), pad16(dt_bias[l]), pad16(a_log[l]),
                     row(jnp.repeat(d_skip[l], SSD_HEAD_DIM)), row(ssd_norm_g[l]))

        lam_init = 0.8 - 0.6 * math.exp(-0.3 * l)
        y_attn = _attention(q1, q2, k, vt, band, row(lambda_q1[l]), row(lambda_k1[l]),
                            row(lambda_q2[l]), row(lambda_k2[l]), row(subln_g[l]),
                            cfg["tq_attn"], lam_init)

        x = _outproj(x, y_ssd, y_attn, mod4, row(mix_post_g[l]), w_out[l].astype(BF16),
                     cfg["tm_proj"])

        x = _ffn(x, mod4, 6, row(ffn2_pre_g[l]), row(ffn2_post_g[l]),
                 ffn2_w_gate[l].astype(BF16), ffn2_w_up[l].astype(BF16),
                 ffn2_w_down[l].astype(BF16), cfg["tm_ffn"], cfg["ff_chunk"],
                 cfg["ffn_chains"])
    return x
```

```python
import functools
import math

import jax
import jax.numpy as jnp
from jax import lax
from jax.experimental import pallas as pl
from jax.experimental.pallas import tpu as pltpu

F32 = jnp.float32
BF16 = jnp.bfloat16

LANES = 128
EPS = 1e-6

SSD_HEADS = 16
SSD_HEAD_DIM = 64
SSD_INNER = SSD_HEADS * SSD_HEAD_DIM
SSD_GROUPS = 2
SSD_STATE = 128
SSD_CONV = 4
SSD_CHUNK = 128
SSD_XBC = SSD_INNER + 2 * SSD_GROUPS * SSD_STATE
ATTN_HEADS = 8
ATTN_QK_DIM = 64
ATTN_V_DIM = 128
ATTN_WIDTH = ATTN_HEADS * ATTN_V_DIM
REL_BUCKETS = 32
REL_MAX_DIST = 128
N_MOD = 9
DT_PAD = LANES
BF16_SUBLANES = 16
V_AUG = ATTN_V_DIM + BF16_SUBLANES
LOG2E = 1.4426950408889634

VMEM_LIMIT = 56 * 1024 * 1024


def _dot(a, b):
    return jnp.dot(a, b, preferred_element_type=F32)


def _dot_nt(a, b):
    return lax.dot_general(a, b, (((1,), (1,)), ((), ())), preferred_element_type=F32)


def _rms(x, g):
    return x * lax.rsqrt(jnp.mean(x * x, axis=-1, keepdims=True) + EPS) * g


def _silu(x):
    half = 0.5 * x
    return half + half * jnp.tanh(half)


def _split_bf16(v):
    hi = v.astype(BF16)
    lo = (v - hi.astype(F32)).astype(BF16)
    return hi, lo


def _mod_kernel(c_ref, w_ref, b_ref, o_ref):
    cs = _silu(c_ref[...]).astype(BF16)
    o_ref[...] = _dot(cs, w_ref[...].astype(BF16)) + b_ref[...]


def _modulation(c_pad, w_ada, b_ada, tn):
    rows, d = c_pad.shape
    n = w_ada.shape[1]
    return pl.pallas_call(
        _mod_kernel,
        grid=(n // tn,),
        in_specs=[
            pl.BlockSpec((rows, d), lambda j: (0, 0)),
            pl.BlockSpec((d, tn), lambda j: (0, j)),
            pl.BlockSpec((1, tn), lambda j: (0, j)),
        ],
        out_specs=pl.BlockSpec((rows, tn), lambda j: (0, j)),
        out_shape=jax.ShapeDtypeStruct((rows, n), F32),
        compiler_params=pltpu.CompilerParams(
            dimension_semantics=("arbitrary",), vmem_limit_bytes=VMEM_LIMIT),
        name="adaln_mod",
    )(c_pad, w_ada, b_ada)


def _mod_spec(idx, d):
    return pl.BlockSpec((None, None, 1, d), lambda b, t: (b, idx, 0, 0))


def _const_spec(shape):
    return pl.BlockSpec(shape, lambda b, t: tuple(0 for _ in shape))


def _ffn_kernel(x_ref, sh_ref, sc_ref, gt_ref, pre_ref, post_ref, wg_ref, wu_ref, wd_ref,
                o_ref, *, ff_chunk, row_chains):
    tm = x_ref.shape[0]
    rows = tm // row_chains
    d_ff = wg_ref.shape[1]
    spans = [slice(r * rows, (r + 1) * rows) for r in range(row_chains)]
    hs = [(_rms(x_ref[sp, :], pre_ref[...]) * (1.0 + sc_ref[...]) + sh_ref[...]).astype(BF16)
          for sp in spans]
    ys = [None] * row_chains
    for c0 in range(0, d_ff, ff_chunk):
        for r in range(row_chains):
            g = _dot(hs[r], wg_ref[:, c0:c0 + ff_chunk])
            u = _dot(hs[r], wu_ref[:, c0:c0 + ff_chunk])
            a = (_silu(g) * u).astype(BF16)
            part = _dot(a, wd_ref[c0:c0 + ff_chunk, :])
            ys[r] = part if ys[r] is None else ys[r] + part
    for r, sp in enumerate(spans):
        o_ref[sp, :] = x_ref[sp, :] + (0.5 * gt_ref[...]) * _rms(ys[r], post_ref[...])


def _resident_spec(shape):
    return pl.BlockSpec(shape, lambda b, t: tuple(0 for _ in shape),
                        pipeline_mode=pl.Buffered(1))


def _ffn(x, mod4, mod_base, pre_g, post_g, wg, wu, wd, tm, ff_chunk, row_chains):
    bsz, s, d = x.shape
    d_ff = wg.shape[1]
    tok = pl.BlockSpec((None, tm, d), lambda b, t: (b, t, 0))
    return pl.pallas_call(
        functools.partial(_ffn_kernel, ff_chunk=ff_chunk, row_chains=row_chains),
        grid=(bsz, s // tm),
        in_specs=[
            tok,
            _mod_spec(mod_base, d), _mod_spec(mod_base + 1, d), _mod_spec(mod_base + 2, d),
            _const_spec((1, d)), _const_spec((1, d)),
            _resident_spec((d, d_ff)), _resident_spec((d, d_ff)), _resident_spec((d_ff, d)),
        ],
        out_specs=tok,
        out_shape=jax.ShapeDtypeStruct((bsz, s, d), F32),
        compiler_params=pltpu.CompilerParams(
            dimension_semantics=("arbitrary", "arbitrary"), vmem_limit_bytes=VMEM_LIMIT),
        name="ffn_half_step",
    )(x, mod4, mod4, mod4, pre_g, post_g, wg, wu, wd)


_SSD_COLS = SSD_INNER + SSD_XBC
MXU_COLS = 256


def _inproj_kernel(x_ref, sh_ref, sc_ref, pre_ref, w_ssd_ref, w_dt_ref, w_attn_ref,
                   z_ref, xbc_ref, dt_ref, q1_ref, q2_ref, k_ref, vt_ref):
    h = (_rms(x_ref[...], pre_ref[...]) * (1.0 + sc_ref[...]) + sh_ref[...]).astype(BF16)
    z_ref[...] = _dot(h, w_ssd_ref[:, 0:SSD_INNER])
    xbc_ref[...] = _dot(h, w_ssd_ref[:, SSD_INNER:_SSD_COLS])
    dt_ref[...] = _dot(h, w_dt_ref[...])
    q = (_dot(h, w_attn_ref[:, 0:ATTN_WIDTH]) * (ATTN_QK_DIM ** -0.5 * LOG2E)).astype(BF16)
    lane = lax.broadcasted_iota(jnp.int32, q.shape, 1)
    is_map1 = lane % (2 * ATTN_QK_DIM) < ATTN_QK_DIM
    zero = jnp.zeros_like(q)
    q1_ref[...] = jnp.where(is_map1, q, zero)
    q2_ref[...] = jnp.where(is_map1, zero, q)
    k_ref[...] = _dot(h, w_attn_ref[:, ATTN_WIDTH:2 * ATTN_WIDTH]).astype(BF16)
    ones = jnp.ones((V_AUG - ATTN_V_DIM, x_ref.shape[0]), BF16)
    heads_per_dot = MXU_COLS // ATTN_V_DIM
    for h0 in range(0, ATTN_HEADS, heads_per_dot):
        c0 = 2 * ATTN_WIDTH + h0 * ATTN_V_DIM
        v_grp = _dot(h, w_attn_ref[:, c0:c0 + MXU_COLS])
        for hh in range(heads_per_dot):
            v_h = v_grp[:, hh * ATTN_V_DIM:(hh + 1) * ATTN_V_DIM]
            vt_ref[h0 + hh, 0:ATTN_V_DIM, :] = v_h.T.astype(BF16)
            vt_ref[h0 + hh, ATTN_V_DIM:V_AUG, :] = ones


def _inproj(x, mod4, pre_g, w_ssd, w_dt, w_attn, tm):
    bsz, s, d = x.shape

    def tok(width):
        return pl.BlockSpec((None, tm, width), lambda b, t: (b, t, 0))

    def out(width, dtype):
        return jax.ShapeDtypeStruct((bsz, s, width), dtype)

    return pl.pallas_call(
        _inproj_kernel,
        grid=(bsz, s // tm),
        in_specs=[tok(d), _mod_spec(3, d), _mod_spec(4, d), _const_spec((1, d)),
                  _resident_spec((d, _SSD_COLS)), _resident_spec((d, DT_PAD)),
                  _resident_spec((d, 3 * ATTN_WIDTH))],
        out_specs=[tok(SSD_INNER), tok(SSD_XBC), tok(DT_PAD),
                   tok(ATTN_WIDTH), tok(ATTN_WIDTH), tok(ATTN_WIDTH),
                   pl.BlockSpec((None, None, ATTN_HEADS, V_AUG, tm),
                                lambda b, t: (b, t, 0, 0, 0))],
        out_shape=[out(SSD_INNER, F32), out(SSD_XBC, F32), out(DT_PAD, F32),
                   out(ATTN_WIDTH, BF16), out(ATTN_WIDTH, BF16), out(ATTN_WIDTH, BF16),
                   jax.ShapeDtypeStruct((bsz, s // tm, ATTN_HEADS, V_AUG, tm), BF16)],
        compiler_params=pltpu.CompilerParams(
            dimension_semantics=("arbitrary", "arbitrary"), vmem_limit_bytes=VMEM_LIMIT),
        name="mixer_in_proj",
    )(x, mod4, mod4, pre_g, w_ssd, w_dt, w_attn)


_CONV_HALO = 8


def _ssd_kernel(xbc_ref, dt_ref, z_ref, cw_ref, cb_ref, dtb_ref, alog_ref, dskip_ref, ng_ref,
                y_ref, halo_ref, hstate, e64, e128, tril):
    q = SSD_CHUNK

    @pl.when(pl.program_id(1) == 0)
    def _init():
        halo_ref[...] = jnp.zeros((_CONV_HALO, SSD_XBC), F32)
        hstate[...] = jnp.zeros(hstate.shape, F32)
        r = lax.broadcasted_iota(jnp.int32, (LANES, SSD_INNER), 0)
        c = lax.broadcasted_iota(jnp.int32, (LANES, SSD_INNER), 1)
        e64[...] = (r == c // SSD_HEAD_DIM).astype(BF16)
        r = lax.broadcasted_iota(jnp.int32, (LANES, SSD_HEADS * LANES), 0)
        c = lax.broadcasted_iota(jnp.int32, (LANES, SSD_HEADS * LANES), 1)
        e128[...] = (r == c // LANES).astype(BF16)
        r = lax.broadcasted_iota(jnp.int32, (q, q), 0)
        c = lax.broadcasted_iota(jnp.int32, (q, q), 1)
        tril[...] = (r >= c).astype(BF16)

    def chunk(ci, carry):
        rows = pl.ds(pl.multiple_of(ci * q, q), q)
        _ssd_chunk(xbc_ref[rows, :], dt_ref[rows, :], z_ref[rows, :], cw_ref, cb_ref, dtb_ref,
                   alog_ref, dskip_ref, ng_ref, y_ref.at[rows, :], halo_ref, hstate, e64, e128,
                   tril)
        return carry

    lax.fori_loop(0, xbc_ref.shape[0] // q, chunk, 0)


def _ssd_chunk(u, dt_raw, z, cw_ref, cb_ref, dtb_ref, alog_ref, dskip_ref, ng_ref,
               y_ref, halo_ref, hstate, e64, e128, tril):
    q = SSD_CHUNK
    gw = SSD_INNER // SSD_GROUPS
    hpg = SSD_HEADS // SSD_GROUPS

    halo = halo_ref[...]
    row = lax.broadcasted_iota(jnp.int32, (_CONV_HALO, SSD_XBC), 0)
    acc = cb_ref[...] + cw_ref[SSD_CONV - 1:SSD_CONV, :] * u
    for shift in range(1, SSD_CONV):
        rolled = pltpu.roll(u, shift, 0)
        head = jnp.where(row < shift, pltpu.roll(halo, shift, 0), rolled[0:_CONV_HALO, :])
        shifted = jnp.concatenate([head, rolled[_CONV_HALO:, :]], axis=0)
        acc = acc + cw_ref[SSD_CONV - 1 - shift:SSD_CONV - shift, :] * shifted
    halo_ref[...] = u[q - _CONV_HALO:q, :]
    xc = _silu(acc)
    xs = xc[:, :SSD_INNER]
    bm = xc[:, SSD_INNER:SSD_INNER + SSD_GROUPS * SSD_STATE]
    cm = xc[:, SSD_INNER + SSD_GROUPS * SSD_STATE:]

    dt_in = dt_raw + dtb_ref[...]
    dtv = jnp.maximum(dt_in, 0.0) + jnp.log1p(jnp.exp(-jnp.abs(dt_in)))
    d_a = dtv * (-jnp.exp(alog_ref[...]) * LOG2E)
    da_hi, da_lo = _split_bf16(d_a)
    acum = _dot(tril[...], da_hi) + _dot(tril[...], da_lo)
    acum_t = acum.T

    def expand(v, e_ref):
        hi, lo = _split_bf16(v)
        return _dot(hi, e_ref[...]) + _dot(lo, e_ref[...])

    dt_x = expand(dtv, e64)
    acum_x = expand(acum, e64)
    acum_x128 = expand(acum, e128)
    alast_x = acum_x[q - 1:q, :]
    ea_x = jnp.exp2(acum_x)
    dte_x = jnp.exp2(alast_x - acum_x)
    cd_x = jnp.exp2(alast_x)

    xdt = xs * dt_x
    xw = (xdt * dte_x).astype(BF16)
    causal = (lax.broadcasted_iota(jnp.int32, (q, q), 0)
              >= lax.broadcasted_iota(jnp.int32, (q, q), 1))
    lane = lax.broadcasted_iota(jnp.int32, (q, LANES), 1)
    low_half = lane < SSD_HEAD_DIM

    y_diag = []
    y_off = []
    for g in range(SSD_GROUPS):
        bg = bm[:, g * SSD_STATE:(g + 1) * SSD_STATE]
        cg = cm[:, g * SSD_STATE:(g + 1) * SSD_STATE].astype(BF16)
        cb = _dot_nt(cg, bg.astype(BF16))
        for pr in range(hpg // 2):
            blk = g * (hpg // 2) + pr
            ms = []
            for hh in (2 * blk, 2 * blk + 1):
                seg = acum_x128[:, hh * LANES:(hh + 1) * LANES] - acum_t[hh:hh + 1, :]
                decay = jnp.exp2(jnp.where(causal, seg, -jnp.inf))
                ms.append((cb * decay).astype(BF16))
            xb = xdt[:, blk * LANES:(blk + 1) * LANES]
            x_lo = jnp.where(low_half, xb, 0.0).astype(BF16)
            x_hi = jnp.where(low_half, 0.0, xb).astype(BF16)
            y_diag.append(_dot(jnp.concatenate(ms, axis=1),
                               jnp.concatenate([x_lo, x_hi], axis=0)))
        h_old = hstate[g]
        y_off.append(_dot(cg, h_old.astype(BF16)))
        new_state = _dot(bg.T.astype(BF16), xw[:, g * gw:(g + 1) * gw])
        hstate[g] = h_old * cd_x[:, g * gw:(g + 1) * gw] + new_state

    y = (jnp.concatenate(y_diag, axis=1) + jnp.concatenate(y_off, axis=1) * ea_x
         + dskip_ref[...] * xs)
    y = y * _silu(z)
    outs = []
    for g in range(SSD_GROUPS):
        outs.append(_rms(y[:, g * gw:(g + 1) * gw], ng_ref[:, g * gw:(g + 1) * gw]))
    y_ref[...] = jnp.concatenate(outs, axis=1).astype(y_ref.dtype)


def _ssd(xbc, dt, z, conv_w, conv_b, dt_bias, a_log, d_skip, norm_g, tm):
    bsz, s, _ = xbc.shape
    q = SSD_CHUNK

    def tok(width):
        return pl.BlockSpec((None, tm, width), lambda b, t: (b, t, 0))

    return pl.pallas_call(
        _ssd_kernel,
        grid=(bsz, s // tm),
        in_specs=[tok(SSD_XBC), tok(DT_PAD), tok(SSD_INNER),
                  _const_spec((SSD_CONV, SSD_XBC)), _const_spec((1, SSD_XBC)),
                  _const_spec((1, DT_PAD)), _const_spec((1, DT_PAD)),
                  _const_spec((1, SSD_INNER)), _const_spec((1, SSD_INNER))],
        out_specs=tok(SSD_INNER),
        out_shape=jax.ShapeDtypeStruct((bsz, s, SSD_INNER), BF16),
        scratch_shapes=[
            pltpu.VMEM((_CONV_HALO, SSD_XBC), F32),
            pltpu.VMEM((SSD_GROUPS, SSD_STATE, SSD_INNER // SSD_GROUPS), F32),
            pltpu.VMEM((LANES, SSD_INNER), BF16),
            pltpu.VMEM((LANES, SSD_HEADS * LANES), BF16),
            pltpu.VMEM((q, q), BF16),
        ],
        compiler_params=pltpu.CompilerParams(
            dimension_semantics=("arbitrary", "arbitrary"), vmem_limit_bytes=VMEM_LIMIT),
        name="ssd_scan",
    )(xbc, dt, z, conv_w, conv_b, dt_bias, a_log, d_skip, norm_g)


def _bias_kernel(tab_ref, o_ref, *, t):
    h = pl.program_id(0)
    kk = lax.broadcasted_iota(jnp.int32, (2 * t, t), 0)
    qq = lax.broadcasted_iota(jnp.int32, (2 * t, t), 1)
    rel = qq - kk + t
    n = jnp.maximum(rel, 0)
    max_exact = REL_BUCKETS // 2
    nf = jnp.maximum(n, 1).astype(F32)
    large = max_exact + (jnp.log(nf / max_exact) / math.log(REL_MAX_DIST / max_exact)
                         * (REL_BUCKETS - max_exact)).astype(jnp.int32)
    large = jnp.minimum(large, REL_BUCKETS - 1)
    bucket = jnp.where(n < max_exact, n, large)
    far = tab_ref[REL_BUCKETS - 1, h]
    acc = jnp.zeros((2 * t, t), F32)
    for b in range(REL_BUCKETS - 1):
        acc = jnp.where(bucket == b, (tab_ref[b, h] - far) * LOG2E, acc)
    o_ref[...] = jnp.where(rel >= 0, acc, -jnp.inf)


def _bias_tiles(rel_bias, t):
    return pl.pallas_call(
        functools.partial(_bias_kernel, t=t),
        grid=(ATTN_HEADS,),
        in_specs=[pl.BlockSpec(memory_space=pltpu.SMEM)],
        out_specs=pl.BlockSpec((None, 2 * t, t), lambda h: (h, 0, 0)),
        out_shape=jax.ShapeDtypeStruct((ATTN_HEADS, 2 * t, t), F32),
        compiler_params=pltpu.CompilerParams(
            dimension_semantics=("arbitrary",), vmem_limit_bytes=VMEM_LIMIT),
        name="rel_bias_tiles",
    )(rel_bias)


def _attn_kernel(lq1_ref, lk1_ref, lq2_ref, lk2_ref, subg_ref, q1_ref, q2_ref, k_ref, vt_ref,
                 band_ref, o_ref, m_sc, acc_sc, s_a, s_b, mx_a, mx_b, *, tq, lam_init):
    tk = tq // 2
    nq = q1_ref.shape[0] // tq
    q_refs = (q1_ref, q2_ref)
    lam = (jnp.exp(jnp.sum(lq1_ref[...] * lk1_ref[...], axis=-1, keepdims=True))
           - jnp.exp(jnp.sum(lq2_ref[...] * lk2_ref[...], axis=-1, keepdims=True))
           + lam_init)

    def add_bias(buf, key_off, c0=0):
        for mp in range(2):
            for r0 in range(0, tk, LANES):
                for cb in range(c0, tq, LANES):
                    rel0 = cb - r0 - key_off
                    blk = (slice(r0, r0 + LANES), slice(cb, cb + LANES))
                    if rel0 < 0:
                        buf[mp, blk[0], blk[1]] = jnp.full((LANES, LANES), -jnp.inf, F32)
                    elif rel0 == 0:
                        buf[mp, blk[0], blk[1]] += band_ref[LANES:2 * LANES, :]
                    elif rel0 == LANES:
                        buf[mp, blk[0], blk[1]] += band_ref[0:LANES, :]

    def qk(i, j, dst, c0=0, mx_dst=None):
        kt = k_ref[pl.ds(pl.multiple_of(j * tk, tk), tk), :]
        for mp in range(2):
            qt = q_refs[mp][pl.ds(pl.multiple_of(i * tq + c0, tq - c0), tq - c0), :]
            sc = _dot_nt(kt, qt)
            dst[mp, :, c0:tq] = sc
            if mx_dst is not None:
                mx_dst[mp, :, c0:tq] = jnp.max(sc, axis=0, keepdims=True)

    def softmax_pv(src, j, c0=0, mx_src=None):
        for mp in range(2):
            s = src[mp, :, c0:tq]
            m_prev = m_sc[mp, :, c0:tq]
            if mx_src is not None:
                tile_max = mx_src[mp, :, c0:tq]
            else:
                tile_max = jnp.max(s, axis=0, keepdims=True)
            m_new = jnp.maximum(m_prev, tile_max)
            alpha = jnp.exp2(m_prev - m_new)
            p = jnp.exp2(s - m_new).astype(BF16)
            acc_sc[mp, :, c0:tq] = alpha * acc_sc[mp, :, c0:tq] + _dot(vt_ref[j], p)
            m_sc[mp, :, c0:tq] = m_new

    qk(0, 0, s_a, mx_dst=mx_a)

    def q_tile(i, carry):
        m_sc[...] = jnp.full(m_sc.shape, -jnp.inf, F32)
        acc_sc[...] = jnp.zeros(acc_sc.shape, F32)

        def far_pair(jj, c):
            j = 2 * jj
            qk(i, j + 1, s_b, mx_dst=mx_b)
            softmax_pv(s_a, j, mx_src=mx_a)
            qk(i, j + 2, s_a, mx_dst=mx_a)
            softmax_pv(s_b, j + 1, mx_src=mx_b)
            return c

        lax.fori_loop(0, i - 1, far_pair, 0)

        @pl.when(i > 0)
        def _far_prev():
            qk(i, 2 * i - 1, s_b)
            softmax_pv(s_a, 2 * i - 2, mx_src=mx_a)
            qk(i, 2 * i, s_a)
            add_bias(s_b, -tk)
            softmax_pv(s_b, 2 * i - 1)

        qk(i, 2 * i + 1, s_b, c0=tk)
        add_bias(s_a, 0)
        softmax_pv(s_a, 2 * i)
        qk(jnp.minimum(i + 1, nq - 1), 0, s_a, mx_dst=mx_a)
        add_bias(s_b, tk, c0=tk)
        softmax_pv(s_b, 2 * i + 1, c0=tk)

        inv1 = 1.0 / acc_sc[0, ATTN_V_DIM:ATTN_V_DIM + 1, :]
        inv2 = lam / acc_sc[1, ATTN_V_DIM:ATTN_V_DIM + 1, :]
        o = (acc_sc[0, 0:ATTN_V_DIM, :] * inv1 - acc_sc[1, 0:ATTN_V_DIM, :] * inv2).T
        o = _rms(o, subg_ref[...]) * (1.0 - lam_init)
        o_ref[pl.ds(pl.multiple_of(i * tq, tq), tq), :] = o.astype(o_ref.dtype)
        return carry

    lax.fori_loop(0, nq, q_tile, 0)


def _attention(q1, q2, k, vt, band, lq1, lk1, lq2, lk2, subg, tq, lam_init):
    bsz, s, _ = q1.shape
    tk = tq // 2
    small = lambda shape: pl.BlockSpec(shape, lambda b, h: (0, 0))
    head_cols = pl.BlockSpec((None, s, LANES), lambda b, h: (b, 0, h))
    return pl.pallas_call(
        functools.partial(_attn_kernel, tq=tq, lam_init=lam_init),
        grid=(bsz, ATTN_HEADS),
        in_specs=[
            small((1, ATTN_QK_DIM)), small((1, ATTN_QK_DIM)),
            small((1, ATTN_QK_DIM)), small((1, ATTN_QK_DIM)),
            small((1, ATTN_V_DIM)),
            head_cols, head_cols, head_cols,
            pl.BlockSpec((None, s // tk, None, V_AUG, tk), lambda b, h: (b, 0, h, 0, 0)),
            pl.BlockSpec((None, 2 * LANES, LANES), lambda b, h: (h, 0, 0)),
        ],
        out_specs=head_cols,
        out_shape=jax.ShapeDtypeStruct((bsz, s, ATTN_WIDTH), BF16),
        scratch_shapes=[
            pltpu.VMEM((2, 1, tq), F32),
            pltpu.VMEM((2, V_AUG, tq), F32),
            pltpu.VMEM((2, tk, tq), F32),
            pltpu.VMEM((2, tk, tq), F32),
            pltpu.VMEM((2, 1, tq), F32),
            pltpu.VMEM((2, 1, tq), F32),
        ],
        compiler_params=pltpu.CompilerParams(
            dimension_semantics=("arbitrary", "arbitrary"),
            vmem_limit_bytes=VMEM_LIMIT),
        name="diff_attention",
    )(lq1, lk1, lq2, lk2, subg, q1, q2, k, vt, band)


def _outproj_kernel(x_ref, ys_ref, ya_ref, gt_ref, post_ref, w_ref, o_ref):
    k_ssd = ys_ref.shape[-1]
    m = _dot(ys_ref[...], w_ref[0:k_ssd, :]) + _dot(ya_ref[...], w_ref[k_ssd:, :])
    o_ref[...] = x_ref[...] + gt_ref[...] * _rms(m, post_ref[...])


def _outproj(x, y_ssd, y_attn, mod4, post_g, w_out, tm):
    bsz, s, d = x.shape

    def tok(width):
        return pl.BlockSpec((None, tm, width), lambda b, t: (b, t, 0))

    return pl.pallas_call(
        _outproj_kernel,
        grid=(bsz, s // tm),
        in_specs=[tok(d), tok(SSD_INNER), tok(ATTN_WIDTH), _mod_spec(5, d),
                  _const_spec((1, d)), _resident_spec((SSD_INNER + ATTN_WIDTH, d))],
        out_specs=tok(d),
        out_shape=jax.ShapeDtypeStruct((bsz, s, d), F32),
        compiler_params=pltpu.CompilerParams(
            dimension_semantics=("arbitrary", "arbitrary"), vmem_limit_bytes=VMEM_LIMIT),
        name="mixer_out_proj",
    )(x, y_ssd, y_attn, mod4, post_g, w_out)


def _tiles(s):
    return dict(
        tm_ffn=min(1024, s),
        ffn_chains=2,
        tm_proj=min(1024, s),
        tm_ssd=min(512, s),
        ff_chunk=1408,
        tq_attn=min(1024, s),
        tn_mod=1152,
    )


def kernel(x, c, w_ada, b_ada, ffn1_pre_g, ffn1_post_g, ffn1_w_gate, ffn1_w_up, ffn1_w_down, mix_pre_g, mix_post_g, w_in, conv_w, conv_b, dt_bias, a_log, d_skip, ssd_norm_g, lambda_q1, lambda_k1, lambda_q2, lambda_k2, subln_g, w_out, ffn2_pre_g, ffn2_post_g, ffn2_w_gate, ffn2_w_up, ffn2_w_down, rel_bias):
    bsz, s, d = x.shape
    depth = w_ada.shape[0]
    cfg = _tiles(s)
    row = lambda v: v.reshape(1, -1)

    c_pad = jnp.zeros((8, d), F32).at[:bsz].set(c)
    band = _bias_tiles(rel_bias, LANES)

    for l in range(depth):
        mod = _modulation(c_pad, w_ada[l], row(b_ada[l]), cfg["tn_mod"])
        mod4 = mod[:bsz].reshape(bsz, N_MOD, 1, d)

        x = _ffn(x, mod4, 0, row(ffn1_pre_g[l]), row(ffn1_post_g[l]),
                 ffn1_w_gate[l].astype(BF16), ffn1_w_up[l].astype(BF16),
                 ffn1_w_down[l].astype(BF16), cfg["tm_ffn"], cfg["ff_chunk"],
                 cfg["ffn_chains"])

        w = w_in[l]
        dt0 = _SSD_COLS
        w_ssd = w[:, :dt0].astype(BF16)
        w_dt = jnp.pad(w[:, dt0:dt0 + SSD_HEADS], ((0, 0), (0, DT_PAD - SSD_HEADS))).astype(BF16)
        w_attn = w[:, dt0 + SSD_HEADS:].astype(BF16)
        z, xbc, dt, q1, q2, k, vt = _inproj(x, mod4, row(mix_pre_g[l]), w_ssd, w_dt, w_attn,
                                            cfg["tq_attn"] // 2)

        pad16 = lambda vec: jnp.zeros((1, DT_PAD), F32).at[0, :SSD_HEADS].set(vec)
        y_ssd = _ssd(xbc, dt, z, conv_w[l], row(conv_b[l]), pad16(dt_bias[l]), pad16(a_log[l]),
                     row(jnp.repeat(d_skip[l], SSD_HEAD_DIM)), row(ssd_norm_g[l]),
                     cfg["tm_ssd"])

        lam_init = 0.8 - 0.6 * math.exp(-0.3 * l)
        y_attn = _attention(q1, q2, k, vt, band, row(lambda_q1[l]), row(lambda_k1[l]),
                            row(lambda_q2[l]), row(lambda_k2[l]), row(subln_g[l]),
                            cfg["tq_attn"], lam_init)

        x = _outproj(x, y_ssd, y_attn, mod4, row(mix_post_g[l]), w_out[l].astype(BF16),
                     cfg["tm_proj"])

        x = _ffn(x, mod4, 6, row(ffn2_pre_g[l]), row(ffn2_post_g[l]),
                 ffn2_w_gate[l].astype(BF16), ffn2_w_up[l].astype(BF16),
                 ffn2_w_down[l].astype(BF16), cfg["tm_ffn"], cfg["ff_chunk"],
                 cfg["ffn_chains"])
    return x
```

```python
import functools
import math

import jax
import jax.numpy as jnp
from jax import lax
from jax.experimental import pallas as pl
from jax.experimental.pallas import tpu as pltpu

F32 = jnp.float32
BF16 = jnp.bfloat16

LANES = 128
EPS = 1e-6

SSD_HEADS = 16
SSD_HEAD_DIM = 64
SSD_INNER = SSD_HEADS * SSD_HEAD_DIM
SSD_GROUPS = 2
SSD_STATE = 128
SSD_CONV = 4
SSD_CHUNK = 128
SSD_XBC = SSD_INNER + 2 * SSD_GROUPS * SSD_STATE
ATTN_HEADS = 8
ATTN_QK_DIM = 64
ATTN_V_DIM = 128
ATTN_WIDTH = ATTN_HEADS * ATTN_V_DIM
REL_BUCKETS = 32
REL_MAX_DIST = 128
N_MOD = 9
DT_PAD = LANES
BF16_SUBLANES = 16
V_AUG = ATTN_V_DIM + BF16_SUBLANES
LOG2E = 1.4426950408889634

VMEM_LIMIT = 56 * 1024 * 1024


def _dot(a, b):
    return jnp.dot(a, b, preferred_element_type=F32)


def _dot_nt(a, b):
    return lax.dot_general(a, b, (((1,), (1,)), ((), ())), preferred_element_type=F32)


def _rms(x, g):
    return x * lax.rsqrt(jnp.mean(x * x, axis=-1, keepdims=True) + EPS) * g


def _silu(x):
    half = 0.5 * x
    return half + half * jnp.tanh(half)


def _split_bf16(v):
    hi = v.astype(BF16)
    lo = (v - hi.astype(F32)).astype(BF16)
    return hi, lo


def _mod_kernel(c_ref, w_ref, b_ref, o_ref):
    cs = _silu(c_ref[...]).astype(BF16)
    o_ref[...] = _dot(cs, w_ref[...].astype(BF16)) + b_ref[...]


def _modulation(c_pad, w_ada, b_ada, tn):
    rows, d = c_pad.shape
    n = w_ada.shape[1]
    return pl.pallas_call(
        _mod_kernel,
        grid=(n // tn,),
        in_specs=[
            pl.BlockSpec((rows, d), lambda j: (0, 0)),
            pl.BlockSpec((d, tn), lambda j: (0, j)),
            pl.BlockSpec((1, tn), lambda j: (0, j)),
        ],
        out_specs=pl.BlockSpec((rows, tn), lambda j: (0, j)),
        out_shape=jax.ShapeDtypeStruct((rows, n), F32),
        compiler_params=pltpu.CompilerParams(
            dimension_semantics=("arbitrary",), vmem_limit_bytes=VMEM_LIMIT),
        name="adaln_mod",
    )(c_pad, w_ada, b_ada)


def _mod_spec(idx, d):
    return pl.BlockSpec((None, None, 1, d), lambda b, t: (b, idx, 0, 0))


def _const_spec(shape):
    return pl.BlockSpec(shape, lambda b, t: tuple(0 for _ in shape))


def _ffn_kernel(x_ref, sh_ref, sc_ref, gt_ref, pre_ref, post_ref, wg_ref, wu_ref, wd_ref,
                o_ref, *, ff_chunk, row_chains):
    tm = x_ref.shape[0]
    rows = tm // row_chains
    d_ff = wg_ref.shape[1]
    spans = [slice(r * rows, (r + 1) * rows) for r in range(row_chains)]
    hs = [(_rms(x_ref[sp, :], pre_ref[...]) * (1.0 + sc_ref[...]) + sh_ref[...]).astype(BF16)
          for sp in spans]
    ys = [None] * row_chains
    for c0 in range(0, d_ff, ff_chunk):
        for r in range(row_chains):
            g = _dot(hs[r], wg_ref[:, c0:c0 + ff_chunk])
            u = _dot(hs[r], wu_ref[:, c0:c0 + ff_chunk])
            a = (_silu(g) * u).astype(BF16)
            part = _dot(a, wd_ref[c0:c0 + ff_chunk, :])
            ys[r] = part if ys[r] is None else ys[r] + part
    for r, sp in enumerate(spans):
        o_ref[sp, :] = x_ref[sp, :] + (0.5 * gt_ref[...]) * _rms(ys[r], post_ref[...])


def _resident_spec(shape):
    return pl.BlockSpec(shape, lambda b, t: tuple(0 for _ in shape),
                        pipeline_mode=pl.Buffered(1))


def _ffn(x, mod4, mod_base, pre_g, post_g, wg, wu, wd, tm, ff_chunk, row_chains):
    bsz, s, d = x.shape
    d_ff = wg.shape[1]
    tok = pl.BlockSpec((None, tm, d), lambda b, t: (b, t, 0))
    return pl.pallas_call(
        functools.partial(_ffn_kernel, ff_chunk=ff_chunk, row_chains=row_chains),
        grid=(bsz, s // tm),
        in_specs=[
            tok,
            _mod_spec(mod_base, d), _mod_spec(mod_base + 1, d), _mod_spec(mod_base + 2, d),
            _const_spec((1, d)), _const_spec((1, d)),
            _resident_spec((d, d_ff)), _resident_spec((d, d_ff)), _resident_spec((d_ff, d)),
        ],
        out_specs=tok,
        out_shape=jax.ShapeDtypeStruct((bsz, s, d), F32),
        compiler_params=pltpu.CompilerParams(
            dimension_semantics=("arbitrary", "arbitrary"), vmem_limit_bytes=VMEM_LIMIT),
        name="ffn_half_step",
    )(x, mod4, mod4, mod4, pre_g, post_g, wg, wu, wd)


_SSD_COLS = SSD_INNER + SSD_XBC
MXU_COLS = 256


def _split_w_in_kernel(w_ref, ssd_ref, dt_ref, attn_ref):
    w = w_ref[...]
    rows = w.shape[0]
    dt0 = _SSD_COLS
    ssd_ref[...] = w[:, :dt0].astype(BF16)
    dt_ref[...] = jnp.concatenate(
        [w[:, dt0:dt0 + SSD_HEADS], jnp.zeros((rows, DT_PAD - SSD_HEADS), F32)],
        axis=1).astype(BF16)
    attn_ref[...] = w[:, dt0 + SSD_HEADS:].astype(BF16)


def _split_w_in(w, row_block):
    d, n = w.shape
    assert n == _SSD_COLS + SSD_HEADS + 3 * ATTN_WIDTH
    rows = lambda width: pl.BlockSpec((row_block, width), lambda r: (r, 0))
    return pl.pallas_call(
        _split_w_in_kernel,
        grid=(d // row_block,),
        in_specs=[rows(n)],
        out_specs=[rows(_SSD_COLS), rows(DT_PAD), rows(3 * ATTN_WIDTH)],
        out_shape=[jax.ShapeDtypeStruct((d, _SSD_COLS), BF16),
                   jax.ShapeDtypeStruct((d, DT_PAD), BF16),
                   jax.ShapeDtypeStruct((d, 3 * ATTN_WIDTH), BF16)],
        compiler_params=pltpu.CompilerParams(
            dimension_semantics=("arbitrary",), vmem_limit_bytes=VMEM_LIMIT),
        name="split_w_in",
    )(w)


def _inproj_kernel(x_ref, sh_ref, sc_ref, pre_ref, w_ssd_ref, w_dt_ref, w_attn_ref,
                   z_ref, xbc_ref, dt_ref, q1_ref, q2_ref, k_ref, vt_ref):
    h = (_rms(x_ref[...], pre_ref[...]) * (1.0 + sc_ref[...]) + sh_ref[...]).astype(BF16)
    z_ref[...] = _dot(h, w_ssd_ref[:, 0:SSD_INNER])
    xbc_ref[...] = _dot(h, w_ssd_ref[:, SSD_INNER:_SSD_COLS])
    dt_ref[...] = _dot(h, w_dt_ref[...])
    q = (_dot(h, w_attn_ref[:, 0:ATTN_WIDTH]) * (ATTN_QK_DIM ** -0.5 * LOG2E)).astype(BF16)
    lane = lax.broadcasted_iota(jnp.int32, q.shape, 1)
    is_map1 = lane % (2 * ATTN_QK_DIM) < ATTN_QK_DIM
    zero = jnp.zeros_like(q)
    q1_ref[...] = jnp.where(is_map1, q, zero)
    q2_ref[...] = jnp.where(is_map1, zero, q)
    k_ref[...] = _dot(h, w_attn_ref[:, ATTN_WIDTH:2 * ATTN_WIDTH]).astype(BF16)
    ones = jnp.ones((V_AUG - ATTN_V_DIM, x_ref.shape[0]), BF16)
    heads_per_dot = MXU_COLS // ATTN_V_DIM
    for h0 in range(0, ATTN_HEADS, heads_per_dot):
        c0 = 2 * ATTN_WIDTH + h0 * ATTN_V_DIM
        v_grp = _dot(h, w_attn_ref[:, c0:c0 + MXU_COLS])
        for hh in range(heads_per_dot):
            v_h = v_grp[:, hh * ATTN_V_DIM:(hh + 1) * ATTN_V_DIM]
            vt_ref[h0 + hh, 0:ATTN_V_DIM, :] = v_h.T.astype(BF16)
            vt_ref[h0 + hh, ATTN_V_DIM:V_AUG, :] = ones


def _inproj(x, mod4, pre_g, w_ssd, w_dt, w_attn, tm):
    bsz, s, d = x.shape

    def tok(width):
        return pl.BlockSpec((None, tm, width), lambda b, t: (b, t, 0))

    def out(width, dtype):
        return jax.ShapeDtypeStruct((bsz, s, width), dtype)

    return pl.pallas_call(
        _inproj_kernel,
        grid=(bsz, s // tm),
        in_specs=[tok(d), _mod_spec(3, d), _mod_spec(4, d), _const_spec((1, d)),
                  _resident_spec((d, _SSD_COLS)), _resident_spec((d, DT_PAD)),
                  _resident_spec((d, 3 * ATTN_WIDTH))],
        out_specs=[tok(SSD_INNER), tok(SSD_XBC), tok(DT_PAD),
                   tok(ATTN_WIDTH), tok(ATTN_WIDTH), tok(ATTN_WIDTH),
                   pl.BlockSpec((None, None, ATTN_HEADS, V_AUG, tm),
                                lambda b, t: (b, t, 0, 0, 0))],
        out_shape=[out(SSD_INNER, F32), out(SSD_XBC, F32), out(DT_PAD, F32),
                   out(ATTN_WIDTH, BF16), out(ATTN_WIDTH, BF16), out(ATTN_WIDTH, BF16),
                   jax.ShapeDtypeStruct((bsz, s // tm, ATTN_HEADS, V_AUG, tm), BF16)],
        compiler_params=pltpu.CompilerParams(
            dimension_semantics=("arbitrary", "arbitrary"), vmem_limit_bytes=VMEM_LIMIT),
        name="mixer_in_proj",
    )(x, mod4, mod4, pre_g, w_ssd, w_dt, w_attn)


_CONV_HALO = 8


def _ssd_kernel(xbc_ref, dt_ref, z_ref, cw_ref, cb_ref, dtb_ref, alog_ref, dskip_ref, ng_ref,
                y_ref, halo_ref, hstate, e64, e128, tril):
    q = SSD_CHUNK

    @pl.when(pl.program_id(1) == 0)
    def _init():
        halo_ref[...] = jnp.zeros((_CONV_HALO, SSD_XBC), F32)
        hstate[...] = jnp.zeros(hstate.shape, F32)
        r = lax.broadcasted_iota(jnp.int32, (LANES, SSD_INNER), 0)
        c = lax.broadcasted_iota(jnp.int32, (LANES, SSD_INNER), 1)
        e64[...] = (r == c // SSD_HEAD_DIM).astype(BF16)
        r = lax.broadcasted_iota(jnp.int32, (LANES, SSD_HEADS * LANES), 0)
        c = lax.broadcasted_iota(jnp.int32, (LANES, SSD_HEADS * LANES), 1)
        e128[...] = (r == c // LANES).astype(BF16)
        r = lax.broadcasted_iota(jnp.int32, (q, q), 0)
        c = lax.broadcasted_iota(jnp.int32, (q, q), 1)
        tril[...] = (r >= c).astype(BF16)

    def chunk(ci, carry):
        rows = pl.ds(pl.multiple_of(ci * q, q), q)
        _ssd_chunk(xbc_ref[rows, :], dt_ref[rows, :], z_ref[rows, :], cw_ref, cb_ref, dtb_ref,
                   alog_ref, dskip_ref, ng_ref, y_ref.at[rows, :], halo_ref, hstate, e64, e128,
                   tril)
        return carry

    lax.fori_loop(0, xbc_ref.shape[0] // q, chunk, 0)


def _ssd_chunk(u, dt_raw, z, cw_ref, cb_ref, dtb_ref, alog_ref, dskip_ref, ng_ref,
               y_ref, halo_ref, hstate, e64, e128, tril):
    q = SSD_CHUNK
    gw = SSD_INNER // SSD_GROUPS
    hpg = SSD_HEADS // SSD_GROUPS

    halo = halo_ref[...]
    row = lax.broadcasted_iota(jnp.int32, (_CONV_HALO, SSD_XBC), 0)
    acc = cb_ref[...] + cw_ref[SSD_CONV - 1:SSD_CONV, :] * u
    for shift in range(1, SSD_CONV):
        rolled = pltpu.roll(u, shift, 0)
        head = jnp.where(row < shift, pltpu.roll(halo, shift, 0), rolled[0:_CONV_HALO, :])
        shifted = jnp.concatenate([head, rolled[_CONV_HALO:, :]], axis=0)
        acc = acc + cw_ref[SSD_CONV - 1 - shift:SSD_CONV - shift, :] * shifted
    halo_ref[...] = u[q - _CONV_HALO:q, :]
    xc = _silu(acc)
    xs = xc[:, :SSD_INNER]
    bm = xc[:, SSD_INNER:SSD_INNER + SSD_GROUPS * SSD_STATE]
    cm = xc[:, SSD_INNER + SSD_GROUPS * SSD_STATE:]

    dt_in = dt_raw + dtb_ref[...]
    dtv = jnp.maximum(dt_in, 0.0) + jnp.log1p(jnp.exp(-jnp.abs(dt_in)))
    d_a = dtv * (-jnp.exp(alog_ref[...]) * LOG2E)
    da_hi, da_lo = _split_bf16(d_a)
    acum = _dot(tril[...], da_hi) + _dot(tril[...], da_lo)
    acum_t = acum.T

    def expand(v, e_ref):
        hi, lo = _split_bf16(v)
        return _dot(hi, e_ref[...]) + _dot(lo, e_ref[...])

    dt_x = expand(dtv, e64)
    acum_x = expand(acum, e64)
    acum_x128 = expand(acum, e128)
    alast_x = acum_x[q - 1:q, :]
    ea_x = jnp.exp2(acum_x)
    dte_x = jnp.exp2(alast_x - acum_x)
    cd_x = jnp.exp2(alast_x)

    xdt = xs * dt_x
    xw = (xdt * dte_x).astype(BF16)
    causal = (lax.broadcasted_iota(jnp.int32, (q, q), 0)
              >= lax.broadcasted_iota(jnp.int32, (q, q), 1))
    lane = lax.broadcasted_iota(jnp.int32, (q, LANES), 1)
    low_half = lane < SSD_HEAD_DIM

    y_diag = []
    y_off = []
    for g in range(SSD_GROUPS):
        bg = bm[:, g * SSD_STATE:(g + 1) * SSD_STATE]
        cg = cm[:, g * SSD_STATE:(g + 1) * SSD_STATE].astype(BF16)
        cb = _dot_nt(cg, bg.astype(BF16))
        for pr in range(hpg // 2):
            blk = g * (hpg // 2) + pr
            ms = []
            for hh in (2 * blk, 2 * blk + 1):
                seg = acum_x128[:, hh * LANES:(hh + 1) * LANES] - acum_t[hh:hh + 1, :]
                decay = jnp.exp2(jnp.where(causal, seg, -jnp.inf))
                ms.append((cb * decay).astype(BF16))
            xb = xdt[:, blk * LANES:(blk + 1) * LANES]
            x_lo = jnp.where(low_half, xb, 0.0).astype(BF16)
            x_hi = jnp.where(low_half, 0.0, xb).astype(BF16)
            y_diag.append(_dot(jnp.concatenate(ms, axis=1),
                               jnp.concatenate([x_lo, x_hi], axis=0)))
        h_old = hstate[g]
        y_off.append(_dot(cg, h_old.astype(BF16)))
        new_state = _dot(bg.T.astype(BF16), xw[:, g * gw:(g + 1) * gw])
        hstate[g] = h_old * cd_x[:, g * gw:(g + 1) * gw] + new_state

    y = (jnp.concatenate(y_diag, axis=1) + jnp.concatenate(y_off, axis=1) * ea_x
         + dskip_ref[...] * xs)
    y = y * _silu(z)
    outs = []
    for g in range(SSD_GROUPS):
        outs.append(_rms(y[:, g * gw:(g + 1) * gw], ng_ref[:, g * gw:(g + 1) * gw]))
    y_ref[...] = jnp.concatenate(outs, axis=1).astype(y_ref.dtype)


def _ssd(xbc, dt, z, conv_w, conv_b, dt_bias, a_log, d_skip, norm_g, tm):
    bsz, s, _ = xbc.shape
    q = SSD_CHUNK

    def tok(width):
        return pl.BlockSpec((None, tm, width), lambda b, t: (b, t, 0))

    return pl.pallas_call(
        _ssd_kernel,
        grid=(bsz, s // tm),
        in_specs=[tok(SSD_XBC), tok(DT_PAD), tok(SSD_INNER),
                  _const_spec((SSD_CONV, SSD_XBC)), _const_spec((1, SSD_XBC)),
                  _const_spec((1, DT_PAD)), _const_spec((1, DT_PAD)),
                  _const_spec((1, SSD_INNER)), _const_spec((1, SSD_INNER))],
        out_specs=tok(SSD_INNER),
        out_shape=jax.ShapeDtypeStruct((bsz, s, SSD_INNER), BF16),
        scratch_shapes=[
            pltpu.VMEM((_CONV_HALO, SSD_XBC), F32),
            pltpu.VMEM((SSD_GROUPS, SSD_STATE, SSD_INNER // SSD_GROUPS), F32),
            pltpu.VMEM((LANES, SSD_INNER), BF16),
            pltpu.VMEM((LANES, SSD_HEADS * LANES), BF16),
            pltpu.VMEM((q, q), BF16),
        ],
        compiler_params=pltpu.CompilerParams(
            dimension_semantics=("arbitrary", "arbitrary"), vmem_limit_bytes=VMEM_LIMIT),
        name="ssd_scan",
    )(xbc, dt, z, conv_w, conv_b, dt_bias, a_log, d_skip, norm_g)


def _bias_kernel(tab_ref, o_ref, *, t):
    h = pl.program_id(0)
    kk = lax.broadcasted_iota(jnp.int32, (2 * t, t), 0)
    qq = lax.broadcasted_iota(jnp.int32, (2 * t, t), 1)
    rel = qq - kk + t
    n = jnp.maximum(rel, 0)
    max_exact = REL_BUCKETS // 2
    nf = jnp.maximum(n, 1).astype(F32)
    large = max_exact + (jnp.log(nf / max_exact) / math.log(REL_MAX_DIST / max_exact)
                         * (REL_BUCKETS - max_exact)).astype(jnp.int32)
    large = jnp.minimum(large, REL_BUCKETS - 1)
    bucket = jnp.where(n < max_exact, n, large)
    far = tab_ref[REL_BUCKETS - 1, h]
    acc = jnp.zeros((2 * t, t), F32)
    for b in range(REL_BUCKETS - 1):
        acc = jnp.where(bucket == b, (tab_ref[b, h] - far) * LOG2E, acc)
    o_ref[...] = jnp.where(rel >= 0, acc, -jnp.inf)


def _bias_tiles(rel_bias, t):
    return pl.pallas_call(
        functools.partial(_bias_kernel, t=t),
        grid=(ATTN_HEADS,),
        in_specs=[pl.BlockSpec(memory_space=pltpu.SMEM)],
        out_specs=pl.BlockSpec((None, 2 * t, t), lambda h: (h, 0, 0)),
        out_shape=jax.ShapeDtypeStruct((ATTN_HEADS, 2 * t, t), F32),
        compiler_params=pltpu.CompilerParams(
            dimension_semantics=("arbitrary",), vmem_limit_bytes=VMEM_LIMIT),
        name="rel_bias_tiles",
    )(rel_bias)


def _attn_kernel(lq1_ref, lk1_ref, lq2_ref, lk2_ref, subg_ref, q1_ref, q2_ref, k_ref, vt_ref,
                 band_ref, o_ref, m_sc, acc_sc, s_a, s_b, mx_a, mx_b, *, tq, lam_init):
    tk = tq // 2
    nq = q1_ref.shape[0] // tq
    q_refs = (q1_ref, q2_ref)
    lam = (jnp.exp(jnp.sum(lq1_ref[...] * lk1_ref[...], axis=-1, keepdims=True))
           - jnp.exp(jnp.sum(lq2_ref[...] * lk2_ref[...], axis=-1, keepdims=True))
           + lam_init)

    def add_bias(buf, key_off, c0=0):
        for mp in range(2):
            for r0 in range(0, tk, LANES):
                for cb in range(c0, tq, LANES):
                    rel0 = cb - r0 - key_off
                    blk = (slice(r0, r0 + LANES), slice(cb, cb + LANES))
                    if rel0 < 0:
                        buf[mp, blk[0], blk[1]] = jnp.full((LANES, LANES), -jnp.inf, F32)
                    elif rel0 == 0:
                        buf[mp, blk[0], blk[1]] += band_ref[LANES:2 * LANES, :]
                    elif rel0 == LANES:
                        buf[mp, blk[0], blk[1]] += band_ref[0:LANES, :]

    def qk(i, j, dst, c0=0, mx_dst=None):
        kt = k_ref[pl.ds(pl.multiple_of(j * tk, tk), tk), :]
        for mp in range(2):
            qt = q_refs[mp][pl.ds(pl.multiple_of(i * tq + c0, tq - c0), tq - c0), :]
            sc = _dot_nt(kt, qt)
            dst[mp, :, c0:tq] = sc
            if mx_dst is not None:
                mx_dst[mp, :, c0:tq] = jnp.max(sc, axis=0, keepdims=True)

    def softmax_pv(src, j, c0=0, mx_src=None):
        for mp in range(2):
            s = src[mp, :, c0:tq]
            m_prev = m_sc[mp, :, c0:tq]
            if mx_src is not None:
                tile_max = mx_src[mp, :, c0:tq]
            else:
                tile_max = jnp.max(s, axis=0, keepdims=True)
            m_new = jnp.maximum(m_prev, tile_max)
            alpha = jnp.exp2(m_prev - m_new)
            p = jnp.exp2(s - m_new).astype(BF16)
            acc_sc[mp, :, c0:tq] = alpha * acc_sc[mp, :, c0:tq] + _dot(vt_ref[j], p)
            m_sc[mp, :, c0:tq] = m_new

    qk(0, 0, s_a, mx_dst=mx_a)

    def q_tile(i, carry):
        m_sc[...] = jnp.full(m_sc.shape, -jnp.inf, F32)
        acc_sc[...] = jnp.zeros(acc_sc.shape, F32)

        def far_pair(jj, c):
            j = 2 * jj
            qk(i, j + 1, s_b, mx_dst=mx_b)
            softmax_pv(s_a, j, mx_src=mx_a)
            qk(i, j + 2, s_a, mx_dst=mx_a)
            softmax_pv(s_b, j + 1, mx_src=mx_b)
            return c

        lax.fori_loop(0, i - 1, far_pair, 0)

        @pl.when(i > 0)
        def _far_prev():
            qk(i, 2 * i - 1, s_b)
            softmax_pv(s_a, 2 * i - 2, mx_src=mx_a)
            qk(i, 2 * i, s_a)
            add_bias(s_b, -tk)
            softmax_pv(s_b, 2 * i - 1)

        qk(i, 2 * i + 1, s_b, c0=tk)
        add_bias(s_a, 0)
        softmax_pv(s_a, 2 * i)
        qk(jnp.minimum(i + 1, nq - 1), 0, s_a, mx_dst=mx_a)
        add_bias(s_b, tk, c0=tk)
        softmax_pv(s_b, 2 * i + 1, c0=tk)

        inv1 = 1.0 / acc_sc[0, ATTN_V_DIM:ATTN_V_DIM + 1, :]
        inv2 = lam / acc_sc[1, ATTN_V_DIM:ATTN_V_DIM + 1, :]
        o = (acc_sc[0, 0:ATTN_V_DIM, :] * inv1 - acc_sc[1, 0:ATTN_V_DIM, :] * inv2).T
        o = _rms(o, subg_ref[...]) * (1.0 - lam_init)
        o_ref[pl.ds(pl.multiple_of(i * tq, tq), tq), :] = o.astype(o_ref.dtype)
        return carry

    lax.fori_loop(0, nq, q_tile, 0)


def _attention(q1, q2, k, vt, band, lq1, lk1, lq2, lk2, subg, tq, lam_init):
    bsz, s, _ = q1.shape
    tk = tq // 2
    small = lambda shape: pl.BlockSpec(shape, lambda b, h: (0, 0))
    head_cols = pl.BlockSpec((None, s, LANES), lambda b, h: (b, 0, h))
    return pl.pallas_call(
        functools.partial(_attn_kernel, tq=tq, lam_init=lam_init),
        grid=(bsz, ATTN_HEADS),
        in_specs=[
            small((1, ATTN_QK_DIM)), small((1, ATTN_QK_DIM)),
            small((1, ATTN_QK_DIM)), small((1, ATTN_QK_DIM)),
            small((1, ATTN_V_DIM)),
            head_cols, head_cols, head_cols,
            pl.BlockSpec((None, s // tk, None, V_AUG, tk), lambda b, h: (b, 0, h, 0, 0)),
            pl.BlockSpec((None, 2 * LANES, LANES), lambda b, h: (h, 0, 0)),
        ],
        out_specs=head_cols,
        out_shape=jax.ShapeDtypeStruct((bsz, s, ATTN_WIDTH), BF16),
        scratch_shapes=[
            pltpu.VMEM((2, 1, tq), F32),
            pltpu.VMEM((2, V_AUG, tq), F32),
            pltpu.VMEM((2, tk, tq), F32),
            pltpu.VMEM((2, tk, tq), F32),
            pltpu.VMEM((2, 1, tq), F32),
            pltpu.VMEM((2, 1, tq), F32),
        ],
        compiler_params=pltpu.CompilerParams(
            dimension_semantics=("arbitrary", "arbitrary"),
            vmem_limit_bytes=VMEM_LIMIT),
        name="diff_attention",
    )(lq1, lk1, lq2, lk2, subg, q1, q2, k, vt, band)


def _outproj_kernel(x_ref, ys_ref, ya_ref, gt_ref, post_ref, w_ref, o_ref):
    k_ssd = ys_ref.shape[-1]
    m = _dot(ys_ref[...], w_ref[0:k_ssd, :]) + _dot(ya_ref[...], w_ref[k_ssd:, :])
    o_ref[...] = x_ref[...] + gt_ref[...] * _rms(m, post_ref[...])


def _outproj(x, y_ssd, y_attn, mod4, post_g, w_out, tm):
    bsz, s, d = x.shape

    def tok(width):
        return pl.BlockSpec((None, tm, width), lambda b, t: (b, t, 0))

    return pl.pallas_call(
        _outproj_kernel,
        grid=(bsz, s // tm),
        in_specs=[tok(d), tok(SSD_INNER), tok(ATTN_WIDTH), _mod_spec(5, d),
                  _const_spec((1, d)), _resident_spec((SSD_INNER + ATTN_WIDTH, d))],
        out_specs=tok(d),
        out_shape=jax.ShapeDtypeStruct((bsz, s, d), F32),
        compiler_params=pltpu.CompilerParams(
            dimension_semantics=("arbitrary", "arbitrary"), vmem_limit_bytes=VMEM_LIMIT),
        name="mixer_out_proj",
    )(x, y_ssd, y_attn, mod4, post_g, w_out)


def _tiles(s):
    return dict(
        tm_ffn=min(1024, s),
        ffn_chains=2,
        tm_proj=min(1024, s),
        tm_ssd=min(512, s),
        ff_chunk=1408,
        tq_attn=min(1024, s),
        tn_mod=1152,
        w_in_rows=128,
    )


def kernel(x, c, w_ada, b_ada, ffn1_pre_g, ffn1_post_g, ffn1_w_gate, ffn1_w_up, ffn1_w_down, mix_pre_g, mix_post_g, w_in, conv_w, conv_b, dt_bias, a_log, d_skip, ssd_norm_g, lambda_q1, lambda_k1, lambda_q2, lambda_k2, subln_g, w_out, ffn2_pre_g, ffn2_post_g, ffn2_w_gate, ffn2_w_up, ffn2_w_down, rel_bias):
    bsz, s, d = x.shape
    depth = w_ada.shape[0]
    cfg = _tiles(s)
    row = lambda v: v.reshape(1, -1)

    c_pad = jnp.zeros((8, d), F32).at[:bsz].set(c)
    band = _bias_tiles(rel_bias, LANES)

    for l in range(depth):
        mod = _modulation(c_pad, w_ada[l], row(b_ada[l]), cfg["tn_mod"])
        mod4 = mod[:bsz].reshape(bsz, N_MOD, 1, d)

        x = _ffn(x, mod4, 0, row(ffn1_pre_g[l]), row(ffn1_post_g[l]),
                 ffn1_w_gate[l].astype(BF16), ffn1_w_up[l].astype(BF16),
                 ffn1_w_down[l].astype(BF16), cfg["tm_ffn"], cfg["ff_chunk"],
                 cfg["ffn_chains"])

        w_ssd, w_dt, w_attn = _split_w_in(w_in[l], cfg["w_in_rows"])
        z, xbc, dt, q1, q2, k, vt = _inproj(x, mod4, row(mix_pre_g[l]), w_ssd, w_dt, w_attn,
                                            cfg["tq_attn"] // 2)

        pad16 = lambda vec: jnp.zeros((1, DT_PAD), F32).at[0, :SSD_HEADS].set(vec)
        y_ssd = _ssd(xbc, dt, z, conv_w[l], row(conv_b[l]), pad16(dt_bias[l]), pad16(a_log[l]),
                     row(jnp.repeat(d_skip[l], SSD_HEAD_DIM)), row(ssd_norm_g[l]),
                     cfg["tm_ssd"])

        lam_init = 0.8 - 0.6 * math.exp(-0.3 * l)
        y_attn = _attention(q1, q2, k, vt, band, row(lambda_q1[l]), row(lambda_k1[l]),
                            row(lambda_q2[l]), row(lambda_k2[l]), row(subln_g[l]),
                            cfg["tq_attn"], lam_init)

        x = _outproj(x, y_ssd, y_attn, mod4, row(mix_post_g[l]), w_out[l].astype(BF16),
                     cfg["tm_proj"])

        x = _ffn(x, mod4, 6, row(ffn2_pre_g[l]), row(ffn2_post_g[l]),
                 ffn2_w_gate[l].astype(BF16), ffn2_w_up[l].astype(BF16),
                 ffn2_w_down[l].astype(BF16), cfg["tm_ffn"], cfg["ff_chunk"],
                 cfg["ffn_chains"])
    return x
```

```python
import functools
import math

import jax
import jax.numpy as jnp
from jax import lax
from jax.experimental import pallas as pl
from jax.experimental.pallas import tpu as pltpu

F32 = jnp.float32
BF16 = jnp.bfloat16

LANES = 128
EPS = 1e-6

SSD_HEADS = 16
SSD_HEAD_DIM = 64
SSD_INNER = SSD_HEADS * SSD_HEAD_DIM
SSD_GROUPS = 2
SSD_STATE = 128
SSD_CONV = 4
SSD_CHUNK = 128
SSD_XBC = SSD_INNER + 2 * SSD_GROUPS * SSD_STATE
ATTN_HEADS = 8
ATTN_QK_DIM = 64
ATTN_V_DIM = 128
ATTN_WIDTH = ATTN_HEADS * ATTN_V_DIM
REL_BUCKETS = 32
REL_MAX_DIST = 128
N_MOD = 9
DT_PAD = LANES
BF16_SUBLANES = 16
V_AUG = ATTN_V_DIM + BF16_SUBLANES
LOG2E = 1.4426950408889634

VMEM_LIMIT = 56 * 1024 * 1024


def _dot(a, b):
    return jnp.dot(a, b, preferred_element_type=F32)


def _dot_nt(a, b):
    return lax.dot_general(a, b, (((1,), (1,)), ((), ())), preferred_element_type=F32)


def _rms(x, g):
    return x * lax.rsqrt(jnp.mean(x * x, axis=-1, keepdims=True) + EPS) * g


def _silu(x):
    half = 0.5 * x
    return half + half * jnp.tanh(half)


def _split_bf16(v):
    hi = v.astype(BF16)
    lo = (v - hi.astype(F32)).astype(BF16)
    return hi, lo


def _mod_kernel(c_ref, w_ref, b_ref, o_ref):
    cs = _silu(c_ref[...]).astype(BF16)
    o_ref[...] = _dot(cs, w_ref[...].astype(BF16)) + b_ref[...]


def _modulation(c_pad, w_ada, b_ada, tn):
    rows, d = c_pad.shape
    n = w_ada.shape[1]
    return pl.pallas_call(
        _mod_kernel,
        grid=(n // tn,),
        in_specs=[
            pl.BlockSpec((rows, d), lambda j: (0, 0)),
            pl.BlockSpec((d, tn), lambda j: (0, j)),
            pl.BlockSpec((1, tn), lambda j: (0, j)),
        ],
        out_specs=pl.BlockSpec((rows, tn), lambda j: (0, j)),
        out_shape=jax.ShapeDtypeStruct((rows, n), F32),
        compiler_params=pltpu.CompilerParams(
            dimension_semantics=("arbitrary",), vmem_limit_bytes=VMEM_LIMIT),
        name="adaln_mod",
    )(c_pad, w_ada, b_ada)


def _mod_spec(idx, d):
    return pl.BlockSpec((None, None, 1, d), lambda b, t: (b, idx, 0, 0))


def _const_spec(shape):
    return pl.BlockSpec(shape, lambda b, t: tuple(0 for _ in shape))


def _ffn_kernel(x_ref, sh_ref, sc_ref, gt_ref, pre_ref, post_ref, wg_ref, wu_ref, wd_ref,
                o_ref, *, ff_chunk, row_chains):
    tm = x_ref.shape[0]
    rows = tm // row_chains
    d_ff = wg_ref.shape[1]
    spans = [slice(r * rows, (r + 1) * rows) for r in range(row_chains)]
    hs = [(_rms(x_ref[sp, :], pre_ref[...]) * (1.0 + sc_ref[...]) + sh_ref[...]).astype(BF16)
          for sp in spans]
    ys = [None] * row_chains
    for c0 in range(0, d_ff, ff_chunk):
        for r in range(row_chains):
            g = _dot(hs[r], wg_ref[:, c0:c0 + ff_chunk])
            u = _dot(hs[r], wu_ref[:, c0:c0 + ff_chunk])
            a = (_silu(g) * u).astype(BF16)
            part = _dot(a, wd_ref[c0:c0 + ff_chunk, :])
            ys[r] = part if ys[r] is None else ys[r] + part
    for r, sp in enumerate(spans):
        o_ref[sp, :] = x_ref[sp, :] + (0.5 * gt_ref[...]) * _rms(ys[r], post_ref[...])


def _resident_spec(shape):
    return pl.BlockSpec(shape, lambda b, t: tuple(0 for _ in shape),
                        pipeline_mode=pl.Buffered(1))


def _ffn(x, mod4, mod_base, pre_g, post_g, wg, wu, wd, tm, ff_chunk, row_chains):
    bsz, s, d = x.shape
    d_ff = wg.shape[1]
    tok = pl.BlockSpec((None, tm, d), lambda b, t: (b, t, 0))
    return pl.pallas_call(
        functools.partial(_ffn_kernel, ff_chunk=ff_chunk, row_chains=row_chains),
        grid=(bsz, s // tm),
        in_specs=[
            tok,
            _mod_spec(mod_base, d), _mod_spec(mod_base + 1, d), _mod_spec(mod_base + 2, d),
            _const_spec((1, d)), _const_spec((1, d)),
            _resident_spec((d, d_ff)), _resident_spec((d, d_ff)), _resident_spec((d_ff, d)),
        ],
        out_specs=tok,
        out_shape=jax.ShapeDtypeStruct((bsz, s, d), F32),
        compiler_params=pltpu.CompilerParams(
            dimension_semantics=("arbitrary", "arbitrary"), vmem_limit_bytes=VMEM_LIMIT),
        name="ffn_half_step",
    )(x, mod4, mod4, mod4, pre_g, post_g, wg, wu, wd)


_SSD_COLS = SSD_INNER + SSD_XBC
MXU_COLS = 256


def _inproj_kernel(x_ref, sh_ref, sc_ref, pre_ref, w_ssd_ref, w_dt_ref, w_attn_ref,
                   z_ref, xbc_ref, dt_ref, q1_ref, q2_ref, k_ref, vt_ref):
    h = (_rms(x_ref[...], pre_ref[...]) * (1.0 + sc_ref[...]) + sh_ref[...]).astype(BF16)
    z_ref[...] = _dot(h, w_ssd_ref[:, 0:SSD_INNER])
    xbc_ref[...] = _dot(h, w_ssd_ref[:, SSD_INNER:_SSD_COLS])
    dt_ref[...] = _dot(h, w_dt_ref[...])
    q = (_dot(h, w_attn_ref[:, 0:ATTN_WIDTH]) * (ATTN_QK_DIM ** -0.5 * LOG2E)).astype(BF16)
    lane = lax.broadcasted_iota(jnp.int32, q.shape, 1)
    is_map1 = lane % (2 * ATTN_QK_DIM) < ATTN_QK_DIM
    zero = jnp.zeros_like(q)
    q1_ref[...] = jnp.where(is_map1, q, zero)
    q2_ref[...] = jnp.where(is_map1, zero, q)
    k_ref[...] = _dot(h, w_attn_ref[:, ATTN_WIDTH:2 * ATTN_WIDTH]).astype(BF16)
    ones = jnp.ones((V_AUG - ATTN_V_DIM, x_ref.shape[0]), BF16)
    heads_per_dot = MXU_COLS // ATTN_V_DIM
    for h0 in range(0, ATTN_HEADS, heads_per_dot):
        c0 = 2 * ATTN_WIDTH + h0 * ATTN_V_DIM
        v_grp = _dot(h, w_attn_ref[:, c0:c0 + MXU_COLS])
        for hh in range(heads_per_dot):
            v_h = v_grp[:, hh * ATTN_V_DIM:(hh + 1) * ATTN_V_DIM]
            vt_ref[h0 + hh, 0:ATTN_V_DIM, :] = v_h.T.astype(BF16)
            vt_ref[h0 + hh, ATTN_V_DIM:V_AUG, :] = ones


def _inproj(x, mod4, pre_g, w_ssd, w_dt, w_attn, tm):
    bsz, s, d = x.shape

    def tok(width):
        return pl.BlockSpec((None, tm, width), lambda b, t: (b, t, 0))

    def out(width, dtype):
        return jax.ShapeDtypeStruct((bsz, s, width), dtype)

    return pl.pallas_call(
        _inproj_kernel,
        grid=(bsz, s // tm),
        in_specs=[tok(d), _mod_spec(3, d), _mod_spec(4, d), _const_spec((1, d)),
                  _resident_spec((d, _SSD_COLS)), _resident_spec((d, DT_PAD)),
                  _resident_spec((d, 3 * ATTN_WIDTH))],
        out_specs=[tok(SSD_INNER), tok(SSD_XBC), tok(DT_PAD),
                   tok(ATTN_WIDTH), tok(ATTN_WIDTH), tok(ATTN_WIDTH),
                   pl.BlockSpec((None, None, ATTN_HEADS, V_AUG, tm),
                                lambda b, t: (b, t, 0, 0, 0))],
        out_shape=[out(SSD_INNER, F32), out(SSD_XBC, F32), out(DT_PAD, F32),
                   out(ATTN_WIDTH, BF16), out(ATTN_WIDTH, BF16), out(ATTN_WIDTH, BF16),
                   jax.ShapeDtypeStruct((bsz, s // tm, ATTN_HEADS, V_AUG, tm), BF16)],
        compiler_params=pltpu.CompilerParams(
            dimension_semantics=("arbitrary", "arbitrary"), vmem_limit_bytes=VMEM_LIMIT),
        name="mixer_in_proj",
    )(x, mod4, mod4, pre_g, w_ssd, w_dt, w_attn)


_CONV_HALO = 8


def _ssd_kernel(xbc_ref, dt_ref, z_ref, cw_ref, cb_ref, dtb_ref, alog_ref, dskip_ref, ng_ref,
                y_ref, halo_ref, hstate, e64, e128, tril):
    q = SSD_CHUNK

    @pl.when(pl.program_id(1) == 0)
    def _init():
        halo_ref[...] = jnp.zeros((_CONV_HALO, SSD_XBC), F32)
        hstate[...] = jnp.zeros(hstate.shape, F32)
        r = lax.broadcasted_iota(jnp.int32, (LANES, SSD_INNER), 0)
        c = lax.broadcasted_iota(jnp.int32, (LANES, SSD_INNER), 1)
        e64[...] = (r == c // SSD_HEAD_DIM).astype(BF16)
        r = lax.broadcasted_iota(jnp.int32, (LANES, SSD_HEADS * LANES), 0)
        c = lax.broadcasted_iota(jnp.int32, (LANES, SSD_HEADS * LANES), 1)
        e128[...] = (r == c // LANES).astype(BF16)
        r = lax.broadcasted_iota(jnp.int32, (q, q), 0)
        c = lax.broadcasted_iota(jnp.int32, (q, q), 1)
        tril[...] = (r >= c).astype(BF16)

    def chunk(ci, carry):
        rows = pl.ds(pl.multiple_of(ci * q, q), q)
        _ssd_chunk(xbc_ref[rows, :], dt_ref[rows, :], z_ref[rows, :], cw_ref, cb_ref, dtb_ref,
                   alog_ref, dskip_ref, ng_ref, y_ref.at[rows, :], halo_ref, hstate, e64, e128,
                   tril)
        return carry

    lax.fori_loop(0, xbc_ref.shape[0] // q, chunk, 0)


def _ssd_chunk(u, dt_raw, z, cw_ref, cb_ref, dtb_ref, alog_ref, dskip_ref, ng_ref,
               y_ref, halo_ref, hstate, e64, e128, tril):
    q = SSD_CHUNK
    gw = SSD_INNER // SSD_GROUPS
    hpg = SSD_HEADS // SSD_GROUPS

    halo = halo_ref[...]
    row = lax.broadcasted_iota(jnp.int32, (_CONV_HALO, SSD_XBC), 0)
    acc = cb_ref[...] + cw_ref[SSD_CONV - 1:SSD_CONV, :] * u
    for shift in range(1, SSD_CONV):
        rolled = pltpu.roll(u, shift, 0)
        head = jnp.where(row < shift, pltpu.roll(halo, shift, 0), rolled[0:_CONV_HALO, :])
        shifted = jnp.concatenate([head, rolled[_CONV_HALO:, :]], axis=0)
        acc = acc + cw_ref[SSD_CONV - 1 - shift:SSD_CONV - shift, :] * shifted
    halo_ref[...] = u[q - _CONV_HALO:q, :]
    xc = _silu(acc)
    xs = xc[:, :SSD_INNER]
    bm = xc[:, SSD_INNER:SSD_INNER + SSD_GROUPS * SSD_STATE]
    cm = xc[:, SSD_INNER + SSD_GROUPS * SSD_STATE:]

    dt_in = dt_raw + dtb_ref[...]
    dtv = jnp.maximum(dt_in, 0.0) + jnp.log1p(jnp.exp(-jnp.abs(dt_in)))
    d_a = dtv * (-jnp.exp(alog_ref[...]) * LOG2E)
    da_hi, da_lo = _split_bf16(d_a)
    acum = _dot(tril[...], da_hi) + _dot(tril[...], da_lo)
    acum_t = acum.T

    def expand(v, e_ref):
        hi, lo = _split_bf16(v)
        return _dot(hi, e_ref[...]) + _dot(lo, e_ref[...])

    dt_x = expand(dtv, e64)
    acum_x = expand(acum, e64)
    acum_x128 = expand(acum, e128)
    alast_x = acum_x[q - 1:q, :]
    ea_x = jnp.exp2(acum_x)
    dte_x = jnp.exp2(alast_x - acum_x)
    cd_x = jnp.exp2(alast_x)

    xdt = xs * dt_x
    xw = (xdt * dte_x).astype(BF16)
    causal = (lax.broadcasted_iota(jnp.int32, (q, q), 0)
              >= lax.broadcasted_iota(jnp.int32, (q, q), 1))
    lane = lax.broadcasted_iota(jnp.int32, (q, LANES), 1)
    low_half = lane < SSD_HEAD_DIM

    y_diag = []
    y_off = []
    for g in range(SSD_GROUPS):
        bg = bm[:, g * SSD_STATE:(g + 1) * SSD_STATE]
        cg = cm[:, g * SSD_STATE:(g + 1) * SSD_STATE].astype(BF16)
        cb = _dot_nt(cg, bg.astype(BF16))
        for pr in range(hpg // 2):
            blk = g * (hpg // 2) + pr
            ms = []
            for hh in (2 * blk, 2 * blk + 1):
                seg = acum_x128[:, hh * LANES:(hh + 1) * LANES] - acum_t[hh:hh + 1, :]
                decay = jnp.exp2(jnp.where(causal, seg, -jnp.inf))
                ms.append((cb * decay).astype(BF16))
            xb = xdt[:, blk * LANES:(blk + 1) * LANES]
            x_lo = jnp.where(low_half, xb, 0.0).astype(BF16)
            x_hi = jnp.where(low_half, 0.0, xb).astype(BF16)
            y_diag.append(_dot(jnp.concatenate(ms, axis=1),
                               jnp.concatenate([x_lo, x_hi], axis=0)))
        h_old = hstate[g]
        y_off.append(_dot(cg, h_old.astype(BF16)))
        new_state = _dot(bg.T.astype(BF16), xw[:, g * gw:(g + 1) * gw])
        hstate[g] = h_old * cd_x[:, g * gw:(g + 1) * gw] + new_state

    y = (jnp.concatenate(y_diag, axis=1) + jnp.concatenate(y_off, axis=1) * ea_x
         + dskip_ref[...] * xs)
    y = y * _silu(z)
    outs = []
    for g in range(SSD_GROUPS):
        outs.append(_rms(y[:, g * gw:(g + 1) * gw], ng_ref[:, g * gw:(g + 1) * gw]))
    y_ref[...] = jnp.concatenate(outs, axis=1).astype(y_ref.dtype)


def _ssd(xbc, dt, z, conv_w, conv_b, dt_bias, a_log, d_skip, norm_g, tm):
    bsz, s, _ = xbc.shape
    q = SSD_CHUNK

    def tok(width):
        return pl.BlockSpec((None, tm, width), lambda b, t: (b, t, 0))

    return pl.pallas_call(
        _ssd_kernel,
        grid=(bsz, s // tm),
        in_specs=[tok(SSD_XBC), tok(DT_PAD), tok(SSD_INNER),
                  _const_spec((SSD_CONV, SSD_XBC)), _const_spec((1, SSD_XBC)),
                  _const_spec((1, DT_PAD)), _const_spec((1, DT_PAD)),
                  _const_spec((1, SSD_INNER)), _const_spec((1, SSD_INNER))],
        out_specs=tok(SSD_INNER),
        out_shape=jax.ShapeDtypeStruct((bsz, s, SSD_INNER), BF16),
        scratch_shapes=[
            pltpu.VMEM((_CONV_HALO, SSD_XBC), F32),
            pltpu.VMEM((SSD_GROUPS, SSD_STATE, SSD_INNER // SSD_GROUPS), F32),
            pltpu.VMEM((LANES, SSD_INNER), BF16),
            pltpu.VMEM((LANES, SSD_HEADS * LANES), BF16),
            pltpu.VMEM((q, q), BF16),
        ],
        compiler_params=pltpu.CompilerParams(
            dimension_semantics=("arbitrary", "arbitrary"), vmem_limit_bytes=VMEM_LIMIT),
        name="ssd_scan",
    )(xbc, dt, z, conv_w, conv_b, dt_bias, a_log, d_skip, norm_g)


def _bias_kernel(tab_ref, o_ref, *, t):
    h = pl.program_id(0)
    kk = lax.broadcasted_iota(jnp.int32, (2 * t, t), 0)
    qq = lax.broadcasted_iota(jnp.int32, (2 * t, t), 1)
    rel = qq - kk + t
    n = jnp.maximum(rel, 0)
    max_exact = REL_BUCKETS // 2
    nf = jnp.maximum(n, 1).astype(F32)
    large = max_exact + (jnp.log(nf / max_exact) / math.log(REL_MAX_DIST / max_exact)
                         * (REL_BUCKETS - max_exact)).astype(jnp.int32)
    large = jnp.minimum(large, REL_BUCKETS - 1)
    bucket = jnp.where(n < max_exact, n, large)
    far = tab_ref[REL_BUCKETS - 1, h]
    acc = jnp.zeros((2 * t, t), F32)
    for b in range(REL_BUCKETS - 1):
        acc = jnp.where(bucket == b, (tab_ref[b, h] - far) * LOG2E, acc)
    o_ref[...] = jnp.where(rel >= 0, acc, -jnp.inf)


def _bias_tiles(rel_bias, t):
    return pl.pallas_call(
        functools.partial(_bias_kernel, t=t),
        grid=(ATTN_HEADS,),
        in_specs=[pl.BlockSpec(memory_space=pltpu.SMEM)],
        out_specs=pl.BlockSpec((None, 2 * t, t), lambda h: (h, 0, 0)),
        out_shape=jax.ShapeDtypeStruct((ATTN_HEADS, 2 * t, t), F32),
        compiler_params=pltpu.CompilerParams(
            dimension_semantics=("arbitrary",), vmem_limit_bytes=VMEM_LIMIT),
        name="rel_bias_tiles",
    )(rel_bias)


def _attn_kernel(lq1_ref, lk1_ref, lq2_ref, lk2_ref, subg_ref, q1_ref, q2_ref, k_ref, vt_ref,
                 band_ref, o_ref, m_sc, acc_sc, s_a, s_b, mx_a, mx_b, *, tq, lam_init):
    tk = tq // 2
    nq = q1_ref.shape[0] // tq
    q_refs = (q1_ref, q2_ref)
    lam = (jnp.exp(jnp.sum(lq1_ref[...] * lk1_ref[...], axis=-1, keepdims=True))
           - jnp.exp(jnp.sum(lq2_ref[...] * lk2_ref[...], axis=-1, keepdims=True))
           + lam_init)

    def add_bias(buf, key_off, c0=0):
        for mp in range(2):
            for r0 in range(0, tk, LANES):
                for cb in range(c0, tq, LANES):
                    rel0 = cb - r0 - key_off
                    blk = (slice(r0, r0 + LANES), slice(cb, cb + LANES))
                    if rel0 < 0:
                        buf[mp, blk[0], blk[1]] = jnp.full((LANES, LANES), -jnp.inf, F32)
                    elif rel0 == 0:
                        buf[mp, blk[0], blk[1]] += band_ref[LANES:2 * LANES, :]
                    elif rel0 == LANES:
                        buf[mp, blk[0], blk[1]] += band_ref[0:LANES, :]

    def qk(i, j, dst, c0=0, mx_dst=None):
        kt = k_ref[pl.ds(pl.multiple_of(j * tk, tk), tk), :]
        for mp in range(2):
            qt = q_refs[mp][pl.ds(pl.multiple_of(i * tq + c0, tq - c0), tq - c0), :]
            sc = _dot_nt(kt, qt)
            dst[mp, :, c0:tq] = sc
            if mx_dst is not None:
                mx_dst[mp, :, c0:tq] = jnp.max(sc, axis=0, keepdims=True)

    def softmax_pv(src, j, c0=0, mx_src=None):
        for mp in range(2):
            s = src[mp, :, c0:tq]
            m_prev = m_sc[mp, :, c0:tq]
            if mx_src is not None:
                tile_max = mx_src[mp, :, c0:tq]
            else:
                tile_max = jnp.max(s, axis=0, keepdims=True)
            m_new = jnp.maximum(m_prev, tile_max)
            alpha = jnp.exp2(m_prev - m_new)
            p = jnp.exp2(s - m_new).astype(BF16)
            acc_sc[mp, :, c0:tq] = alpha * acc_sc[mp, :, c0:tq] + _dot(vt_ref[j], p)
            m_sc[mp, :, c0:tq] = m_new

    qk(0, 0, s_a, mx_dst=mx_a)

    def q_tile(i, carry):
        m_sc[...] = jnp.full(m_sc.shape, -jnp.inf, F32)
        acc_sc[...] = jnp.zeros(acc_sc.shape, F32)

        def far_pair(jj, c):
            j = 2 * jj
            qk(i, j + 1, s_b, mx_dst=mx_b)
            softmax_pv(s_a, j, mx_src=mx_a)
            qk(i, j + 2, s_a, mx_dst=mx_a)
            softmax_pv(s_b, j + 1, mx_src=mx_b)
            return c

        lax.fori_loop(0, i - 1, far_pair, 0)

        @pl.when(i > 0)
        def _far_prev():
            qk(i, 2 * i - 1, s_b)
            softmax_pv(s_a, 2 * i - 2, mx_src=mx_a)
            qk(i, 2 * i, s_a)
            add_bias(s_b, -tk)
            softmax_pv(s_b, 2 * i - 1)

        qk(i, 2 * i + 1, s_b, c0=tk)
        add_bias(s_a, 0)
        softmax_pv(s_a, 2 * i)
        qk(jnp.minimum(i + 1, nq - 1), 0, s_a, mx_dst=mx_a)
        add_bias(s_b, tk, c0=tk)
        softmax_pv(s_b, 2 * i + 1, c0=tk)

        inv1 = 1.0 / acc_sc[0, ATTN_V_DIM:ATTN_V_DIM + 1, :]
        inv2 = lam / acc_sc[1, ATTN_V_DIM:ATTN_V_DIM + 1, :]
        o = (acc_sc[0, 0:ATTN_V_DIM, :] * inv1 - acc_sc[1, 0:ATTN_V_DIM, :] * inv2).T
        o = _rms(o, subg_ref[...]) * (1.0 - lam_init)
        o_ref[pl.ds(pl.multiple_of(i * tq, tq), tq), :] = o.astype(o_ref.dtype)
        return carry

    lax.fori_loop(0, nq, q_tile, 0)


def _attention(q1, q2, k, vt, band, lq1, lk1, lq2, lk2, subg, tq, lam_init):
    bsz, s, _ = q1.shape
    tk = tq // 2
    small = lambda shape: pl.BlockSpec(shape, lambda b, h: (0, 0))
    head_cols = pl.BlockSpec((None, s, LANES), lambda b, h: (b, 0, h))
    return pl.pallas_call(
        functools.partial(_attn_kernel, tq=tq, lam_init=lam_init),
        grid=(bsz, ATTN_HEADS),
        in_specs=[
            small((1, ATTN_QK_DIM)), small((1, ATTN_QK_DIM)),
            small((1, ATTN_QK_DIM)), small((1, ATTN_QK_DIM)),
            small((1, ATTN_V_DIM)),
            head_cols, head_cols, head_cols,
            pl.BlockSpec((None, s // tk, None, V_AUG, tk), lambda b, h: (b, 0, h, 0, 0)),
            pl.BlockSpec((None, 2 * LANES, LANES), lambda b, h: (h, 0, 0)),
        ],
        out_specs=head_cols,
        out_shape=jax.ShapeDtypeStruct((bsz, s, ATTN_WIDTH), BF16),
        scratch_shapes=[
            pltpu.VMEM((2, 1, tq), F32),
            pltpu.VMEM((2, V_AUG, tq), F32),
            pltpu.VMEM((2, tk, tq), F32),
            pltpu.VMEM((2, tk, tq), F32),
            pltpu.VMEM((2, 1, tq), F32),
            pltpu.VMEM((2, 1, tq), F32),
        ],
        compiler_params=pltpu.CompilerParams(
            dimension_semantics=("arbitrary", "arbitrary"),
            vmem_limit_bytes=VMEM_LIMIT),
        name="diff_attention",
    )(lq1, lk1, lq2, lk2, subg, q1, q2, k, vt, band)


def _outproj_kernel(x_ref, ys_ref, ya_ref, gt_ref, post_ref, w_ref, o_ref, *, row_chains):
    k_ssd = ys_ref.shape[-1]
    rows = x_ref.shape[0] // row_chains
    for r in range(row_chains):
        sp = slice(r * rows, (r + 1) * rows)
        m = _dot(ys_ref[sp, :], w_ref[0:k_ssd, :]) + _dot(ya_ref[sp, :], w_ref[k_ssd:, :])
        o_ref[sp, :] = x_ref[sp, :] + gt_ref[...] * _rms(m, post_ref[...])


def _outproj(x, y_ssd, y_attn, mod4, post_g, w_out, tm, row_chains):
    bsz, s, d = x.shape

    def tok(width):
        return pl.BlockSpec((None, tm, width), lambda b, t: (b, t, 0))

    return pl.pallas_call(
        functools.partial(_outproj_kernel, row_chains=row_chains),
        grid=(bsz, s // tm),
        in_specs=[tok(d), tok(SSD_INNER), tok(ATTN_WIDTH), _mod_spec(5, d),
                  _const_spec((1, d)), _resident_spec((SSD_INNER + ATTN_WIDTH, d))],
        out_specs=tok(d),
        out_shape=jax.ShapeDtypeStruct((bsz, s, d), F32),
        compiler_params=pltpu.CompilerParams(
            dimension_semantics=("arbitrary", "arbitrary"), vmem_limit_bytes=VMEM_LIMIT),
        name="mixer_out_proj",
    )(x, y_ssd, y_attn, mod4, post_g, w_out)


def _tiles(s):
    return dict(
        tm_ffn=min(1024, s),
        ffn_chains=4,
        tm_proj=min(1024, s),
        proj_chains=2,
        tm_ssd=min(512, s),
        ff_chunk=2816,
        tq_attn=min(1024, s),
        tn_mod=1152,
    )


def kernel(x, c, w_ada, b_ada, ffn1_pre_g, ffn1_post_g, ffn1_w_gate, ffn1_w_up, ffn1_w_down, mix_pre_g, mix_post_g, w_in, conv_w, conv_b, dt_bias, a_log, d_skip, ssd_norm_g, lambda_q1, lambda_k1, lambda_q2, lambda_k2, subln_g, w_out, ffn2_pre_g, ffn2_post_g, ffn2_w_gate, ffn2_w_up, ffn2_w_down, rel_bias):
    bsz, s, d = x.shape
    depth = w_ada.shape[0]
    cfg = _tiles(s)
    row = lambda v: v.reshape(1, -1)

    c_pad = jnp.zeros((8, d), F32).at[:bsz].set(c)
    band = _bias_tiles(rel_bias, LANES)

    for l in range(depth):
        mod = _modulation(c_pad, w_ada[l], row(b_ada[l]), cfg["tn_mod"])
        mod4 = mod[:bsz].reshape(bsz, N_MOD, 1, d)

        x = _ffn(x, mod4, 0, row(ffn1_pre_g[l]), row(ffn1_post_g[l]),
                 ffn1_w_gate[l].astype(BF16), ffn1_w_up[l].astype(BF16),
                 ffn1_w_down[l].astype(BF16), cfg["tm_ffn"], cfg["ff_chunk"],
                 cfg["ffn_chains"])

        w = w_in[l]
        dt0 = _SSD_COLS
        w_ssd = w[:, :dt0].astype(BF16)
        w_dt = jnp.pad(w[:, dt0:dt0 + SSD_HEADS], ((0, 0), (0, DT_PAD - SSD_HEADS))).astype(BF16)
        w_attn = w[:, dt0 + SSD_HEADS:].astype(BF16)
        z, xbc, dt, q1, q2, k, vt = _inproj(x, mod4, row(mix_pre_g[l]), w_ssd, w_dt, w_attn,
                                            cfg["tq_attn"] // 2)

        pad16 = lambda vec: jnp.zeros((1, DT_PAD), F32).at[0, :SSD_HEADS].set(vec)
        y_ssd = _ssd(xbc, dt, z, conv_w[l], row(conv_b[l]), pad16(dt_bias[l]), pad16(a_log[l]),
                     row(jnp.repeat(d_skip[l], SSD_HEAD_DIM)), row(ssd_norm_g[l]),
                     cfg["tm_ssd"])

        lam_init = 0.8 - 0.6 * math.exp(-0.3 * l)
        y_attn = _attention(q1, q2, k, vt, band, row(lambda_q1[l]), row(lambda_k1[l]),
                            row(lambda_q2[l]), row(lambda_k2[l]), row(subln_g[l]),
                            cfg["tq_attn"], lam_init)

        x = _outproj(x, y_ssd, y_attn, mod4, row(mix_post_g[l]), w_out[l].astype(BF16),
                     cfg["tm_proj"], cfg["proj_chains"])

        x = _ffn(x, mod4, 6, row(ffn2_pre_g[l]), row(ffn2_post_g[l]),
                 ffn2_w_gate[l].astype(BF16), ffn2_w_up[l].astype(BF16),
                 ffn2_w_down[l].astype(BF16), cfg["tm_ffn"], cfg["ff_chunk"],
                 cfg["ffn_chains"])
    return x
```

```python
import functools
import math

import jax
import jax.numpy as jnp
from jax import lax
from jax.experimental import pallas as pl
from jax.experimental.pallas import tpu as pltpu

F32 = jnp.float32
BF16 = jnp.bfloat16

LANES = 128
EPS = 1e-6

SSD_HEADS = 16
SSD_HEAD_DIM = 64
SSD_INNER = SSD_HEADS * SSD_HEAD_DIM
SSD_GROUPS = 2
SSD_STATE = 128
SSD_CONV = 4
SSD_CHUNK = 128
SSD_XBC = SSD_INNER + 2 * SSD_GROUPS * SSD_STATE
ATTN_HEADS = 8
ATTN_QK_DIM = 64
ATTN_V_DIM = 128
ATTN_WIDTH = ATTN_HEADS * ATTN_V_DIM
REL_BUCKETS = 32
REL_MAX_DIST = 128
N_MOD = 9
DT_PAD = LANES
BF16_SUBLANES = 16
V_AUG = ATTN_V_DIM + BF16_SUBLANES
LOG2E = 1.4426950408889634

VMEM_LIMIT = 56 * 1024 * 1024


def _dot(a, b):
    return jnp.dot(a, b, preferred_element_type=F32)


def _dot_nt(a, b):
    return lax.dot_general(a, b, (((1,), (1,)), ((), ())), preferred_element_type=F32)


def _rms(x, g):
    return x * lax.rsqrt(jnp.mean(x * x, axis=-1, keepdims=True) + EPS) * g


def _silu(x):
    half = 0.5 * x
    return half + half * jnp.tanh(half)


def _split_bf16(v):
    hi = v.astype(BF16)
    lo = (v - hi.astype(F32)).astype(BF16)
    return hi, lo


def _mod_kernel(c_ref, w_ref, b_ref, o_ref):
    cs = _silu(c_ref[...]).astype(BF16)
    o_ref[...] = _dot(cs, w_ref[...].astype(BF16)) + b_ref[...]


def _modulation(c_pad, w_ada, b_ada, tn):
    rows, d = c_pad.shape
    n = w_ada.shape[1]
    return pl.pallas_call(
        _mod_kernel,
        grid=(n // tn,),
        in_specs=[
            pl.BlockSpec((rows, d), lambda j: (0, 0)),
            pl.BlockSpec((d, tn), lambda j: (0, j)),
            pl.BlockSpec((1, tn), lambda j: (0, j)),
        ],
        out_specs=pl.BlockSpec((rows, tn), lambda j: (0, j)),
        out_shape=jax.ShapeDtypeStruct((rows, n), F32),
        compiler_params=pltpu.CompilerParams(
            dimension_semantics=("arbitrary",), vmem_limit_bytes=VMEM_LIMIT),
        name="adaln_mod",
    )(c_pad, w_ada, b_ada)


def _mod_spec(idx, d):
    return pl.BlockSpec((None, None, 1, d), lambda b, t: (b, idx, 0, 0))


def _const_spec(shape):
    return pl.BlockSpec(shape, lambda b, t: tuple(0 for _ in shape))


def _ffn_kernel(x_ref, sh_ref, sc_ref, gt_ref, pre_ref, post_ref, wg_ref, wu_ref, wd_ref,
                o_ref, *, ff_chunk, row_chains):
    tm = x_ref.shape[0]
    rows = tm // row_chains
    d_ff = wg_ref.shape[1]
    spans = [slice(r * rows, (r + 1) * rows) for r in range(row_chains)]
    hs = [(_rms(x_ref[sp, :], pre_ref[...]) * (1.0 + sc_ref[...]) + sh_ref[...]).astype(BF16)
          for sp in spans]
    ys = [None] * row_chains
    for c0 in range(0, d_ff, ff_chunk):
        for r in range(row_chains):
            g = _dot(hs[r], wg_ref[:, c0:c0 + ff_chunk])
            u = _dot(hs[r], wu_ref[:, c0:c0 + ff_chunk])
            a = (_silu(g) * u).astype(BF16)
            part = _dot(a, wd_ref[c0:c0 + ff_chunk, :])
            ys[r] = part if ys[r] is None else ys[r] + part
    for r, sp in enumerate(spans):
        o_ref[sp, :] = x_ref[sp, :] + (0.5 * gt_ref[...]) * _rms(ys[r], post_ref[...])


def _resident_spec(shape):
    return pl.BlockSpec(shape, lambda b, t: tuple(0 for _ in shape),
                        pipeline_mode=pl.Buffered(1))


def _ffn(x, mod4, mod_base, pre_g, post_g, wg, wu, wd, tm, ff_chunk, row_chains):
    bsz, s, d = x.shape
    d_ff = wg.shape[1]
    tok = pl.BlockSpec((None, tm, d), lambda b, t: (b, t, 0))
    return pl.pallas_call(
        functools.partial(_ffn_kernel, ff_chunk=ff_chunk, row_chains=row_chains),
        grid=(bsz, s // tm),
        in_specs=[
            tok,
            _mod_spec(mod_base, d), _mod_spec(mod_base + 1, d), _mod_spec(mod_base + 2, d),
            _const_spec((1, d)), _const_spec((1, d)),
            _resident_spec((d, d_ff)), _resident_spec((d, d_ff)), _resident_spec((d_ff, d)),
        ],
        out_specs=tok,
        out_shape=jax.ShapeDtypeStruct((bsz, s, d), F32),
        compiler_params=pltpu.CompilerParams(
            dimension_semantics=("arbitrary", "arbitrary"), vmem_limit_bytes=VMEM_LIMIT),
        name="ffn_half_step",
    )(x, mod4, mod4, mod4, pre_g, post_g, wg, wu, wd)


_SSD_COLS = SSD_INNER + SSD_XBC
MXU_COLS = 256


def _inproj_kernel(x_ref, sh_ref, sc_ref, pre_ref, w_ssd_ref, w_dt_ref, w_attn_ref,
                   z_ref, xbc_ref, dt_ref, q1_ref, q2_ref, k_ref, vt_ref):
    h = (_rms(x_ref[...], pre_ref[...]) * (1.0 + sc_ref[...]) + sh_ref[...]).astype(BF16)
    z_ref[...] = _dot(h, w_ssd_ref[:, 0:SSD_INNER])
    xbc_ref[...] = _dot(h, w_ssd_ref[:, SSD_INNER:_SSD_COLS])
    dt_ref[...] = _dot(h, w_dt_ref[...])
    q = (_dot(h, w_attn_ref[:, 0:ATTN_WIDTH]) * (ATTN_QK_DIM ** -0.5 * LOG2E)).astype(BF16)
    lane = lax.broadcasted_iota(jnp.int32, q.shape, 1)
    is_map1 = lane % (2 * ATTN_QK_DIM) < ATTN_QK_DIM
    zero = jnp.zeros_like(q)
    q1_ref[...] = jnp.where(is_map1, q, zero)
    q2_ref[...] = jnp.where(is_map1, zero, q)
    k_ref[...] = _dot(h, w_attn_ref[:, ATTN_WIDTH:2 * ATTN_WIDTH]).astype(BF16)
    ones = jnp.ones((V_AUG - ATTN_V_DIM, x_ref.shape[0]), BF16)
    heads_per_dot = MXU_COLS // ATTN_V_DIM
    for h0 in range(0, ATTN_HEADS, heads_per_dot):
        c0 = 2 * ATTN_WIDTH + h0 * ATTN_V_DIM
        v_grp = _dot(h, w_attn_ref[:, c0:c0 + MXU_COLS])
        for hh in range(heads_per_dot):
            v_h = v_grp[:, hh * ATTN_V_DIM:(hh + 1) * ATTN_V_DIM]
            vt_ref[h0 + hh, 0:ATTN_V_DIM, :] = v_h.T.astype(BF16)
            vt_ref[h0 + hh, ATTN_V_DIM:V_AUG, :] = ones


def _inproj(x, mod4, pre_g, w_ssd, w_dt, w_attn, tm):
    bsz, s, d = x.shape

    def tok(width):
        return pl.BlockSpec((None, tm, width), lambda b, t: (b, t, 0))

    def out(width, dtype):
        return jax.ShapeDtypeStruct((bsz, s, width), dtype)

    return pl.pallas_call(
        _inproj_kernel,
        grid=(bsz, s // tm),
        in_specs=[tok(d), _mod_spec(3, d), _mod_spec(4, d), _const_spec((1, d)),
                  _resident_spec((d, _SSD_COLS)), _resident_spec((d, DT_PAD)),
                  _resident_spec((d, 3 * ATTN_WIDTH))],
        out_specs=[tok(SSD_INNER), tok(SSD_XBC), tok(DT_PAD),
                   tok(ATTN_WIDTH), tok(ATTN_WIDTH), tok(ATTN_WIDTH),
                   pl.BlockSpec((None, None, ATTN_HEADS, V_AUG, tm),
                                lambda b, t: (b, t, 0, 0, 0))],
        out_shape=[out(SSD_INNER, F32), out(SSD_XBC, F32), out(DT_PAD, F32),
                   out(ATTN_WIDTH, BF16), out(ATTN_WIDTH, BF16), out(ATTN_WIDTH, BF16),
                   jax.ShapeDtypeStruct((bsz, s // tm, ATTN_HEADS, V_AUG, tm), BF16)],
        compiler_params=pltpu.CompilerParams(
            dimension_semantics=("arbitrary", "arbitrary"), vmem_limit_bytes=VMEM_LIMIT),
        name="mixer_in_proj",
    )(x, mod4, mod4, pre_g, w_ssd, w_dt, w_attn)


_CONV_HALO = 8


def _ssd_kernel(xbc_ref, dt_ref, z_ref, cw_ref, cb_ref, dtb_ref, alog_ref, dskip_ref, ng_ref,
                y_ref, halo_ref, hstate, e64, e128, tril):
    q = SSD_CHUNK

    @pl.when(pl.program_id(1) == 0)
    def _init():
        halo_ref[...] = jnp.zeros((_CONV_HALO, SSD_XBC), F32)
        hstate[...] = jnp.zeros(hstate.shape, F32)
        r = lax.broadcasted_iota(jnp.int32, (LANES, SSD_INNER), 0)
        c = lax.broadcasted_iota(jnp.int32, (LANES, SSD_INNER), 1)
        e64[...] = (r == c // SSD_HEAD_DIM).astype(BF16)
        r = lax.broadcasted_iota(jnp.int32, (LANES, SSD_HEADS * LANES), 0)
        c = lax.broadcasted_iota(jnp.int32, (LANES, SSD_HEADS * LANES), 1)
        e128[...] = (r == c // LANES).astype(BF16)
        r = lax.broadcasted_iota(jnp.int32, (q, q), 0)
        c = lax.broadcasted_iota(jnp.int32, (q, q), 1)
        tril[...] = (r >= c).astype(BF16)

    def chunk(ci, carry):
        rows = pl.ds(pl.multiple_of(ci * q, q), q)
        _ssd_chunk(xbc_ref[rows, :], dt_ref[rows, :], z_ref[rows, :], cw_ref, cb_ref, dtb_ref,
                   alog_ref, dskip_ref, ng_ref, y_ref.at[rows, :], halo_ref, hstate, e64, e128,
                   tril)
        return carry

    lax.fori_loop(0, xbc_ref.shape[0] // q, chunk, 0)


def _ssd_chunk(u, dt_raw, z, cw_ref, cb_ref, dtb_ref, alog_ref, dskip_ref, ng_ref,
               y_ref, halo_ref, hstate, e64, e128, tril):
    q = SSD_CHUNK
    gw = SSD_INNER // SSD_GROUPS
    hpg = SSD_HEADS // SSD_GROUPS

    halo = halo_ref[...]
    row = lax.broadcasted_iota(jnp.int32, (_CONV_HALO, SSD_XBC), 0)
    acc = cb_ref[...] + cw_ref[SSD_CONV - 1:SSD_CONV, :] * u
    for shift in range(1, SSD_CONV):
        rolled = pltpu.roll(u, shift, 0)
        head = jnp.where(row < shift, pltpu.roll(halo, shift, 0), rolled[0:_CONV_HALO, :])
        shifted = jnp.concatenate([head, rolled[_CONV_HALO:, :]], axis=0)
        acc = acc + cw_ref[SSD_CONV - 1 - shift:SSD_CONV - shift, :] * shifted
    halo_ref[...] = u[q - _CONV_HALO:q, :]
    xc = _silu(acc)
    xs = xc[:, :SSD_INNER]
    bm = xc[:, SSD_INNER:SSD_INNER + SSD_GROUPS * SSD_STATE]
    cm = xc[:, SSD_INNER + SSD_GROUPS * SSD_STATE:]

    dt_in = dt_raw + dtb_ref[...]
    dtv = jnp.maximum(dt_in, 0.0) + jnp.log1p(jnp.exp(-jnp.abs(dt_in)))
    d_a = dtv * (-jnp.exp(alog_ref[...]) * LOG2E)
    da_hi, da_lo = _split_bf16(d_a)
    acum = _dot(tril[...], da_hi) + _dot(tril[...], da_lo)
    acum_t = acum.T

    def expand(v, e_ref):
        hi, lo = _split_bf16(v)
        return _dot(hi, e_ref[...]) + _dot(lo, e_ref[...])

    dt_x = expand(dtv, e64)
    acum_x = expand(acum, e64)
    acum_x128 = expand(acum, e128)
    alast_x = acum_x[q - 1:q, :]
    ea_x = jnp.exp2(acum_x)
    dte_x = jnp.exp2(alast_x - acum_x)
    cd_x = jnp.exp2(alast_x)

    xdt = xs * dt_x
    xw = (xdt * dte_x).astype(BF16)
    causal = (lax.broadcasted_iota(jnp.int32, (q, q), 0)
              >= lax.broadcasted_iota(jnp.int32, (q, q), 1))
    lane = lax.broadcasted_iota(jnp.int32, (q, LANES), 1)
    low_half = lane < SSD_HEAD_DIM

    y_diag = []
    y_off = []
    for g in range(SSD_GROUPS):
        bg = bm[:, g * SSD_STATE:(g + 1) * SSD_STATE]
        cg = cm[:, g * SSD_STATE:(g + 1) * SSD_STATE].astype(BF16)
        cb = _dot_nt(cg, bg.astype(BF16))
        for pr in range(hpg // 2):
            blk = g * (hpg // 2) + pr
            ms = []
            for hh in (2 * blk, 2 * blk + 1):
                seg = acum_x128[:, hh * LANES:(hh + 1) * LANES] - acum_t[hh:hh + 1, :]
                decay = jnp.exp2(jnp.where(causal, seg, -jnp.inf))
                ms.append((cb * decay).astype(BF16))
            xb = xdt[:, blk * LANES:(blk + 1) * LANES]
            x_lo = jnp.where(low_half, xb, 0.0).astype(BF16)
            x_hi = jnp.where(low_half, 0.0, xb).astype(BF16)
            y_diag.append(_dot(jnp.concatenate(ms, axis=1),
                               jnp.concatenate([x_lo, x_hi], axis=0)))
        h_old = hstate[g]
        y_off.append(_dot(cg, h_old.astype(BF16)))
        new_state = _dot(bg.T.astype(BF16), xw[:, g * gw:(g + 1) * gw])
        hstate[g] = h_old * cd_x[:, g * gw:(g + 1) * gw] + new_state

    y = (jnp.concatenate(y_diag, axis=1) + jnp.concatenate(y_off, axis=1) * ea_x
         + dskip_ref[...] * xs)
    y = y * _silu(z)
    outs = []
    for g in range(SSD_GROUPS):
        outs.append(_rms(y[:, g * gw:(g + 1) * gw], ng_ref[:, g * gw:(g + 1) * gw]))
    y_ref[...] = jnp.concatenate(outs, axis=1).astype(y_ref.dtype)


def _ssd(xbc, dt, z, conv_w, conv_b, dt_bias, a_log, d_skip, norm_g, tm):
    bsz, s, _ = xbc.shape
    q = SSD_CHUNK

    def tok(width):
        return pl.BlockSpec((None, tm, width), lambda b, t: (b, t, 0))

    return pl.pallas_call(
        _ssd_kernel,
        grid=(bsz, s // tm),
        in_specs=[tok(SSD_XBC), tok(DT_PAD), tok(SSD_INNER),
                  _const_spec((SSD_CONV, SSD_XBC)), _const_spec((1, SSD_XBC)),
                  _const_spec((1, DT_PAD)), _const_spec((1, DT_PAD)),
                  _const_spec((1, SSD_INNER)), _const_spec((1, SSD_INNER))],
        out_specs=tok(SSD_INNER),
        out_shape=jax.ShapeDtypeStruct((bsz, s, SSD_INNER), BF16),
        scratch_shapes=[
            pltpu.VMEM((_CONV_HALO, SSD_XBC), F32),
            pltpu.VMEM((SSD_GROUPS, SSD_STATE, SSD_INNER // SSD_GROUPS), F32),
            pltpu.VMEM((LANES, SSD_INNER), BF16),
            pltpu.VMEM((LANES, SSD_HEADS * LANES), BF16),
            pltpu.VMEM((q, q), BF16),
        ],
        compiler_params=pltpu.CompilerParams(
            dimension_semantics=("arbitrary", "arbitrary"), vmem_limit_bytes=VMEM_LIMIT),
        name="ssd_scan",
    )(xbc, dt, z, conv_w, conv_b, dt_bias, a_log, d_skip, norm_g)


def _bias_kernel(tab_ref, o_ref, *, t):
    h = pl.program_id(0)
    kk = lax.broadcasted_iota(jnp.int32, (2 * t, t), 0)
    qq = lax.broadcasted_iota(jnp.int32, (2 * t, t), 1)
    rel = qq - kk + t
    n = jnp.maximum(rel, 0)
    max_exact = REL_BUCKETS // 2
    nf = jnp.maximum(n, 1).astype(F32)
    large = max_exact + (jnp.log(nf / max_exact) / math.log(REL_MAX_DIST / max_exact)
                         * (REL_BUCKETS - max_exact)).astype(jnp.int32)
    large = jnp.minimum(large, REL_BUCKETS - 1)
    bucket = jnp.where(n < max_exact, n, large)
    far = tab_ref[REL_BUCKETS - 1, h]
    acc = jnp.zeros((2 * t, t), F32)
    for b in range(REL_BUCKETS - 1):
        acc = jnp.where(bucket == b, (tab_ref[b, h] - far) * LOG2E, acc)
    o_ref[...] = jnp.where(rel >= 0, acc, -jnp.inf)


def _bias_tiles(rel_bias, t):
    return pl.pallas_call(
        functools.partial(_bias_kernel, t=t),
        grid=(ATTN_HEADS,),
        in_specs=[pl.BlockSpec(memory_space=pltpu.SMEM)],
        out_specs=pl.BlockSpec((None, 2 * t, t), lambda h: (h, 0, 0)),
        out_shape=jax.ShapeDtypeStruct((ATTN_HEADS, 2 * t, t), F32),
        compiler_params=pltpu.CompilerParams(
            dimension_semantics=("arbitrary",), vmem_limit_bytes=VMEM_LIMIT),
        name="rel_bias_tiles",
    )(rel_bias)


def _attn_kernel(lq1_ref, lk1_ref, lq2_ref, lk2_ref, subg_ref, q1_ref, q2_ref, k_ref, vt_ref,
                 band_ref, o_ref, m_sc, acc_sc, s_a, s_b, mx_a, mx_b, *, tq, tk, lam_init):
    ndiag = tq // tk
    assert ndiag % 2 == 0
    nq = q1_ref.shape[0] // tq
    q_refs = (q1_ref, q2_ref)
    lam = (jnp.exp(jnp.sum(lq1_ref[...] * lk1_ref[...], axis=-1, keepdims=True))
           - jnp.exp(jnp.sum(lq2_ref[...] * lk2_ref[...], axis=-1, keepdims=True))
           + lam_init)

    def add_bias(buf, key_off, c0=0):
        for mp in range(2):
            for r0 in range(0, tk, LANES):
                for cb in range(c0, tq, LANES):
                    rel0 = cb - r0 - key_off
                    blk = (slice(r0, r0 + LANES), slice(cb, cb + LANES))
                    if rel0 < 0:
                        buf[mp, blk[0], blk[1]] = jnp.full((LANES, LANES), -jnp.inf, F32)
                    elif rel0 == 0:
                        buf[mp, blk[0], blk[1]] += band_ref[LANES:2 * LANES, :]
                    elif rel0 == LANES:
                        buf[mp, blk[0], blk[1]] += band_ref[0:LANES, :]

    def qk(i, j, dst, c0=0, mx_dst=None):
        kt = k_ref[pl.ds(pl.multiple_of(j * tk, tk), tk), :]
        for mp in range(2):
            qt = q_refs[mp][pl.ds(pl.multiple_of(i * tq + c0, tk), tq - c0), :]
            sc = _dot_nt(kt, qt)
            dst[mp, :, c0:tq] = sc
            if mx_dst is not None:
                mx_dst[mp, :, c0:tq] = jnp.max(sc, axis=0, keepdims=True)

    def softmax_pv(src, j, c0=0, mx_src=None):
        for mp in range(2):
            s = src[mp, :, c0:tq]
            m_prev = m_sc[mp, :, c0:tq]
            if mx_src is not None:
                tile_max = mx_src[mp, :, c0:tq]
            else:
                tile_max = jnp.max(s, axis=0, keepdims=True)
            m_new = jnp.maximum(m_prev, tile_max)
            alpha = jnp.exp2(m_prev - m_new)
            p = jnp.exp2(s - m_new).astype(BF16)
            acc_sc[mp, :, c0:tq] = alpha * acc_sc[mp, :, c0:tq] + _dot(vt_ref[j], p)
            m_sc[mp, :, c0:tq] = m_new

    qk(0, 0, s_a, mx_dst=mx_a)

    def q_tile(i, carry):
        base = ndiag * i
        m_sc[...] = jnp.full(m_sc.shape, -jnp.inf, F32)
        acc_sc[...] = jnp.zeros(acc_sc.shape, F32)

        def far_pair(jj, c):
            j = 2 * jj
            qk(i, j + 1, s_b, mx_dst=mx_b)
            softmax_pv(s_a, j, mx_src=mx_a)
            qk(i, j + 2, s_a, mx_dst=mx_a)
            softmax_pv(s_b, j + 1, mx_src=mx_b)
            return c

        lax.fori_loop(0, base // 2 - 1, far_pair, 0)

        @pl.when(i > 0)
        def _far_prev():
            qk(i, base - 1, s_b)
            softmax_pv(s_a, base - 2, mx_src=mx_a)
            qk(i, base, s_a)
            add_bias(s_b, -tk)
            softmax_pv(s_b, base - 1)

        for d in range(0, ndiag, 2):
            qk(i, base + d + 1, s_b, c0=(d + 1) * tk)
            add_bias(s_a, d * tk, c0=d * tk)
            softmax_pv(s_a, base + d, c0=d * tk)
            if d + 2 < ndiag:
                qk(i, base + d + 2, s_a, c0=(d + 2) * tk)
            else:
                qk(jnp.minimum(i + 1, nq - 1), 0, s_a, mx_dst=mx_a)
            add_bias(s_b, (d + 1) * tk, c0=(d + 1) * tk)
            softmax_pv(s_b, base + d + 1, c0=(d + 1) * tk)

        inv1 = 1.0 / acc_sc[0, ATTN_V_DIM:ATTN_V_DIM + 1, :]
        inv2 = lam / acc_sc[1, ATTN_V_DIM:ATTN_V_DIM + 1, :]
        o = (acc_sc[0, 0:ATTN_V_DIM, :] * inv1 - acc_sc[1, 0:ATTN_V_DIM, :] * inv2).T
        o = _rms(o, subg_ref[...]) * (1.0 - lam_init)
        o_ref[pl.ds(pl.multiple_of(i * tq, tq), tq), :] = o.astype(o_ref.dtype)
        return carry

    lax.fori_loop(0, nq, q_tile, 0)


def _attention(q1, q2, k, vt, band, lq1, lk1, lq2, lk2, subg, tq, tk, lam_init):
    bsz, s, _ = q1.shape
    small = lambda shape: pl.BlockSpec(shape, lambda b, h: (0, 0))
    head_cols = pl.BlockSpec((None, s, LANES), lambda b, h: (b, 0, h))
    return pl.pallas_call(
        functools.partial(_attn_kernel, tq=tq, tk=tk, lam_init=lam_init),
        grid=(bsz, ATTN_HEADS),
        in_specs=[
            small((1, ATTN_QK_DIM)), small((1, ATTN_QK_DIM)),
            small((1, ATTN_QK_DIM)), small((1, ATTN_QK_DIM)),
            small((1, ATTN_V_DIM)),
            head_cols, head_cols, head_cols,
            pl.BlockSpec((None, s // tk, None, V_AUG, tk), lambda b, h: (b, 0, h, 0, 0)),
            pl.BlockSpec((None, 2 * LANES, LANES), lambda b, h: (h, 0, 0)),
        ],
        out_specs=head_cols,
        out_shape=jax.ShapeDtypeStruct((bsz, s, ATTN_WIDTH), BF16),
        scratch_shapes=[
            pltpu.VMEM((2, 1, tq), F32),
            pltpu.VMEM((2, V_AUG, tq), F32),
            pltpu.VMEM((2, tk, tq), F32),
            pltpu.VMEM((2, tk, tq), F32),
            pltpu.VMEM((2, 1, tq), F32),
            pltpu.VMEM((2, 1, tq), F32),
        ],
        compiler_params=pltpu.CompilerParams(
            dimension_semantics=("arbitrary", "arbitrary"),
            vmem_limit_bytes=VMEM_LIMIT),
        name="diff_attention",
    )(lq1, lk1, lq2, lk2, subg, q1, q2, k, vt, band)


def _outproj_kernel(x_ref, ys_ref, ya_ref, gt_ref, post_ref, w_ref, o_ref, *, row_chains):
    k_ssd = ys_ref.shape[-1]
    rows = x_ref.shape[0] // row_chains
    for r in range(row_chains):
        sp = slice(r * rows, (r + 1) * rows)
        m = _dot(ys_ref[sp, :], w_ref[0:k_ssd, :]) + _dot(ya_ref[sp, :], w_ref[k_ssd:, :])
        o_ref[sp, :] = x_ref[sp, :] + gt_ref[...] * _rms(m, post_ref[...])


def _outproj(x, y_ssd, y_attn, mod4, post_g, w_out, tm, row_chains):
    bsz, s, d = x.shape

    def tok(width):
        return pl.BlockSpec((None, tm, width), lambda b, t: (b, t, 0))

    return pl.pallas_call(
        functools.partial(_outproj_kernel, row_chains=row_chains),
        grid=(bsz, s // tm),
        in_specs=[tok(d), tok(SSD_INNER), tok(ATTN_WIDTH), _mod_spec(5, d),
                  _const_spec((1, d)), _resident_spec((SSD_INNER + ATTN_WIDTH, d))],
        out_specs=tok(d),
        out_shape=jax.ShapeDtypeStruct((bsz, s, d), F32),
        compiler_params=pltpu.CompilerParams(
            dimension_semantics=("arbitrary", "arbitrary"), vmem_limit_bytes=VMEM_LIMIT),
        name="mixer_out_proj",
    )(x, y_ssd, y_attn, mod4, post_g, w_out)


def _tiles(s):
    return dict(
        tm_ffn=min(1024, s),
        ffn_chains=4,
        tm_proj=min(1024, s),
        proj_chains=2,
        tm_ssd=min(512, s),
        ff_chunk=2816,
        tq_attn=min(2048, s),
        tk_attn=min(512, s // 2),
        tn_mod=1152,
    )


def kernel(x, c, w_ada, b_ada, ffn1_pre_g, ffn1_post_g, ffn1_w_gate, ffn1_w_up, ffn1_w_down, mix_pre_g, mix_post_g, w_in, conv_w, conv_b, dt_bias, a_log, d_skip, ssd_norm_g, lambda_q1, lambda_k1, lambda_q2, lambda_k2, subln_g, w_out, ffn2_pre_g, ffn2_post_g, ffn2_w_gate, ffn2_w_up, ffn2_w_down, rel_bias):
    bsz, s, d = x.shape
    depth = w_ada.shape[0]
    cfg = _tiles(s)
    row = lambda v: v.reshape(1, -1)

    c_pad = jnp.zeros((8, d), F32).at[:bsz].set(c)
    band = _bias_tiles(rel_bias, LANES)

    for l in range(depth):
        mod = _modulation(c_pad, w_ada[l], row(b_ada[l]), cfg["tn_mod"])
        mod4 = mod[:bsz].reshape(bsz, N_MOD, 1, d)

        x = _ffn(x, mod4, 0, row(ffn1_pre_g[l]), row(ffn1_post_g[l]),
                 ffn1_w_gate[l].astype(BF16), ffn1_w_up[l].astype(BF16),
                 ffn1_w_down[l].astype(BF16), cfg["tm_ffn"], cfg["ff_chunk"],
                 cfg["ffn_chains"])

        w = w_in[l]
        dt0 = _SSD_COLS
        w_ssd = w[:, :dt0].astype(BF16)
        w_dt = jnp.pad(w[:, dt0:dt0 + SSD_HEADS], ((0, 0), (0, DT_PAD - SSD_HEADS))).astype(BF16)
        w_attn = w[:, dt0 + SSD_HEADS:].astype(BF16)
        z, xbc, dt, q1, q2, k, vt = _inproj(x, mod4, row(mix_pre_g[l]), w_ssd, w_dt, w_attn,
                                            cfg["tk_attn"])

        pad16 = lambda vec: jnp.zeros((1, DT_PAD), F32).at[0, :SSD_HEADS].set(vec)
        y_ssd = _ssd(xbc, dt, z, conv_w[l], row(conv_b[l]), pad16(dt_bias[l]), pad16(a_log[l]),
                     row(jnp.repeat(d_skip[l], SSD_HEAD_DIM)), row(ssd_norm_g[l]),
                     cfg["tm_ssd"])

        lam_init = 0.8 - 0.6 * math.exp(-0.3 * l)
        y_attn = _attention(q1, q2, k, vt, band, row(lambda_q1[l]), row(lambda_k1[l]),
                            row(lambda_q2[l]), row(lambda_k2[l]), row(subln_g[l]),
                            cfg["tq_attn"], cfg["tk_attn"], lam_init)

        x = _outproj(x, y_ssd, y_attn, mod4, row(mix_post_g[l]), w_out[l].astype(BF16),
                     cfg["tm_proj"], cfg["proj_chains"])

        x = _ffn(x, mod4, 6, row(ffn2_pre_g[l]), row(ffn2_post_g[l]),
                 ffn2_w_gate[l].astype(BF16), ffn2_w_up[l].astype(BF16),
                 ffn2_w_down[l].astype(BF16), cfg["tm_ffn"], cfg["ff_chunk"],
                 cfg["ffn_chains"])
    return x
```

```python
import functools
import math

import jax
import jax.numpy as jnp
from jax import lax
from jax.experimental import pallas as pl
from jax.experimental.pallas import tpu as pltpu

F32 = jnp.float32
BF16 = jnp.bfloat16

LANES = 128
EPS = 1e-6

SSD_HEADS = 16
SSD_HEAD_DIM = 64
SSD_INNER = SSD_HEADS * SSD_HEAD_DIM
SSD_GROUPS = 2
SSD_STATE = 128
SSD_CONV = 4
SSD_CHUNK = 128
SSD_XBC = SSD_INNER + 2 * SSD_GROUPS * SSD_STATE
ATTN_HEADS = 8
ATTN_QK_DIM = 64
ATTN_V_DIM = 128
ATTN_WIDTH = ATTN_HEADS * ATTN_V_DIM
REL_BUCKETS = 32
REL_MAX_DIST = 128
N_MOD = 9
DT_PAD = LANES
BF16_SUBLANES = 16
V_AUG = ATTN_V_DIM + BF16_SUBLANES
LOG2E = 1.4426950408889634

VMEM_LIMIT = 56 * 1024 * 1024


def _dot(a, b):
    return jnp.dot(a, b, preferred_element_type=F32)


def _dot_nt(a, b):
    return lax.dot_general(a, b, (((1,), (1,)), ((), ())), preferred_element_type=F32)


def _rms(x, g):
    return x * lax.rsqrt(jnp.mean(x * x, axis=-1, keepdims=True) + EPS) * g


def _silu(x):
    half = 0.5 * x
    return half + half * jnp.tanh(half)


def _split_bf16(v):
    hi = v.astype(BF16)
    lo = (v - hi.astype(F32)).astype(BF16)
    return hi, lo


def _mod_kernel(c_ref, w_ref, b_ref, o_ref):
    cs = _silu(c_ref[...]).astype(BF16)
    o_ref[...] = _dot(cs, w_ref[...].astype(BF16)) + b_ref[...]


def _modulation(c_pad, w_ada, b_ada, tn):
    rows, d = c_pad.shape
    n = w_ada.shape[1]
    return pl.pallas_call(
        _mod_kernel,
        grid=(n // tn,),
        in_specs=[
            pl.BlockSpec((rows, d), lambda j: (0, 0)),
            pl.BlockSpec((d, tn), lambda j: (0, j)),
            pl.BlockSpec((1, tn), lambda j: (0, j)),
        ],
        out_specs=pl.BlockSpec((rows, tn), lambda j: (0, j)),
        out_shape=jax.ShapeDtypeStruct((rows, n), F32),
        compiler_params=pltpu.CompilerParams(
            dimension_semantics=("arbitrary",), vmem_limit_bytes=VMEM_LIMIT),
        name="adaln_mod",
    )(c_pad, w_ada, b_ada)


def _mod_spec(idx, d):
    return pl.BlockSpec((None, None, 1, d), lambda b, t: (b, idx, 0, 0))


def _const_spec(shape):
    return pl.BlockSpec(shape, lambda b, t: tuple(0 for _ in shape))


def _ffn_kernel(x_ref, sh_ref, sc_ref, gt_ref, pre_ref, post_ref, wg_ref, wu_ref, wd_ref,
                o_ref, *, ff_chunk, row_chains):
    tm = x_ref.shape[0]
    rows = tm // row_chains
    d_ff = wg_ref.shape[1]
    spans = [slice(r * rows, (r + 1) * rows) for r in range(row_chains)]
    hs = [(_rms(x_ref[sp, :], pre_ref[...]) * (1.0 + sc_ref[...]) + sh_ref[...]).astype(BF16)
          for sp in spans]
    ys = [None] * row_chains
    for c0 in range(0, d_ff, ff_chunk):
        for r in range(row_chains):
            g = _dot(hs[r], wg_ref[:, c0:c0 + ff_chunk])
            u = _dot(hs[r], wu_ref[:, c0:c0 + ff_chunk])
            a = (_silu(g) * u).astype(BF16)
            part = _dot(a, wd_ref[c0:c0 + ff_chunk, :])
            ys[r] = part if ys[r] is None else ys[r] + part
    for r, sp in enumerate(spans):
        o_ref[sp, :] = x_ref[sp, :] + (0.5 * gt_ref[...]) * _rms(ys[r], post_ref[...])


def _resident_spec(shape):
    return pl.BlockSpec(shape, lambda b, t: tuple(0 for _ in shape),
                        pipeline_mode=pl.Buffered(1))


def _ffn(x, mod4, mod_base, pre_g, post_g, wg, wu, wd, tm, ff_chunk, row_chains):
    bsz, s, d = x.shape
    d_ff = wg.shape[1]
    tok = pl.BlockSpec((None, tm, d), lambda b, t: (b, t, 0))
    return pl.pallas_call(
        functools.partial(_ffn_kernel, ff_chunk=ff_chunk, row_chains=row_chains),
        grid=(bsz, s // tm),
        in_specs=[
            tok,
            _mod_spec(mod_base, d), _mod_spec(mod_base + 1, d), _mod_spec(mod_base + 2, d),
            _const_spec((1, d)), _const_spec((1, d)),
            _resident_spec((d, d_ff)), _resident_spec((d, d_ff)), _resident_spec((d_ff, d)),
        ],
        out_specs=tok,
        out_shape=jax.ShapeDtypeStruct((bsz, s, d), F32),
        compiler_params=pltpu.CompilerParams(
            dimension_semantics=("arbitrary", "arbitrary"), vmem_limit_bytes=VMEM_LIMIT),
        name="ffn_half_step",
    )(x, mod4, mod4, mod4, pre_g, post_g, wg, wu, wd)


_SSD_COLS = SSD_INNER + SSD_XBC
MXU_COLS = 256


def _inproj_kernel(x_ref, sh_ref, sc_ref, pre_ref, w_ssd_ref, w_dt_ref, w_attn_ref,
                   z_ref, xbc_ref, dt_ref, q1_ref, q2_ref, k_ref, vt_ref):
    h = (_rms(x_ref[...], pre_ref[...]) * (1.0 + sc_ref[...]) + sh_ref[...]).astype(BF16)
    z_ref[...] = _dot(h, w_ssd_ref[:, 0:SSD_INNER])
    xbc_ref[...] = _dot(h, w_ssd_ref[:, SSD_INNER:_SSD_COLS])
    dt_ref[...] = _dot(h, w_dt_ref[...])
    q = (_dot(h, w_attn_ref[:, 0:ATTN_WIDTH]) * (ATTN_QK_DIM ** -0.5 * LOG2E)).astype(BF16)
    lane = lax.broadcasted_iota(jnp.int32, q.shape, 1)
    is_map1 = lane % (2 * ATTN_QK_DIM) < ATTN_QK_DIM
    zero = jnp.zeros_like(q)
    q1_ref[...] = jnp.where(is_map1, q, zero)
    q2_ref[...] = jnp.where(is_map1, zero, q)
    k_ref[...] = _dot(h, w_attn_ref[:, ATTN_WIDTH:2 * ATTN_WIDTH]).astype(BF16)
    ones = jnp.ones((V_AUG - ATTN_V_DIM, x_ref.shape[0]), BF16)
    heads_per_dot = MXU_COLS // ATTN_V_DIM
    for h0 in range(0, ATTN_HEADS, heads_per_dot):
        c0 = 2 * ATTN_WIDTH + h0 * ATTN_V_DIM
        v_grp = _dot(h, w_attn_ref[:, c0:c0 + MXU_COLS])
        for hh in range(heads_per_dot):
            v_h = v_grp[:, hh * ATTN_V_DIM:(hh + 1) * ATTN_V_DIM]
            vt_ref[h0 + hh, 0:ATTN_V_DIM, :] = v_h.T.astype(BF16)
            vt_ref[h0 + hh, ATTN_V_DIM:V_AUG, :] = ones


def _inproj(x, mod4, pre_g, w_ssd, w_dt, w_attn, tm):
    bsz, s, d = x.shape

    def tok(width):
        return pl.BlockSpec((None, tm, width), lambda b, t: (b, t, 0))

    def out(width, dtype):
        return jax.ShapeDtypeStruct((bsz, s, width), dtype)

    return pl.pallas_call(
        _inproj_kernel,
        grid=(bsz, s // tm),
        in_specs=[tok(d), _mod_spec(3, d), _mod_spec(4, d), _const_spec((1, d)),
                  _resident_spec((d, _SSD_COLS)), _resident_spec((d, DT_PAD)),
                  _resident_spec((d, 3 * ATTN_WIDTH))],
        out_specs=[tok(SSD_INNER), tok(SSD_XBC), tok(DT_PAD),
                   tok(ATTN_WIDTH), tok(ATTN_WIDTH), tok(ATTN_WIDTH),
                   pl.BlockSpec((None, None, ATTN_HEADS, V_AUG, tm),
                                lambda b, t: (b, t, 0, 0, 0))],
        out_shape=[out(SSD_INNER, F32), out(SSD_XBC, F32), out(DT_PAD, F32),
                   out(ATTN_WIDTH, BF16), out(ATTN_WIDTH, BF16), out(ATTN_WIDTH, BF16),
                   jax.ShapeDtypeStruct((bsz, s // tm, ATTN_HEADS, V_AUG, tm), BF16)],
        compiler_params=pltpu.CompilerParams(
            dimension_semantics=("arbitrary", "arbitrary"), vmem_limit_bytes=VMEM_LIMIT),
        name="mixer_in_proj",
    )(x, mod4, mod4, pre_g, w_ssd, w_dt, w_attn)


_CONV_HALO = 8


def _ssd_kernel(xbc_ref, dt_ref, z_ref, cw_ref, cb_ref, dtb_ref, alog_ref, dskip_ref, ng_ref,
                y_ref, halo_ref, hstate, e64, e128, tril):
    q = SSD_CHUNK

    @pl.when(pl.program_id(1) == 0)
    def _init():
        halo_ref[...] = jnp.zeros((_CONV_HALO, SSD_XBC), F32)
        hstate[...] = jnp.zeros(hstate.shape, F32)
        r = lax.broadcasted_iota(jnp.int32, (LANES, SSD_INNER), 0)
        c = lax.broadcasted_iota(jnp.int32, (LANES, SSD_INNER), 1)
        e64[...] = (r == c // SSD_HEAD_DIM).astype(BF16)
        r = lax.broadcasted_iota(jnp.int32, (LANES, SSD_HEADS * LANES), 0)
        c = lax.broadcasted_iota(jnp.int32, (LANES, SSD_HEADS * LANES), 1)
        e128[...] = (r == c // LANES).astype(BF16)
        r = lax.broadcasted_iota(jnp.int32, (q, q), 0)
        c = lax.broadcasted_iota(jnp.int32, (q, q), 1)
        tril[...] = (r >= c).astype(BF16)

    def chunk(ci, carry):
        rows = pl.ds(pl.multiple_of(ci * q, q), q)
        _ssd_chunk(xbc_ref[rows, :], dt_ref[rows, :], z_ref[rows, :], cw_ref, cb_ref, dtb_ref,
                   alog_ref, dskip_ref, ng_ref, y_ref.at[rows, :], halo_ref, hstate, e64, e128,
                   tril)
        return carry

    lax.fori_loop(0, xbc_ref.shape[0] // q, chunk, 0)


def _ssd_chunk(u, dt_raw, z, cw_ref, cb_ref, dtb_ref, alog_ref, dskip_ref, ng_ref,
               y_ref, halo_ref, hstate, e64, e128, tril):
    q = SSD_CHUNK
    gw = SSD_INNER // SSD_GROUPS
    hpg = SSD_HEADS // SSD_GROUPS

    halo = halo_ref[...]
    row = lax.broadcasted_iota(jnp.int32, (_CONV_HALO, SSD_XBC), 0)

    def shift_rows(cur, prev_tail, n):
        rolled = pltpu.roll(cur, n, 0)
        head = jnp.where(row < n, pltpu.roll(prev_tail, n, 0), rolled[0:_CONV_HALO, :])
        return jnp.concatenate([head, rolled[_CONV_HALO:, :]], axis=0)

    w0, w1, w2, w3 = (cw_ref[k:k + 1, :] for k in range(SSD_CONV))
    u1 = shift_rows(u, halo, 1)
    b_cur = w1 * u + w0 * u1
    b_tail = w1 * halo + w0 * pltpu.roll(halo, 1, 0)
    acc = (cb_ref[...] + w3 * u + w2 * u1) + shift_rows(b_cur, b_tail, 2)
    halo_ref[...] = u[q - _CONV_HALO:q, :]
    xc = _silu(acc)
    xs = xc[:, :SSD_INNER]
    bm = xc[:, SSD_INNER:SSD_INNER + SSD_GROUPS * SSD_STATE]
    cm = xc[:, SSD_INNER + SSD_GROUPS * SSD_STATE:]

    dt_in = dt_raw + dtb_ref[...]
    dtv = jnp.maximum(dt_in, 0.0) + jnp.log1p(jnp.exp(-jnp.abs(dt_in)))
    d_a = dtv * (-jnp.exp(alog_ref[...]) * LOG2E)
    da_hi, da_lo = _split_bf16(d_a)
    acum = _dot(tril[...], da_hi) + _dot(tril[...], da_lo)
    acum_t = acum.T

    def expand(v, e_ref):
        hi, lo = _split_bf16(v)
        return _dot(hi, e_ref[...]) + _dot(lo, e_ref[...])

    dt_x = expand(dtv, e64)
    acum_x = expand(acum, e64)
    acum_x128 = expand(acum, e128)
    alast_x = acum_x[q - 1:q, :]
    ea_x = jnp.exp2(acum_x)
    dte_x = jnp.exp2(alast_x - acum_x)
    cd_x = jnp.exp2(alast_x)

    xdt = xs * dt_x
    xw = (xdt * dte_x).astype(BF16)
    causal = (lax.broadcasted_iota(jnp.int32, (q, q), 0)
              >= lax.broadcasted_iota(jnp.int32, (q, q), 1))
    lane = lax.broadcasted_iota(jnp.int32, (q, LANES), 1)
    low_half = lane < SSD_HEAD_DIM

    y_diag = []
    y_off = []
    for g in range(SSD_GROUPS):
        bg = bm[:, g * SSD_STATE:(g + 1) * SSD_STATE]
        cg = cm[:, g * SSD_STATE:(g + 1) * SSD_STATE].astype(BF16)
        cb = _dot_nt(cg, bg.astype(BF16))
        for pr in range(hpg // 2):
            blk = g * (hpg // 2) + pr
            ms = []
            for hh in (2 * blk, 2 * blk + 1):
                seg = acum_x128[:, hh * LANES:(hh + 1) * LANES] - acum_t[hh:hh + 1, :]
                decay = jnp.exp2(jnp.where(causal, seg, -jnp.inf))
                ms.append((cb * decay).astype(BF16))
            xb = xdt[:, blk * LANES:(blk + 1) * LANES]
            x_lo = jnp.where(low_half, xb, 0.0).astype(BF16)
            x_hi = jnp.where(low_half, 0.0, xb).astype(BF16)
            y_diag.append(_dot(jnp.concatenate(ms, axis=1),
                               jnp.concatenate([x_lo, x_hi], axis=0)))
        h_old = hstate[g]
        y_off.append(_dot(cg, h_old.astype(BF16)))
        new_state = _dot(bg.T.astype(BF16), xw[:, g * gw:(g + 1) * gw])
        hstate[g] = h_old * cd_x[:, g * gw:(g + 1) * gw] + new_state

    y = (jnp.concatenate(y_diag, axis=1) + jnp.concatenate(y_off, axis=1) * ea_x
         + dskip_ref[...] * xs)
    y = y * _silu(z)
    outs = []
    for g in range(SSD_GROUPS):
        outs.append(_rms(y[:, g * gw:(g + 1) * gw], ng_ref[:, g * gw:(g + 1) * gw]))
    y_ref[...] = jnp.concatenate(outs, axis=1).astype(y_ref.dtype)


def _ssd(xbc, dt, z, conv_w, conv_b, dt_bias, a_log, d_skip, norm_g, tm):
    bsz, s, _ = xbc.shape
    q = SSD_CHUNK

    def tok(width):
        return pl.BlockSpec((None, tm, width), lambda b, t: (b, t, 0))

    return pl.pallas_call(
        _ssd_kernel,
        grid=(bsz, s // tm),
        in_specs=[tok(SSD_XBC), tok(DT_PAD), tok(SSD_INNER),
                  _const_spec((SSD_CONV, SSD_XBC)), _const_spec((1, SSD_XBC)),
                  _const_spec((1, DT_PAD)), _const_spec((1, DT_PAD)),
                  _const_spec((1, SSD_INNER)), _const_spec((1, SSD_INNER))],
        out_specs=tok(SSD_INNER),
        out_shape=jax.ShapeDtypeStruct((bsz, s, SSD_INNER), BF16),
        scratch_shapes=[
            pltpu.VMEM((_CONV_HALO, SSD_XBC), F32),
            pltpu.VMEM((SSD_GROUPS, SSD_STATE, SSD_INNER // SSD_GROUPS), F32),
            pltpu.VMEM((LANES, SSD_INNER), BF16),
            pltpu.VMEM((LANES, SSD_HEADS * LANES), BF16),
            pltpu.VMEM((q, q), BF16),
        ],
        compiler_params=pltpu.CompilerParams(
            dimension_semantics=("arbitrary", "arbitrary"), vmem_limit_bytes=VMEM_LIMIT),
        name="ssd_scan",
    )(xbc, dt, z, conv_w, conv_b, dt_bias, a_log, d_skip, norm_g)


def _bias_kernel(tab_ref, o_ref, *, t):
    h = pl.program_id(0)
    kk = lax.broadcasted_iota(jnp.int32, (2 * t, t), 0)
    qq = lax.broadcasted_iota(jnp.int32, (2 * t, t), 1)
    rel = qq - kk + t
    n = jnp.maximum(rel, 0)
    max_exact = REL_BUCKETS // 2
    nf = jnp.maximum(n, 1).astype(F32)
    large = max_exact + (jnp.log(nf / max_exact) / math.log(REL_MAX_DIST / max_exact)
                         * (REL_BUCKETS - max_exact)).astype(jnp.int32)
    large = jnp.minimum(large, REL_BUCKETS - 1)
    bucket = jnp.where(n < max_exact, n, large)
    far = tab_ref[REL_BUCKETS - 1, h]
    acc = jnp.zeros((2 * t, t), F32)
    for b in range(REL_BUCKETS - 1):
        acc = jnp.where(bucket == b, (tab_ref[b, h] - far) * LOG2E, acc)
    o_ref[...] = jnp.where(rel >= 0, acc, -jnp.inf)


def _bias_tiles(rel_bias, t):
    return pl.pallas_call(
        functools.partial(_bias_kernel, t=t),
        grid=(ATTN_HEADS,),
        in_specs=[pl.BlockSpec(memory_space=pltpu.SMEM)],
        out_specs=pl.BlockSpec((None, 2 * t, t), lambda h: (h, 0, 0)),
        out_shape=jax.ShapeDtypeStruct((ATTN_HEADS, 2 * t, t), F32),
        compiler_params=pltpu.CompilerParams(
            dimension_semantics=("arbitrary",), vmem_limit_bytes=VMEM_LIMIT),
        name="rel_bias_tiles",
    )(rel_bias)


def _attn_kernel(lq1_ref, lk1_ref, lq2_ref, lk2_ref, subg_ref, q1_ref, q2_ref, k_ref, vt_ref,
                 band_ref, o_ref, m_sc, acc_sc, s_a, s_b, mx_a, mx_b, *, tq, tk, lam_init):
    ndiag = tq // tk
    assert ndiag % 2 == 0
    nq = q1_ref.shape[0] // tq
    q_refs = (q1_ref, q2_ref)
    lam = (jnp.exp(jnp.sum(lq1_ref[...] * lk1_ref[...], axis=-1, keepdims=True))
           - jnp.exp(jnp.sum(lq2_ref[...] * lk2_ref[...], axis=-1, keepdims=True))
           + lam_init)

    def add_bias(buf, key_off, c0=0):
        for mp in range(2):
            for r0 in range(0, tk, LANES):
                for cb in range(c0, tq, LANES):
                    rel0 = cb - r0 - key_off
                    blk = (slice(r0, r0 + LANES), slice(cb, cb + LANES))
                    if rel0 < 0:
                        buf[mp, blk[0], blk[1]] = jnp.full((LANES, LANES), -jnp.inf, F32)
                    elif rel0 == 0:
                        buf[mp, blk[0], blk[1]] += band_ref[LANES:2 * LANES, :]
                    elif rel0 == LANES:
                        buf[mp, blk[0], blk[1]] += band_ref[0:LANES, :]

    def qk(i, j, dst, c0=0, mx_dst=None):
        step((i, j, dst, c0, mx_dst), None)

    def softmax_pv(src, j, c0=0, mx_src=None):
        step(None, (src, j, c0, mx_src))

    def step(qk_args, sm_args):
        if qk_args is not None:
            qi, qj, dst, q_c0, mx_dst = qk_args
            kt = k_ref[pl.ds(pl.multiple_of(qj * tk, tk), tk), :]
        if sm_args is not None:
            src, sj, s_c0, mx_src = sm_args
        for mp in range(2):
            for cb in range(0, tq, MXU_COLS):
                cols = slice(cb, cb + MXU_COLS)
                if qk_args is not None and cb >= q_c0:
                    qt = q_refs[mp][pl.ds(pl.multiple_of(qi * tq + cb, MXU_COLS), MXU_COLS), :]
                    sc = _dot_nt(kt, qt)
                    dst[mp, :, cols] = sc
                    if mx_dst is not None:
                        mx_dst[mp, :, cols] = jnp.max(sc, axis=0, keepdims=True)
                if sm_args is not None and cb >= s_c0:
                    s = src[mp, :, cols]
                    m_prev = m_sc[mp, :, cols]
                    if mx_src is not None:
                        tile_max = mx_src[mp, :, cols]
                    else:
                        tile_max = jnp.max(s, axis=0, keepdims=True)
                    m_new = jnp.maximum(m_prev, tile_max)
                    alpha = jnp.exp2(m_prev - m_new)
                    p = jnp.exp2(s - m_new).astype(BF16)
                    acc_sc[mp, :, cols] = alpha * acc_sc[mp, :, cols] + _dot(vt_ref[sj], p)
                    m_sc[mp, :, cols] = m_new

    qk(0, 0, s_a, mx_dst=mx_a)

    def q_tile(i, carry):
        base = ndiag * i
        m_sc[...] = jnp.full(m_sc.shape, -jnp.inf, F32)
        acc_sc[...] = jnp.zeros(acc_sc.shape, F32)

        def far_pair(jj, c):
            j = 2 * jj
            step((i, j + 1, s_b, 0, mx_b), (s_a, j, 0, mx_a))
            step((i, j + 2, s_a, 0, mx_a), (s_b, j + 1, 0, mx_b))
            return c

        lax.fori_loop(0, base // 2 - 1, far_pair, 0)

        @pl.when(i > 0)
        def _far_prev():
            step((i, base - 1, s_b, 0, None), (s_a, base - 2, 0, mx_a))
            add_bias(s_b, -tk)
            step((i, base, s_a, 0, None), (s_b, base - 1, 0, None))

        for d in range(0, ndiag, 2):
            add_bias(s_a, d * tk, c0=d * tk)
            step((i, base + d + 1, s_b, (d + 1) * tk, None), (s_a, base + d, d * tk, None))
            add_bias(s_b, (d + 1) * tk, c0=(d + 1) * tk)
            if d + 2 < ndiag:
                nxt = (i, base + d + 2, s_a, (d + 2) * tk, None)
            else:
                nxt = (jnp.minimum(i + 1, nq - 1), 0, s_a, 0, mx_a)
            step(nxt, (s_b, base + d + 1, (d + 1) * tk, None))

        inv1 = 1.0 / acc_sc[0, ATTN_V_DIM:ATTN_V_DIM + 1, :]
        inv2 = lam / acc_sc[1, ATTN_V_DIM:ATTN_V_DIM + 1, :]
        o = (acc_sc[0, 0:ATTN_V_DIM, :] * inv1 - acc_sc[1, 0:ATTN_V_DIM, :] * inv2).T
        o = _rms(o, subg_ref[...]) * (1.0 - lam_init)
        o_ref[pl.ds(pl.multiple_of(i * tq, tq), tq), :] = o.astype(o_ref.dtype)
        return carry

    lax.fori_loop(0, nq, q_tile, 0)


def _attention(q1, q2, k, vt, band, lq1, lk1, lq2, lk2, subg, tq, tk, lam_init):
    bsz, s, _ = q1.shape
    small = lambda shape: pl.BlockSpec(shape, lambda b, h: (0, 0))
    head_cols = pl.BlockSpec((None, s, LANES), lambda b, h: (b, 0, h))
    return pl.pallas_call(
        functools.partial(_attn_kernel, tq=tq, tk=tk, lam_init=lam_init),
        grid=(bsz, ATTN_HEADS),
        in_specs=[
            small((1, ATTN_QK_DIM)), small((1, ATTN_QK_DIM)),
            small((1, ATTN_QK_DIM)), small((1, ATTN_QK_DIM)),
            small((1, ATTN_V_DIM)),
            head_cols, head_cols, head_cols,
            pl.BlockSpec((None, s // tk, None, V_AUG, tk), lambda b, h: (b, 0, h, 0, 0)),
            pl.BlockSpec((None, 2 * LANES, LANES), lambda b, h: (h, 0, 0)),
        ],
        out_specs=head_cols,
        out_shape=jax.ShapeDtypeStruct((bsz, s, ATTN_WIDTH), BF16),
        scratch_shapes=[
            pltpu.VMEM((2, 1, tq), F32),
            pltpu.VMEM((2, V_AUG, tq), F32),
            pltpu.VMEM((2, tk, tq), F32),
            pltpu.VMEM((2, tk, tq), F32),
            pltpu.VMEM((2, 1, tq), F32),
            pltpu.VMEM((2, 1, tq), F32),
        ],
        compiler_params=pltpu.CompilerParams(
            dimension_semantics=("arbitrary", "arbitrary"),
            vmem_limit_bytes=VMEM_LIMIT),
        name="diff_attention",
    )(lq1, lk1, lq2, lk2, subg, q1, q2, k, vt, band)


def _outproj_kernel(x_ref, ys_ref, ya_ref, gt_ref, post_ref, w_ref, o_ref, *, row_chains):
    k_ssd = ys_ref.shape[-1]
    rows = x_ref.shape[0] // row_chains
    for r in range(row_chains):
        sp = slice(r * rows, (r + 1) * rows)
        m = _dot(ys_ref[sp, :], w_ref[0:k_ssd, :]) + _dot(ya_ref[sp, :], w_ref[k_ssd:, :])
        o_ref[sp, :] = x_ref[sp, :] + gt_ref[...] * _rms(m, post_ref[...])


def _outproj(x, y_ssd, y_attn, mod4, post_g, w_out, tm, row_chains):
    bsz, s, d = x.shape

    def tok(width):
        return pl.BlockSpec((None, tm, width), lambda b, t: (b, t, 0))

    return pl.pallas_call(
        functools.partial(_outproj_kernel, row_chains=row_chains),
        grid=(bsz, s // tm),
        in_specs=[tok(d), tok(SSD_INNER), tok(ATTN_WIDTH), _mod_spec(5, d),
                  _const_spec((1, d)), _resident_spec((SSD_INNER + ATTN_WIDTH, d))],
        out_specs=tok(d),
        out_shape=jax.ShapeDtypeStruct((bsz, s, d), F32),
        compiler_params=pltpu.CompilerParams(
            dimension_semantics=("arbitrary", "arbitrary"), vmem_limit_bytes=VMEM_LIMIT),
        name="mixer_out_proj",
    )(x, y_ssd, y_attn, mod4, post_g, w_out)


def _tiles(s):
    return dict(
        tm_ffn=min(1024, s),
        ffn_chains=4,
        tm_proj=min(1024, s),
        proj_chains=2,
        tm_ssd=min(512, s),
        ff_chunk=2816,
        tq_attn=min(2048, s),
        tk_attn=min(512, s // 2),
        tn_mod=1152,
    )


def kernel(x, c, w_ada, b_ada, ffn1_pre_g, ffn1_post_g, ffn1_w_gate, ffn1_w_up, ffn1_w_down, mix_pre_g, mix_post_g, w_in, conv_w, conv_b, dt_bias, a_log, d_skip, ssd_norm_g, lambda_q1, lambda_k1, lambda_q2, lambda_k2, subln_g, w_out, ffn2_pre_g, ffn2_post_g, ffn2_w_gate, ffn2_w_up, ffn2_w_down, rel_bias):
    bsz, s, d = x.shape
    depth = w_ada.shape[0]
    cfg = _tiles(s)
    row = lambda v: v.reshape(1, -1)

    c_pad = jnp.zeros((8, d), F32).at[:bsz].set(c)
    band = _bias_tiles(rel_bias, LANES)

    for l in range(depth):
        mod = _modulation(c_pad, w_ada[l], row(b_ada[l]), cfg["tn_mod"])
        mod4 = mod[:bsz].reshape(bsz, N_MOD, 1, d)

        x = _ffn(x, mod4, 0, row(ffn1_pre_g[l]), row(ffn1_post_g[l]),
                 ffn1_w_gate[l].astype(BF16), ffn1_w_up[l].astype(BF16),
                 ffn1_w_down[l].astype(BF16), cfg["tm_ffn"], cfg["ff_chunk"],
                 cfg["ffn_chains"])

        w = w_in[l]
        dt0 = _SSD_COLS
        w_ssd = w[:, :dt0].astype(BF16)
        w_dt = jnp.pad(w[:, dt0:dt0 + SSD_HEADS], ((0, 0), (0, DT_PAD - SSD_HEADS))).astype(BF16)
        w_attn = w[:, dt0 + SSD_HEADS:].astype(BF16)
        z, xbc, dt, q1, q2, k, vt = _inproj(x, mod4, row(mix_pre_g[l]), w_ssd, w_dt, w_attn,
                                            cfg["tk_attn"])

        pad16 = lambda vec: jnp.zeros((1, DT_PAD), F32).at[0, :SSD_HEADS].set(vec)
        y_ssd = _ssd(xbc, dt, z, conv_w[l], row(conv_b[l]), pad16(dt_bias[l]), pad16(a_log[l]),
                     row(jnp.repeat(d_skip[l], SSD_HEAD_DIM)), row(ssd_norm_g[l]),
                     cfg["tm_ssd"])

        lam_init = 0.8 - 0.6 * math.exp(-0.3 * l)
        y_attn = _attention(q1, q2, k, vt, band, row(lambda_q1[l]), row(lambda_k1[l]),
                            row(lambda_q2[l]), row(lambda_k2[l]), row(subln_g[l]),
                            cfg["tq_attn"], cfg["tk_attn"], lam_init)

        x = _outproj(x, y_ssd, y_attn, mod4, row(mix_post_g[l]), w_out[l].astype(BF16),
                     cfg["tm_proj"], cfg["proj_chains"])

        x = _ffn(x, mod4, 6, row(ffn2_pre_g[l]), row(ffn2_post_g[l]),
                 ffn2_w_gate[l].astype(BF16), ffn2_w_up[l].astype(BF16),
                 ffn2_w_down[l].astype(BF16), cfg["tm_ffn"], cfg["ff_chunk"],
                 cfg["ffn_chains"])
    return x
```

```python
import functools
import math

import jax
import jax.numpy as jnp
from jax import lax
from jax.experimental import pallas as pl
from jax.experimental.pallas import tpu as pltpu

F32 = jnp.float32
BF16 = jnp.bfloat16

LANES = 128
EPS = 1e-6

SSD_HEADS = 16
SSD_HEAD_DIM = 64
SSD_INNER = SSD_HEADS * SSD_HEAD_DIM
SSD_GROUPS = 2
SSD_STATE = 128
SSD_CONV = 4
SSD_CHUNK = 128
SSD_XBC = SSD_INNER + 2 * SSD_GROUPS * SSD_STATE
ATTN_HEADS = 8
ATTN_QK_DIM = 64
ATTN_V_DIM = 128
ATTN_WIDTH = ATTN_HEADS * ATTN_V_DIM
REL_BUCKETS = 32
REL_MAX_DIST = 128
N_MOD = 9
DT_PAD = LANES
BF16_SUBLANES = 16
V_AUG = ATTN_V_DIM + BF16_SUBLANES
LOG2E = 1.4426950408889634

V7X_VMEM_BYTES = 64 * 1024 * 1024
VMEM_LIMIT = V7X_VMEM_BYTES // 8 * 7


def _dot(a, b):
    return jnp.dot(a, b, preferred_element_type=F32)


def _dot_nt(a, b):
    return lax.dot_general(a, b, (((1,), (1,)), ((), ())), preferred_element_type=F32)


def _rms(x, g):
    return x * lax.rsqrt(jnp.mean(x * x, axis=-1, keepdims=True) + EPS) * g


def _silu(x):
    half = 0.5 * x
    return half + half * jnp.tanh(half)


def _split_bf16(v):
    hi = v.astype(BF16)
    lo = (v - hi.astype(F32)).astype(BF16)
    return hi, lo


def _mod_kernel(c_ref, w_ref, b_ref, o_ref):
    cs = _silu(c_ref[...]).astype(BF16)
    o_ref[...] = _dot(cs, w_ref[...].astype(BF16)) + b_ref[...]


def _modulation(c_pad, w_ada, b_ada, tn):
    rows, d = c_pad.shape
    n = w_ada.shape[1]
    return pl.pallas_call(
        _mod_kernel,
        grid=(n // tn,),
        in_specs=[
            pl.BlockSpec((rows, d), lambda j: (0, 0)),
            pl.BlockSpec((d, tn), lambda j: (0, j)),
            pl.BlockSpec((1, tn), lambda j: (0, j)),
        ],
        out_specs=pl.BlockSpec((rows, tn), lambda j: (0, j)),
        out_shape=jax.ShapeDtypeStruct((rows, n), F32),
        compiler_params=pltpu.CompilerParams(
            dimension_semantics=("arbitrary",), vmem_limit_bytes=VMEM_LIMIT),
        name="adaln_mod",
    )(c_pad, w_ada, b_ada)


def _mod_spec(idx, d):
    return pl.BlockSpec((None, None, 1, d), lambda b, t: (b, idx, 0, 0))


def _const_spec(shape):
    return pl.BlockSpec(shape, lambda b, t: tuple(0 for _ in shape))


_W_STAGE = 256


def _load_weights_bf16(hbm_refs, bf_refs, stage_cols, stage_rows, sems):
    chunks = []
    for src, dst in zip(hbm_refs, bf_refs):
        rows, cols = src.shape
        if rows <= cols:
            for c0 in range(0, cols, _W_STAGE):
                cs = slice(c0, c0 + _W_STAGE)
                chunks.append((src.at[:, cs], stage_cols, dst.at[:, cs]))
        else:
            for r0 in range(0, rows, _W_STAGE):
                rs = slice(r0, r0 + _W_STAGE)
                chunks.append((src.at[rs, :], stage_rows, dst.at[rs, :]))

    def copy(n):
        src, stage, _ = chunks[n]
        return pltpu.make_async_copy(src, stage.at[n % 2], sems.at[n % 2])

    copy(0).start()
    for n, (_, stage, dst) in enumerate(chunks):
        if n + 1 < len(chunks):
            copy(n + 1).start()
        copy(n).wait()
        dst[...] = stage[n % 2].astype(BF16)


def _ffn_kernel(x_ref, sh_ref, sc_ref, gt_ref, pre_ref, post_ref, wg_hbm, wu_hbm, wd_hbm,
                o_ref, wg_ref, wu_ref, wd_ref, stage_cols, stage_rows, sems,
                *, ff_chunk, row_chains):
    @pl.when(jnp.logical_and(pl.program_id(0) == 0, pl.program_id(1) == 0))
    def _weights():
        _load_weights_bf16((wg_hbm, wu_hbm, wd_hbm), (wg_ref, wu_ref, wd_ref),
                           stage_cols, stage_rows, sems)

    tm = x_ref.shape[0]
    rows = tm // row_chains
    d_ff = wg_ref.shape[1]
    spans = [slice(r * rows, (r + 1) * rows) for r in range(row_chains)]
    hs = [(_rms(x_ref[sp, :], pre_ref[...]) * (1.0 + sc_ref[...]) + sh_ref[...]).astype(BF16)
          for sp in spans]
    ys = [None] * row_chains
    for c0 in range(0, d_ff, ff_chunk):
        for r in range(row_chains):
            g = _dot(hs[r], wg_ref[:, c0:c0 + ff_chunk])
            u = _dot(hs[r], wu_ref[:, c0:c0 + ff_chunk])
            a = (_silu(g) * u).astype(BF16)
            part = _dot(a, wd_ref[c0:c0 + ff_chunk, :])
            ys[r] = part if ys[r] is None else ys[r] + part
    for r, sp in enumerate(spans):
        o_ref[sp, :] = x_ref[sp, :] + (0.5 * gt_ref[...]) * _rms(ys[r], post_ref[...])


def _resident_spec(shape):
    return pl.BlockSpec(shape, lambda b, t: tuple(0 for _ in shape),
                        pipeline_mode=pl.Buffered(1))


def _ffn(x, mod4, mod_base, pre_g, post_g, wg, wu, wd, tm, ff_chunk, row_chains):
    bsz, s, d = x.shape
    d_ff = wg.shape[1]
    tok = pl.BlockSpec((None, tm, d), lambda b, t: (b, t, 0))
    return pl.pallas_call(
        functools.partial(_ffn_kernel, ff_chunk=ff_chunk, row_chains=row_chains),
        grid=(bsz, s // tm),
        in_specs=[
            tok,
            _mod_spec(mod_base, d), _mod_spec(mod_base + 1, d), _mod_spec(mod_base + 2, d),
            _const_spec((1, d)), _const_spec((1, d)),
            pl.BlockSpec(memory_space=pl.ANY), pl.BlockSpec(memory_space=pl.ANY),
            pl.BlockSpec(memory_space=pl.ANY),
        ],
        out_specs=tok,
        out_shape=jax.ShapeDtypeStruct((bsz, s, d), F32),
        scratch_shapes=[
            pltpu.VMEM((d, d_ff), BF16), pltpu.VMEM((d, d_ff), BF16), pltpu.VMEM((d_ff, d), BF16),
            pltpu.VMEM((2, d, _W_STAGE), F32), pltpu.VMEM((2, _W_STAGE, d), F32),
            pltpu.SemaphoreType.DMA((2,)),
        ],
        compiler_params=pltpu.CompilerParams(
            dimension_semantics=("arbitrary", "arbitrary"), vmem_limit_bytes=VMEM_LIMIT),
        name="ffn_half_step",
    )(x, mod4, mod4, mod4, pre_g, post_g, wg, wu, wd)


_SSD_COLS = SSD_INNER + SSD_XBC
MXU_COLS = 256


def _inproj_kernel(x_ref, sh_ref, sc_ref, pre_ref, w_ssd_ref, w_dt_ref, w_attn_ref,
                   z_ref, xbc_ref, dt_ref, q1_ref, q2_ref, k_ref, vt_ref):
    h = (_rms(x_ref[...], pre_ref[...]) * (1.0 + sc_ref[...]) + sh_ref[...]).astype(BF16)
    z_ref[...] = _dot(h, w_ssd_ref[:, 0:SSD_INNER])
    xbc_ref[...] = _dot(h, w_ssd_ref[:, SSD_INNER:_SSD_COLS])
    dt_ref[...] = _dot(h, w_dt_ref[...])
    q = (_dot(h, w_attn_ref[:, 0:ATTN_WIDTH]) * (ATTN_QK_DIM ** -0.5 * LOG2E)).astype(BF16)
    lane = lax.broadcasted_iota(jnp.int32, q.shape, 1)
    is_map1 = lane % (2 * ATTN_QK_DIM) < ATTN_QK_DIM
    zero = jnp.zeros_like(q)
    q1_ref[...] = jnp.where(is_map1, q, zero)
    q2_ref[...] = jnp.where(is_map1, zero, q)
    k_ref[...] = _dot(h, w_attn_ref[:, ATTN_WIDTH:2 * ATTN_WIDTH]).astype(BF16)
    ones = jnp.ones((V_AUG - ATTN_V_DIM, x_ref.shape[0]), BF16)
    heads_per_dot = MXU_COLS // ATTN_V_DIM
    for h0 in range(0, ATTN_HEADS, heads_per_dot):
        c0 = 2 * ATTN_WIDTH + h0 * ATTN_V_DIM
        v_grp = _dot(h, w_attn_ref[:, c0:c0 + MXU_COLS])
        for hh in range(heads_per_dot):
            v_h = v_grp[:, hh * ATTN_V_DIM:(hh + 1) * ATTN_V_DIM]
            vt_ref[h0 + hh, 0:ATTN_V_DIM, :] = v_h.T.astype(BF16)
            vt_ref[h0 + hh, ATTN_V_DIM:V_AUG, :] = ones


def _inproj(x, mod4, pre_g, w_ssd, w_dt, w_attn, tm):
    bsz, s, d = x.shape

    def tok(width):
        return pl.BlockSpec((None, tm, width), lambda b, t: (b, t, 0))

    def out(width, dtype):
        return jax.ShapeDtypeStruct((bsz, s, width), dtype)

    return pl.pallas_call(
        _inproj_kernel,
        grid=(bsz, s // tm),
        in_specs=[tok(d), _mod_spec(3, d), _mod_spec(4, d), _const_spec((1, d)),
                  _resident_spec((d, _SSD_COLS)), _resident_spec((d, DT_PAD)),
                  _resident_spec((d, 3 * ATTN_WIDTH))],
        out_specs=[tok(SSD_INNER), tok(SSD_XBC), tok(DT_PAD),
                   tok(ATTN_WIDTH), tok(ATTN_WIDTH), tok(ATTN_WIDTH),
                   pl.BlockSpec((None, None, ATTN_HEADS, V_AUG, tm),
                                lambda b, t: (b, t, 0, 0, 0))],
        out_shape=[out(SSD_INNER, F32), out(SSD_XBC, F32), out(DT_PAD, F32),
                   out(ATTN_WIDTH, BF16), out(ATTN_WIDTH, BF16), out(ATTN_WIDTH, BF16),
                   jax.ShapeDtypeStruct((bsz, s // tm, ATTN_HEADS, V_AUG, tm), BF16)],
        compiler_params=pltpu.CompilerParams(
            dimension_semantics=("arbitrary", "arbitrary"), vmem_limit_bytes=VMEM_LIMIT),
        name="mixer_in_proj",
    )(x, mod4, mod4, pre_g, w_ssd, w_dt, w_attn)


_CONV_HALO = 8


def _ssd_kernel(xbc_ref, dt_ref, z_ref, cw_ref, cb_ref, dtb_ref, alog_ref, dskip_ref, ng_ref,
                y_ref, halo_ref, hstate, e64, e128, tril):
    q = SSD_CHUNK

    @pl.when(pl.program_id(1) == 0)
    def _init():
        halo_ref[...] = jnp.zeros((_CONV_HALO, SSD_XBC), F32)
        hstate[...] = jnp.zeros(hstate.shape, F32)
        r = lax.broadcasted_iota(jnp.int32, (LANES, SSD_INNER), 0)
        c = lax.broadcasted_iota(jnp.int32, (LANES, SSD_INNER), 1)
        e64[...] = (r == c // SSD_HEAD_DIM).astype(BF16)
        r = lax.broadcasted_iota(jnp.int32, (LANES, SSD_HEADS * LANES), 0)
        c = lax.broadcasted_iota(jnp.int32, (LANES, SSD_HEADS * LANES), 1)
        e128[...] = (r == c // LANES).astype(BF16)
        r = lax.broadcasted_iota(jnp.int32, (q, q), 0)
        c = lax.broadcasted_iota(jnp.int32, (q, q), 1)
        tril[...] = (r >= c).astype(BF16)

    def chunk(ci, carry):
        rows = pl.ds(pl.multiple_of(ci * q, q), q)
        _ssd_chunk(xbc_ref[rows, :], dt_ref[rows, :], z_ref[rows, :], cw_ref, cb_ref, dtb_ref,
                   alog_ref, dskip_ref, ng_ref, y_ref.at[rows, :], halo_ref, hstate, e64, e128,
                   tril)
        return carry

    lax.fori_loop(0, xbc_ref.shape[0] // q, chunk, 0)


def _ssd_chunk(u, dt_raw, z, cw_ref, cb_ref, dtb_ref, alog_ref, dskip_ref, ng_ref,
               y_ref, halo_ref, hstate, e64, e128, tril):
    q = SSD_CHUNK
    gw = SSD_INNER // SSD_GROUPS
    hpg = SSD_HEADS // SSD_GROUPS

    halo = halo_ref[...]
    row = lax.broadcasted_iota(jnp.int32, (_CONV_HALO, SSD_XBC), 0)

    def shift_rows(cur, prev_tail, n):
        rolled = pltpu.roll(cur, n, 0)
        head = jnp.where(row < n, pltpu.roll(prev_tail, n, 0), rolled[0:_CONV_HALO, :])
        return jnp.concatenate([head, rolled[_CONV_HALO:, :]], axis=0)

    w0, w1, w2, w3 = (cw_ref[k:k + 1, :] for k in range(SSD_CONV))
    u1 = shift_rows(u, halo, 1)
    b_cur = w1 * u + w0 * u1
    b_tail = w1 * halo + w0 * pltpu.roll(halo, 1, 0)
    acc = (cb_ref[...] + w3 * u + w2 * u1) + shift_rows(b_cur, b_tail, 2)
    halo_ref[...] = u[q - _CONV_HALO:q, :]
    xc = _silu(acc)
    xs = xc[:, :SSD_INNER]
    bm = xc[:, SSD_INNER:SSD_INNER + SSD_GROUPS * SSD_STATE]
    cm = xc[:, SSD_INNER + SSD_GROUPS * SSD_STATE:]

    dt_in = dt_raw + dtb_ref[...]
    dtv = jnp.maximum(dt_in, 0.0) + jnp.log1p(jnp.exp(-jnp.abs(dt_in)))
    d_a = dtv * (-jnp.exp(alog_ref[...]) * LOG2E)
    da_hi, da_lo = _split_bf16(d_a)
    acum = _dot(tril[...], da_hi) + _dot(tril[...], da_lo)
    acum_t = acum.T

    def expand(v, e_ref):
        hi, lo = _split_bf16(v)
        return _dot(hi, e_ref[...]) + _dot(lo, e_ref[...])

    dt_x = expand(dtv, e64)
    acum_x = expand(acum, e64)
    acum_x128 = expand(acum, e128)
    alast_x = acum_x[q - 1:q, :]
    ea_x = jnp.exp2(acum_x)
    dte_x = jnp.exp2(alast_x - acum_x)
    cd_x = jnp.exp2(alast_x)

    xdt = xs * dt_x
    xw = (xdt * dte_x).astype(BF16)
    causal = (lax.broadcasted_iota(jnp.int32, (q, q), 0)
              >= lax.broadcasted_iota(jnp.int32, (q, q), 1))
    lane = lax.broadcasted_iota(jnp.int32, (q, LANES), 1)
    low_half = lane < SSD_HEAD_DIM

    y_diag = []
    y_off = []
    for g in range(SSD_GROUPS):
        bg = bm[:, g * SSD_STATE:(g + 1) * SSD_STATE]
        cg = cm[:, g * SSD_STATE:(g + 1) * SSD_STATE].astype(BF16)
        cb = _dot_nt(cg, bg.astype(BF16))
        for pr in range(hpg // 2):
            blk = g * (hpg // 2) + pr
            ms = []
            for hh in (2 * blk, 2 * blk + 1):
                seg = acum_x128[:, hh * LANES:(hh + 1) * LANES] - acum_t[hh:hh + 1, :]
                decay = jnp.exp2(jnp.where(causal, seg, -jnp.inf))
                ms.append((cb * decay).astype(BF16))
            xb = xdt[:, blk * LANES:(blk + 1) * LANES]
            x_lo = jnp.where(low_half, xb, 0.0).astype(BF16)
            x_hi = jnp.where(low_half, 0.0, xb).astype(BF16)
            y_diag.append(_dot(jnp.concatenate(ms, axis=1),
                               jnp.concatenate([x_lo, x_hi], axis=0)))
        h_old = hstate[g]
        y_off.append(_dot(cg, h_old.astype(BF16)))
        new_state = _dot(bg.T.astype(BF16), xw[:, g * gw:(g + 1) * gw])
        hstate[g] = h_old * cd_x[:, g * gw:(g + 1) * gw] + new_state

    y = (jnp.concatenate(y_diag, axis=1) + jnp.concatenate(y_off, axis=1) * ea_x
         + dskip_ref[...] * xs)
    y = y * _silu(z)
    outs = []
    for g in range(SSD_GROUPS):
        outs.append(_rms(y[:, g * gw:(g + 1) * gw], ng_ref[:, g * gw:(g + 1) * gw]))
    y_ref[...] = jnp.concatenate(outs, axis=1).astype(y_ref.dtype)


def _ssd(xbc, dt, z, conv_w, conv_b, dt_bias, a_log, d_skip, norm_g, tm):
    bsz, s, _ = xbc.shape
    q = SSD_CHUNK

    def tok(width):
        return pl.BlockSpec((None, tm, width), lambda b, t: (b, t, 0))

    return pl.pallas_call(
        _ssd_kernel,
        grid=(bsz, s // tm),
        in_specs=[tok(SSD_XBC), tok(DT_PAD), tok(SSD_INNER),
                  _const_spec((SSD_CONV, SSD_XBC)), _const_spec((1, SSD_XBC)),
                  _const_spec((1, DT_PAD)), _const_spec((1, DT_PAD)),
                  _const_spec((1, SSD_INNER)), _const_spec((1, SSD_INNER))],
        out_specs=tok(SSD_INNER),
        out_shape=jax.ShapeDtypeStruct((bsz, s, SSD_INNER), BF16),
        scratch_shapes=[
            pltpu.VMEM((_CONV_HALO, SSD_XBC), F32),
            pltpu.VMEM((SSD_GROUPS, SSD_STATE, SSD_INNER // SSD_GROUPS), F32),
            pltpu.VMEM((LANES, SSD_INNER), BF16),
            pltpu.VMEM((LANES, SSD_HEADS * LANES), BF16),
            pltpu.VMEM((q, q), BF16),
        ],
        compiler_params=pltpu.CompilerParams(
            dimension_semantics=("arbitrary", "arbitrary"), vmem_limit_bytes=VMEM_LIMIT),
        name="ssd_scan",
    )(xbc, dt, z, conv_w, conv_b, dt_bias, a_log, d_skip, norm_g)


def _bias_kernel(tab_ref, o_ref, *, t):
    h = pl.program_id(0)
    kk = lax.broadcasted_iota(jnp.int32, (2 * t, t), 0)
    qq = lax.broadcasted_iota(jnp.int32, (2 * t, t), 1)
    rel = qq - kk + t
    n = jnp.maximum(rel, 0)
    max_exact = REL_BUCKETS // 2
    nf = jnp.maximum(n, 1).astype(F32)
    large = max_exact + (jnp.log(nf / max_exact) / math.log(REL_MAX_DIST / max_exact)
                         * (REL_BUCKETS - max_exact)).astype(jnp.int32)
    large = jnp.minimum(large, REL_BUCKETS - 1)
    bucket = jnp.where(n < max_exact, n, large)
    far = tab_ref[REL_BUCKETS - 1, h]
    acc = jnp.zeros((2 * t, t), F32)
    for b in range(REL_BUCKETS - 1):
        acc = jnp.where(bucket == b, (tab_ref[b, h] - far) * LOG2E, acc)
    o_ref[...] = jnp.where(rel >= 0, acc, -jnp.inf)


def _bias_tiles(rel_bias, t):
    return pl.pallas_call(
        functools.partial(_bias_kernel, t=t),
        grid=(ATTN_HEADS,),
        in_specs=[pl.BlockSpec(memory_space=pltpu.SMEM)],
        out_specs=pl.BlockSpec((None, 2 * t, t), lambda h: (h, 0, 0)),
        out_shape=jax.ShapeDtypeStruct((ATTN_HEADS, 2 * t, t), F32),
        compiler_params=pltpu.CompilerParams(
            dimension_semantics=("arbitrary",), vmem_limit_bytes=VMEM_LIMIT),
        name="rel_bias_tiles",
    )(rel_bias)


def _attn_kernel(lq1_ref, lk1_ref, lq2_ref, lk2_ref, subg_ref, q1_ref, q2_ref, k_ref, vt_ref,
                 band_ref, o_ref, m_sc, acc_sc, s_a, s_b, mx_a, mx_b, *, tq, tk, lam_init):
    ndiag = tq // tk
    assert ndiag % 2 == 0
    nq = q1_ref.shape[0] // tq
    q_refs = (q1_ref, q2_ref)
    lam = (jnp.exp(jnp.sum(lq1_ref[...] * lk1_ref[...], axis=-1, keepdims=True))
           - jnp.exp(jnp.sum(lq2_ref[...] * lk2_ref[...], axis=-1, keepdims=True))
           + lam_init)

    def add_bias(buf, key_off, c0=0):
        for mp in range(2):
            for r0 in range(0, tk, LANES):
                for cb in range(c0, tq, LANES):
                    rel0 = cb - r0 - key_off
                    blk = (slice(r0, r0 + LANES), slice(cb, cb + LANES))
                    if rel0 < 0:
                        buf[mp, blk[0], blk[1]] = jnp.full((LANES, LANES), -jnp.inf, F32)
                    elif rel0 == 0:
                        buf[mp, blk[0], blk[1]] += band_ref[LANES:2 * LANES, :]
                    elif rel0 == LANES:
                        buf[mp, blk[0], blk[1]] += band_ref[0:LANES, :]

    def step(qk_args, sm_args):
        if qk_args is not None:
            qi, qj, dst, q_c0, mx_dst = qk_args
            kt = k_ref[pl.ds(pl.multiple_of(qj * tk, tk), tk), :]
        if sm_args is not None:
            src, sj, s_c0, mx_src = sm_args
        for mp in range(2):
            for cb in range(0, tq, MXU_COLS):
                cols = slice(cb, cb + MXU_COLS)
                if qk_args is not None and cb >= q_c0:
                    qt = q_refs[mp][pl.ds(pl.multiple_of(qi * tq + cb, MXU_COLS), MXU_COLS), :]
                    sc = _dot_nt(kt, qt)
                    dst[mp, :, cols] = sc
                    if mx_dst is not None:
                        mx_dst[mp, :, cols] = jnp.max(sc, axis=0, keepdims=True)
                if sm_args is not None and cb >= s_c0:
                    s = src[mp, :, cols]
                    m_prev = m_sc[mp, :, cols]
                    if mx_src is not None:
                        tile_max = mx_src[mp, :, cols]
                    else:
                        tile_max = jnp.max(s, axis=0, keepdims=True)
                    m_new = jnp.maximum(m_prev, tile_max)
                    alpha = jnp.exp2(m_prev - m_new)
                    p = jnp.exp2(s - m_new).astype(BF16)
                    acc_sc[mp, :, cols] = alpha * acc_sc[mp, :, cols] + _dot(vt_ref[sj], p)
                    m_sc[mp, :, cols] = m_new

    step((0, 0, s_a, 0, mx_a), None)

    def q_tile(i, carry):
        base = ndiag * i
        m_sc[...] = jnp.full(m_sc.shape, -jnp.inf, F32)
        acc_sc[...] = jnp.zeros(acc_sc.shape, F32)

        def far_pair(jj, c):
            j = 2 * jj
            step((i, j + 1, s_b, 0, mx_b), (s_a, j, 0, mx_a))
            step((i, j + 2, s_a, 0, mx_a), (s_b, j + 1, 0, mx_b))
            return c

        lax.fori_loop(0, base // 2 - 1, far_pair, 0)

        @pl.when(i > 0)
        def _far_prev():
            step((i, base - 1, s_b, 0, None), (s_a, base - 2, 0, mx_a))
            add_bias(s_b, -tk)
            step((i, base, s_a, 0, None), (s_b, base - 1, 0, None))

        for d in range(0, ndiag, 2):
            add_bias(s_a, d * tk, c0=d * tk)
            step((i, base + d + 1, s_b, (d + 1) * tk, None), (s_a, base + d, d * tk, None))
            add_bias(s_b, (d + 1) * tk, c0=(d + 1) * tk)
            if d + 2 < ndiag:
                nxt = (i, base + d + 2, s_a, (d + 2) * tk, None)
            else:
                nxt = (jnp.minimum(i + 1, nq - 1), 0, s_a, 0, mx_a)
            step(nxt, (s_b, base + d + 1, (d + 1) * tk, None))

        inv1 = 1.0 / acc_sc[0, ATTN_V_DIM:ATTN_V_DIM + 1, :]
        inv2 = lam / acc_sc[1, ATTN_V_DIM:ATTN_V_DIM + 1, :]
        o = (acc_sc[0, 0:ATTN_V_DIM, :] * inv1 - acc_sc[1, 0:ATTN_V_DIM, :] * inv2).T
        o = _rms(o, subg_ref[...]) * (1.0 - lam_init)
        o_ref[pl.ds(pl.multiple_of(i * tq, tq), tq), :] = o.astype(o_ref.dtype)
        return carry

    lax.fori_loop(0, nq, q_tile, 0)


def _attention(q1, q2, k, vt, band, lq1, lk1, lq2, lk2, subg, tq, tk, lam_init):
    bsz, s, _ = q1.shape
    small = lambda shape: pl.BlockSpec(shape, lambda b, h: (0, 0))
    head_cols = pl.BlockSpec((None, s, LANES), lambda b, h: (b, 0, h))
    return pl.pallas_call(
        functools.partial(_attn_kernel, tq=tq, tk=tk, lam_init=lam_init),
        grid=(bsz, ATTN_HEADS),
        in_specs=[
            small((1, ATTN_QK_DIM)), small((1, ATTN_QK_DIM)),
            small((1, ATTN_QK_DIM)), small((1, ATTN_QK_DIM)),
            small((1, ATTN_V_DIM)),
            head_cols, head_cols, head_cols,
            pl.BlockSpec((None, s // tk, None, V_AUG, tk), lambda b, h: (b, 0, h, 0, 0)),
            pl.BlockSpec((None, 2 * LANES, LANES), lambda b, h: (h, 0, 0)),
        ],
        out_specs=head_cols,
        out_shape=jax.ShapeDtypeStruct((bsz, s, ATTN_WIDTH), BF16),
        scratch_shapes=[
            pltpu.VMEM((2, 1, tq), F32),
            pltpu.VMEM((2, V_AUG, tq), F32),
            pltpu.VMEM((2, tk, tq), F32),
            pltpu.VMEM((2, tk, tq), F32),
            pltpu.VMEM((2, 1, tq), F32),
            pltpu.VMEM((2, 1, tq), F32),
        ],
        compiler_params=pltpu.CompilerParams(
            dimension_semantics=("arbitrary", "arbitrary"),
            vmem_limit_bytes=VMEM_LIMIT),
        name="diff_attention",
    )(lq1, lk1, lq2, lk2, subg, q1, q2, k, vt, band)


def _outproj_kernel(x_ref, ys_ref, ya_ref, gt_ref, post_ref, w_ref, o_ref, *, row_chains):
    k_ssd = ys_ref.shape[-1]
    rows = x_ref.shape[0] // row_chains
    for r in range(row_chains):
        sp = slice(r * rows, (r + 1) * rows)
        m = _dot(ys_ref[sp, :], w_ref[0:k_ssd, :]) + _dot(ya_ref[sp, :], w_ref[k_ssd:, :])
        o_ref[sp, :] = x_ref[sp, :] + gt_ref[...] * _rms(m, post_ref[...])


def _outproj(x, y_ssd, y_attn, mod4, post_g, w_out, tm, row_chains):
    bsz, s, d = x.shape

    def tok(width):
        return pl.BlockSpec((None, tm, width), lambda b, t: (b, t, 0))

    return pl.pallas_call(
        functools.partial(_outproj_kernel, row_chains=row_chains),
        grid=(bsz, s // tm),
        in_specs=[tok(d), tok(SSD_INNER), tok(ATTN_WIDTH), _mod_spec(5, d),
                  _const_spec((1, d)), _resident_spec((SSD_INNER + ATTN_WIDTH, d))],
        out_specs=tok(d),
        out_shape=jax.ShapeDtypeStruct((bsz, s, d), F32),
        compiler_params=pltpu.CompilerParams(
            dimension_semantics=("arbitrary", "arbitrary"), vmem_limit_bytes=VMEM_LIMIT),
        name="mixer_out_proj",
    )(x, y_ssd, y_attn, mod4, post_g, w_out)


def _tiles(s):
    return dict(
        tm_ffn=min(1024, s),
        ffn_chains=4,
        tm_proj=min(1024, s),
        proj_chains=2,
        tm_ssd=min(512, s),
        ff_chunk=2816,
        tq_attn=min(2048, s),
        tk_attn=min(512, s // 2),
        tn_mod=1152,
    )


def kernel(x, c, w_ada, b_ada, ffn1_pre_g, ffn1_post_g, ffn1_w_gate, ffn1_w_up, ffn1_w_down, mix_pre_g, mix_post_g, w_in, conv_w, conv_b, dt_bias, a_log, d_skip, ssd_norm_g, lambda_q1, lambda_k1, lambda_q2, lambda_k2, subln_g, w_out, ffn2_pre_g, ffn2_post_g, ffn2_w_gate, ffn2_w_up, ffn2_w_down, rel_bias):
    bsz, s, d = x.shape
    depth = w_ada.shape[0]
    cfg = _tiles(s)
    row = lambda v: v.reshape(1, -1)

    c_pad = jnp.zeros((8, d), F32).at[:bsz].set(c)
    band = _bias_tiles(rel_bias, LANES)

    for l in range(depth):
        mod = _modulation(c_pad, w_ada[l], row(b_ada[l]), cfg["tn_mod"])
        mod4 = mod[:bsz].reshape(bsz, N_MOD, 1, d)

        x = _ffn(x, mod4, 0, row(ffn1_pre_g[l]), row(ffn1_post_g[l]),
                 ffn1_w_gate[l], ffn1_w_up[l], ffn1_w_down[l], cfg["tm_ffn"], cfg["ff_chunk"],
                 cfg["ffn_chains"])

        w = w_in[l]
        dt0 = _SSD_COLS
        w_ssd = w[:, :dt0].astype(BF16)
        w_dt = jnp.pad(w[:, dt0:dt0 + SSD_HEADS], ((0, 0), (0, DT_PAD - SSD_HEADS))).astype(BF16)
        w_attn = w[:, dt0 + SSD_HEADS:].astype(BF16)
        z, xbc, dt, q1, q2, k, vt = _inproj(x, mod4, row(mix_pre_g[l]), w_ssd, w_dt, w_attn,
                                            cfg["tk_attn"])

        pad16 = lambda vec: jnp.zeros((1, DT_PAD), F32).at[0, :SSD_HEADS].set(vec)
        y_ssd = _ssd(xbc, dt, z, conv_w[l], row(conv_b[l]), pad16(dt_bias[l]), pad16(a_log[l]),
                     row(jnp.repeat(d_skip[l], SSD_HEAD_DIM)), row(ssd_norm_g[l]),
                     cfg["tm_ssd"])

        lam_init = 0.8 - 0.6 * math.exp(-0.3 * l)
        y_attn = _attention(q1, q2, k, vt, band, row(lambda_q1[l]), row(lambda_k1[l]),
                            row(lambda_q2[l]), row(lambda_k2[l]), row(subln_g[l]),
                            cfg["tq_attn"], cfg["tk_attn"], lam_init)

        x = _outproj(x, y_ssd, y_attn, mod4, row(mix_post_g[l]), w_out[l].astype(BF16),
                     cfg["tm_proj"], cfg["proj_chains"])

        x = _ffn(x, mod4, 6, row(ffn2_pre_g[l]), row(ffn2_post_g[l]),
                 ffn2_w_gate[l], ffn2_w_up[l], ffn2_w_down[l], cfg["tm_ffn"], cfg["ff_chunk"],
                 cfg["ffn_chains"])
    return x
```

```python
import functools
import math

import jax
import jax.numpy as jnp
from jax import lax
from jax.experimental import pallas as pl
from jax.experimental.pallas import tpu as pltpu

F32 = jnp.float32
BF16 = jnp.bfloat16

LANES = 128
EPS = 1e-6

SSD_HEADS = 16
SSD_HEAD_DIM = 64
SSD_INNER = SSD_HEADS * SSD_HEAD_DIM
SSD_GROUPS = 2
SSD_STATE = 128
SSD_CONV = 4
SSD_CHUNK = 128
SSD_XBC = SSD_INNER + 2 * SSD_GROUPS * SSD_STATE
ATTN_HEADS = 8
ATTN_QK_DIM = 64
ATTN_V_DIM = 128
ATTN_WIDTH = ATTN_HEADS * ATTN_V_DIM
REL_BUCKETS = 32
REL_MAX_DIST = 128
N_MOD = 9
DT_PAD = LANES
BF16_SUBLANES = 16
V_AUG = ATTN_V_DIM + BF16_SUBLANES
LOG2E = 1.4426950408889634

V7X_VMEM_BYTES = 64 * 1024 * 1024
VMEM_LIMIT = V7X_VMEM_BYTES // 8 * 7


def _dot(a, b):
    return jnp.dot(a, b, preferred_element_type=F32)


def _dot_nt(a, b):
    return lax.dot_general(a, b, (((1,), (1,)), ((), ())), preferred_element_type=F32)


def _rms(x, g):
    return x * lax.rsqrt(jnp.mean(x * x, axis=-1, keepdims=True) + EPS) * g


def _silu(x):
    half = 0.5 * x
    return half + half * jnp.tanh(half)


def _split_bf16(v):
    hi = v.astype(BF16)
    lo = (v - hi.astype(F32)).astype(BF16)
    return hi, lo


def _mod_kernel(c_ref, w_ref, b_ref, o_ref):
    cs = _silu(c_ref[...]).astype(BF16)
    o_ref[...] = _dot(cs, w_ref[...].astype(BF16)) + b_ref[...]


def _modulation(c_pad, w_ada, b_ada, tn):
    rows, d = c_pad.shape
    n = w_ada.shape[1]
    return pl.pallas_call(
        _mod_kernel,
        grid=(n // tn,),
        in_specs=[
            pl.BlockSpec((rows, d), lambda j: (0, 0)),
            pl.BlockSpec((d, tn), lambda j: (0, j)),
            pl.BlockSpec((1, tn), lambda j: (0, j)),
        ],
        out_specs=pl.BlockSpec((rows, tn), lambda j: (0, j)),
        out_shape=jax.ShapeDtypeStruct((rows, n), F32),
        compiler_params=pltpu.CompilerParams(
            dimension_semantics=("arbitrary",), vmem_limit_bytes=VMEM_LIMIT),
        name="adaln_mod",
    )(c_pad, w_ada, b_ada)


def _mod_spec(idx, d):
    return pl.BlockSpec((None, None, 1, d), lambda b, t: (b, idx, 0, 0))


def _const_spec(shape):
    return pl.BlockSpec(shape, lambda b, t: tuple(0 for _ in shape))


_W_SLOTS = 4
_W_ROWS_WIDE = 128
_W_ROWS_TALL = 256


def _load_weights_bf16(hbm_refs, bf_refs, stage_wide, stage_tall, sems):
    chunks = []
    for src, dst in zip(hbm_refs, bf_refs):
        rows, cols = src.shape
        stage, step = (stage_wide, _W_ROWS_WIDE) if rows <= cols else (stage_tall, _W_ROWS_TALL)
        for r0 in range(0, rows, step):
            chunks.append((src.at[r0:r0 + step, :], stage, dst.at[r0:r0 + step, :]))

    def copy(n):
        src, stage, _ = chunks[n]
        return pltpu.make_async_copy(src, stage.at[n % _W_SLOTS], sems.at[n % _W_SLOTS])

    for n in range(min(_W_SLOTS, len(chunks))):
        copy(n).start()
    for n, (_, stage, dst) in enumerate(chunks):
        copy(n).wait()
        dst[...] = stage[n % _W_SLOTS].astype(BF16)
        if n + _W_SLOTS < len(chunks):
            copy(n + _W_SLOTS).start()


def _ffn_kernel(x_ref, sh_ref, sc_ref, gt_ref, pre_ref, post_ref, wg_hbm, wu_hbm, wd_hbm,
                o_ref, wg_ref, wu_ref, wd_ref, stage_wide, stage_tall, sems,
                *, ff_chunk, row_chains):
    @pl.when(jnp.logical_and(pl.program_id(0) == 0, pl.program_id(1) == 0))
    def _weights():
        _load_weights_bf16((wg_hbm, wu_hbm, wd_hbm), (wg_ref, wu_ref, wd_ref),
                           stage_wide, stage_tall, sems)

    tm = x_ref.shape[0]
    rows = tm // row_chains
    d_ff = wg_ref.shape[1]
    spans = [slice(r * rows, (r + 1) * rows) for r in range(row_chains)]
    hs = [(_rms(x_ref[sp, :], pre_ref[...]) * (1.0 + sc_ref[...]) + sh_ref[...]).astype(BF16)
          for sp in spans]
    ys = [None] * row_chains
    for c0 in range(0, d_ff, ff_chunk):
        for r in range(row_chains):
            g = _dot(hs[r], wg_ref[:, c0:c0 + ff_chunk])
            u = _dot(hs[r], wu_ref[:, c0:c0 + ff_chunk])
            a = (_silu(g) * u).astype(BF16)
            part = _dot(a, wd_ref[c0:c0 + ff_chunk, :])
            ys[r] = part if ys[r] is None else ys[r] + part
    for r, sp in enumerate(spans):
        o_ref[sp, :] = x_ref[sp, :] + (0.5 * gt_ref[...]) * _rms(ys[r], post_ref[...])


def _resident_spec(shape):
    return pl.BlockSpec(shape, lambda b, t: tuple(0 for _ in shape),
                        pipeline_mode=pl.Buffered(1))


def _ffn(x, mod4, mod_base, pre_g, post_g, wg, wu, wd, tm, ff_chunk, row_chains):
    bsz, s, d = x.shape
    d_ff = wg.shape[1]
    tok = pl.BlockSpec((None, tm, d), lambda b, t: (b, t, 0))
    return pl.pallas_call(
        functools.partial(_ffn_kernel, ff_chunk=ff_chunk, row_chains=row_chains),
        grid=(bsz, s // tm),
        in_specs=[
            tok,
            _mod_spec(mod_base, d), _mod_spec(mod_base + 1, d), _mod_spec(mod_base + 2, d),
            _const_spec((1, d)), _const_spec((1, d)),
            pl.BlockSpec(memory_space=pl.ANY), pl.BlockSpec(memory_space=pl.ANY),
            pl.BlockSpec(memory_space=pl.ANY),
        ],
        out_specs=tok,
        out_shape=jax.ShapeDtypeStruct((bsz, s, d), F32),
        scratch_shapes=[
            pltpu.VMEM((d, d_ff), BF16), pltpu.VMEM((d, d_ff), BF16), pltpu.VMEM((d_ff, d), BF16),
            pltpu.VMEM((_W_SLOTS, _W_ROWS_WIDE, d_ff), F32),
            pltpu.VMEM((_W_SLOTS, _W_ROWS_TALL, d), F32),
            pltpu.SemaphoreType.DMA((_W_SLOTS,)),
        ],
        compiler_params=pltpu.CompilerParams(
            dimension_semantics=("arbitrary", "arbitrary"), vmem_limit_bytes=VMEM_LIMIT),
        name="ffn_half_step",
    )(x, mod4, mod4, mod4, pre_g, post_g, wg, wu, wd)


_SSD_COLS = SSD_INNER + SSD_XBC
MXU_COLS = 256


def _inproj_kernel(x_ref, sh_ref, sc_ref, pre_ref, w_ssd_ref, w_dt_ref, w_attn_ref,
                   z_ref, xbc_ref, dt_ref, q1_ref, q2_ref, k_ref, vt_ref):
    h = (_rms(x_ref[...], pre_ref[...]) * (1.0 + sc_ref[...]) + sh_ref[...]).astype(BF16)
    z_ref[...] = _dot(h, w_ssd_ref[:, 0:SSD_INNER])
    xbc_ref[...] = _dot(h, w_ssd_ref[:, SSD_INNER:_SSD_COLS])
    dt_ref[...] = _dot(h, w_dt_ref[...])
    q = (_dot(h, w_attn_ref[:, 0:ATTN_WIDTH]) * (ATTN_QK_DIM ** -0.5 * LOG2E)).astype(BF16)
    lane = lax.broadcasted_iota(jnp.int32, q.shape, 1)
    is_map1 = lane % (2 * ATTN_QK_DIM) < ATTN_QK_DIM
    zero = jnp.zeros_like(q)
    q1_ref[...] = jnp.where(is_map1, q, zero)
    q2_ref[...] = jnp.where(is_map1, zero, q)
    k_ref[...] = _dot(h, w_attn_ref[:, ATTN_WIDTH:2 * ATTN_WIDTH]).astype(BF16)
    ones = jnp.ones((V_AUG - ATTN_V_DIM, x_ref.shape[0]), BF16)
    heads_per_dot = MXU_COLS // ATTN_V_DIM
    for h0 in range(0, ATTN_HEADS, heads_per_dot):
        c0 = 2 * ATTN_WIDTH + h0 * ATTN_V_DIM
        v_grp = _dot(h, w_attn_ref[:, c0:c0 + MXU_COLS])
        for hh in range(heads_per_dot):
            v_h = v_grp[:, hh * ATTN_V_DIM:(hh + 1) * ATTN_V_DIM]
            vt_ref[h0 + hh, 0:ATTN_V_DIM, :] = v_h.T.astype(BF16)
            vt_ref[h0 + hh, ATTN_V_DIM:V_AUG, :] = ones


def _inproj(x, mod4, pre_g, w_ssd, w_dt, w_attn, tm):
    bsz, s, d = x.shape

    def tok(width):
        return pl.BlockSpec((None, tm, width), lambda b, t: (b, t, 0))

    def out(width, dtype):
        return jax.ShapeDtypeStruct((bsz, s, width), dtype)

    return pl.pallas_call(
        _inproj_kernel,
        grid=(bsz, s // tm),
        in_specs=[tok(d), _mod_spec(3, d), _mod_spec(4, d), _const_spec((1, d)),
                  _resident_spec((d, _SSD_COLS)), _resident_spec((d, DT_PAD)),
                  _resident_spec((d, 3 * ATTN_WIDTH))],
        out_specs=[tok(SSD_INNER), tok(SSD_XBC), tok(DT_PAD),
                   tok(ATTN_WIDTH), tok(ATTN_WIDTH), tok(ATTN_WIDTH),
                   pl.BlockSpec((None, None, ATTN_HEADS, V_AUG, tm),
                                lambda b, t: (b, t, 0, 0, 0))],
        out_shape=[out(SSD_INNER, F32), out(SSD_XBC, F32), out(DT_PAD, F32),
                   out(ATTN_WIDTH, BF16), out(ATTN_WIDTH, BF16), out(ATTN_WIDTH, BF16),
                   jax.ShapeDtypeStruct((bsz, s // tm, ATTN_HEADS, V_AUG, tm), BF16)],
        compiler_params=pltpu.CompilerParams(
            dimension_semantics=("arbitrary", "arbitrary"), vmem_limit_bytes=VMEM_LIMIT),
        name="mixer_in_proj",
    )(x, mod4, mod4, pre_g, w_ssd, w_dt, w_attn)


_CONV_HALO = 8


def _ssd_kernel(xbc_ref, dt_ref, z_ref, cw_ref, cb_ref, dtb_ref, alog_ref, dskip_ref, ng_ref,
                y_ref, halo_ref, hstate, e64, e128, tril):
    q = SSD_CHUNK

    @pl.when(pl.program_id(1) == 0)
    def _init():
        halo_ref[...] = jnp.zeros((_CONV_HALO, SSD_XBC), F32)
        hstate[...] = jnp.zeros(hstate.shape, F32)
        r = lax.broadcasted_iota(jnp.int32, (LANES, SSD_INNER), 0)
        c = lax.broadcasted_iota(jnp.int32, (LANES, SSD_INNER), 1)
        e64[...] = (r == c // SSD_HEAD_DIM).astype(BF16)
        r = lax.broadcasted_iota(jnp.int32, (LANES, SSD_HEADS * LANES), 0)
        c = lax.broadcasted_iota(jnp.int32, (LANES, SSD_HEADS * LANES), 1)
        e128[...] = (r == c // LANES).astype(BF16)
        r = lax.broadcasted_iota(jnp.int32, (q, q), 0)
        c = lax.broadcasted_iota(jnp.int32, (q, q), 1)
        tril[...] = (r >= c).astype(BF16)

    def chunk(ci, carry):
        rows = pl.ds(pl.multiple_of(ci * q, q), q)
        _ssd_chunk(xbc_ref[rows, :], dt_ref[rows, :], z_ref[rows, :], cw_ref, cb_ref, dtb_ref,
                   alog_ref, dskip_ref, ng_ref, y_ref.at[rows, :], halo_ref, hstate, e64, e128,
                   tril)
        return carry

    lax.fori_loop(0, xbc_ref.shape[0] // q, chunk, 0)


def _ssd_chunk(u, dt_raw, z, cw_ref, cb_ref, dtb_ref, alog_ref, dskip_ref, ng_ref,
               y_ref, halo_ref, hstate, e64, e128, tril):
    q = SSD_CHUNK
    gw = SSD_INNER // SSD_GROUPS
    hpg = SSD_HEADS // SSD_GROUPS

    halo = halo_ref[...]
    row = lax.broadcasted_iota(jnp.int32, (_CONV_HALO, SSD_XBC), 0)

    def shift_rows(cur, prev_tail, n):
        rolled = pltpu.roll(cur, n, 0)
        head = jnp.where(row < n, pltpu.roll(prev_tail, n, 0), rolled[0:_CONV_HALO, :])
        return jnp.concatenate([head, rolled[_CONV_HALO:, :]], axis=0)

    w0, w1, w2, w3 = (cw_ref[k:k + 1, :] for k in range(SSD_CONV))
    u1 = shift_rows(u, halo, 1)
    b_cur = w1 * u + w0 * u1
    b_tail = w1 * halo + w0 * pltpu.roll(halo, 1, 0)
    acc = (cb_ref[...] + w3 * u + w2 * u1) + shift_rows(b_cur, b_tail, 2)
    halo_ref[...] = u[q - _CONV_HALO:q, :]
    xc = _silu(acc)
    xs = xc[:, :SSD_INNER]
    bm = xc[:, SSD_INNER:SSD_INNER + SSD_GROUPS * SSD_STATE]
    cm = xc[:, SSD_INNER + SSD_GROUPS * SSD_STATE:]

    dt_in = dt_raw + dtb_ref[...]
    dtv = jnp.maximum(dt_in, 0.0) + jnp.log1p(jnp.exp(-jnp.abs(dt_in)))
    d_a = dtv * (-jnp.exp(alog_ref[...]) * LOG2E)
    da_hi, da_lo = _split_bf16(d_a)
    acum = _dot(tril[...], da_hi) + _dot(tril[...], da_lo)
    acum_t = acum.T

    def expand(v, e_ref):
        hi, lo = _split_bf16(v)
        return _dot(hi, e_ref[...]) + _dot(lo, e_ref[...])

    dt_x = expand(dtv, e64)
    acum_x = expand(acum, e64)
    acum_x128 = expand(acum, e128)
    alast_x = acum_x[q - 1:q, :]
    ea_x = jnp.exp2(acum_x)
    dte_x = jnp.exp2(alast_x - acum_x)
    cd_x = jnp.exp2(alast_x)

    xdt = xs * dt_x
    xw = (xdt * dte_x).astype(BF16)
    causal = (lax.broadcasted_iota(jnp.int32, (q, q), 0)
              >= lax.broadcasted_iota(jnp.int32, (q, q), 1))
    lane = lax.broadcasted_iota(jnp.int32, (q, LANES), 1)
    low_half = lane < SSD_HEAD_DIM

    y_diag = []
    y_off = []
    for g in range(SSD_GROUPS):
        bg = bm[:, g * SSD_STATE:(g + 1) * SSD_STATE]
        cg = cm[:, g * SSD_STATE:(g + 1) * SSD_STATE].astype(BF16)
        cb = _dot_nt(cg, bg.astype(BF16))
        for pr in range(hpg // 2):
            blk = g * (hpg // 2) + pr
            ms = []
            for hh in (2 * blk, 2 * blk + 1):
                seg = acum_x128[:, hh * LANES:(hh + 1) * LANES] - acum_t[hh:hh + 1, :]
                decay = jnp.exp2(jnp.where(causal, seg, -jnp.inf))
                ms.append((cb * decay).astype(BF16))
            xb = xdt[:, blk * LANES:(blk + 1) * LANES]
            x_lo = jnp.where(low_half, xb, 0.0).astype(BF16)
            x_hi = jnp.where(low_half, 0.0, xb).astype(BF16)
            y_diag.append(_dot(jnp.concatenate(ms, axis=1),
                               jnp.concatenate([x_lo, x_hi], axis=0)))
        h_old = hstate[g]
        y_off.append(_dot(cg, h_old.astype(BF16)))
        new_state = _dot(bg.T.astype(BF16), xw[:, g * gw:(g + 1) * gw])
        hstate[g] = h_old * cd_x[:, g * gw:(g + 1) * gw] + new_state

    y = (jnp.concatenate(y_diag, axis=1) + jnp.concatenate(y_off, axis=1) * ea_x
         + dskip_ref[...] * xs)
    y = y * _silu(z)
    outs = []
    for g in range(SSD_GROUPS):
        outs.append(_rms(y[:, g * gw:(g + 1) * gw], ng_ref[:, g * gw:(g + 1) * gw]))
    y_ref[...] = jnp.concatenate(outs, axis=1).astype(y_ref.dtype)


def _ssd(xbc, dt, z, conv_w, conv_b, dt_bias, a_log, d_skip, norm_g, tm):
    bsz, s, _ = xbc.shape
    q = SSD_CHUNK

    def tok(width):
        return pl.BlockSpec((None, tm, width), lambda b, t: (b, t, 0))

    return pl.pallas_call(
        _ssd_kernel,
        grid=(bsz, s // tm),
        in_specs=[tok(SSD_XBC), tok(DT_PAD), tok(SSD_INNER),
                  _const_spec((SSD_CONV, SSD_XBC)), _const_spec((1, SSD_XBC)),
                  _const_spec((1, DT_PAD)), _const_spec((1, DT_PAD)),
                  _const_spec((1, SSD_INNER)), _const_spec((1, SSD_INNER))],
        out_specs=tok(SSD_INNER),
        out_shape=jax.ShapeDtypeStruct((bsz, s, SSD_INNER), BF16),
        scratch_shapes=[
            pltpu.VMEM((_CONV_HALO, SSD_XBC), F32),
            pltpu.VMEM((SSD_GROUPS, SSD_STATE, SSD_INNER // SSD_GROUPS), F32),
            pltpu.VMEM((LANES, SSD_INNER), BF16),
            pltpu.VMEM((LANES, SSD_HEADS * LANES), BF16),
            pltpu.VMEM((q, q), BF16),
        ],
        compiler_params=pltpu.CompilerParams(
            dimension_semantics=("arbitrary", "arbitrary"), vmem_limit_bytes=VMEM_LIMIT),
        name="ssd_scan",
    )(xbc, dt, z, conv_w, conv_b, dt_bias, a_log, d_skip, norm_g)


def _bias_kernel(tab_ref, o_ref, *, t):
    h = pl.program_id(0)
    kk = lax.broadcasted_iota(jnp.int32, (2 * t, t), 0)
    qq = lax.broadcasted_iota(jnp.int32, (2 * t, t), 1)
    rel = qq - kk + t
    n = jnp.maximum(rel, 0)
    max_exact = REL_BUCKETS // 2
    nf = jnp.maximum(n, 1).astype(F32)
    large = max_exact + (jnp.log(nf / max_exact) / math.log(REL_MAX_DIST / max_exact)
                         * (REL_BUCKETS - max_exact)).astype(jnp.int32)
    large = jnp.minimum(large, REL_BUCKETS - 1)
    bucket = jnp.where(n < max_exact, n, large)
    far = tab_ref[REL_BUCKETS - 1, h]
    acc = jnp.zeros((2 * t, t), F32)
    for b in range(REL_BUCKETS - 1):
        acc = jnp.where(bucket == b, (tab_ref[b, h] - far) * LOG2E, acc)
    o_ref[...] = jnp.where(rel >= 0, acc, -jnp.inf)


def _bias_tiles(rel_bias, t):
    return pl.pallas_call(
        functools.partial(_bias_kernel, t=t),
        grid=(ATTN_HEADS,),
        in_specs=[pl.BlockSpec(memory_space=pltpu.SMEM)],
        out_specs=pl.BlockSpec((None, 2 * t, t), lambda h: (h, 0, 0)),
        out_shape=jax.ShapeDtypeStruct((ATTN_HEADS, 2 * t, t), F32),
        compiler_params=pltpu.CompilerParams(
            dimension_semantics=("arbitrary",), vmem_limit_bytes=VMEM_LIMIT),
        name="rel_bias_tiles",
    )(rel_bias)


def _attn_kernel(lq1_ref, lk1_ref, lq2_ref, lk2_ref, subg_ref, q1_ref, q2_ref, k_ref, vt_ref,
                 band_ref, o_ref, m_sc, acc_sc, s_a, s_b, mx_a, mx_b, *, tq, tk, lam_init):
    ndiag = tq // tk
    assert ndiag % 2 == 0
    nq = q1_ref.shape[0] // tq
    q_refs = (q1_ref, q2_ref)
    lam = (jnp.exp(jnp.sum(lq1_ref[...] * lk1_ref[...], axis=-1, keepdims=True))
           - jnp.exp(jnp.sum(lq2_ref[...] * lk2_ref[...], axis=-1, keepdims=True))
           + lam_init)

    def add_bias(buf, key_off, c0=0):
        for mp in range(2):
            for r0 in range(0, tk, LANES):
                for cb in range(c0, tq, LANES):
                    rel0 = cb - r0 - key_off
                    blk = (slice(r0, r0 + LANES), slice(cb, cb + LANES))
                    if rel0 < 0:
                        buf[mp, blk[0], blk[1]] = jnp.full((LANES, LANES), -jnp.inf, F32)
                    elif rel0 == 0:
                        buf[mp, blk[0], blk[1]] += band_ref[LANES:2 * LANES, :]
                    elif rel0 == LANES:
                        buf[mp, blk[0], blk[1]] += band_ref[0:LANES, :]

    def step(qk_args, sm_args):
        if qk_args is not None:
            qi, qj, dst, q_c0, mx_dst = qk_args
            kt = k_ref[pl.ds(pl.multiple_of(qj * tk, tk), tk), :]
        if sm_args is not None:
            src, sj, s_c0, mx_src = sm_args
        for mp in range(2):
            for cb in range(0, tq, MXU_COLS):
                cols = slice(cb, cb + MXU_COLS)
                if qk_args is not None and cb >= q_c0:
                    qt = q_refs[mp][pl.ds(pl.multiple_of(qi * tq + cb, MXU_COLS), MXU_COLS), :]
                    sc = _dot_nt(kt, qt)
                    dst[mp, :, cols] = sc
                    if mx_dst is not None:
                        mx_dst[mp, :, cols] = jnp.max(sc, axis=0, keepdims=True)
                if sm_args is not None and cb >= s_c0:
                    s = src[mp, :, cols]
                    m_prev = m_sc[mp, :, cols]
                    if mx_src is not None:
                        tile_max = mx_src[mp, :, cols]
                    else:
                        tile_max = jnp.max(s, axis=0, keepdims=True)
                    m_new = jnp.maximum(m_prev, tile_max)
                    alpha = jnp.exp2(m_prev - m_new)
                    p = jnp.exp2(s - m_new).astype(BF16)
                    acc_sc[mp, :, cols] = alpha * acc_sc[mp, :, cols] + _dot(vt_ref[sj], p)
                    m_sc[mp, :, cols] = m_new

    step((0, 0, s_a, 0, mx_a), None)

    def q_tile(i, carry):
        base = ndiag * i
        m_sc[...] = jnp.full(m_sc.shape, -jnp.inf, F32)
        acc_sc[...] = jnp.zeros(acc_sc.shape, F32)

        def far_pair(jj, c):
            j = 2 * jj
            step((i, j + 1, s_b, 0, mx_b), (s_a, j, 0, mx_a))
            step((i, j + 2, s_a, 0, mx_a), (s_b, j + 1, 0, mx_b))
            return c

        lax.fori_loop(0, base // 2 - 1, far_pair, 0)

        @pl.when(i > 0)
        def _far_prev():
            step((i, base - 1, s_b, 0, None), (s_a, base - 2, 0, mx_a))
            add_bias(s_b, -tk)
            step((i, base, s_a, 0, None), (s_b, base - 1, 0, None))

        for d in range(0, ndiag, 2):
            add_bias(s_a, d * tk, c0=d * tk)
            step((i, base + d + 1, s_b, (d + 1) * tk, None), (s_a, base + d, d * tk, None))
            add_bias(s_b, (d + 1) * tk, c0=(d + 1) * tk)
            if d + 2 < ndiag:
                nxt = (i, base + d + 2, s_a, (d + 2) * tk, None)
            else:
                nxt = (jnp.minimum(i + 1, nq - 1), 0, s_a, 0, mx_a)
            step(nxt, (s_b, base + d + 1, (d + 1) * tk, None))

        inv1 = 1.0 / acc_sc[0, ATTN_V_DIM:ATTN_V_DIM + 1, :]
        inv2 = lam / acc_sc[1, ATTN_V_DIM:ATTN_V_DIM + 1, :]
        o = (acc_sc[0, 0:ATTN_V_DIM, :] * inv1 - acc_sc[1, 0:ATTN_V_DIM, :] * inv2).T
        o = _rms(o, subg_ref[...]) * (1.0 - lam_init)
        o_ref[pl.ds(pl.multiple_of(i * tq, tq), tq), :] = o.astype(o_ref.dtype)
        return carry

    lax.fori_loop(0, nq, q_tile, 0)


def _attention(q1, q2, k, vt, band, lq1, lk1, lq2, lk2, subg, tq, tk, lam_init):
    bsz, s, _ = q1.shape
    small = lambda shape: pl.BlockSpec(shape, lambda b, h: (0, 0))
    head_cols = pl.BlockSpec((None, s, LANES), lambda b, h: (b, 0, h))
    return pl.pallas_call(
        functools.partial(_attn_kernel, tq=tq, tk=tk, lam_init=lam_init),
        grid=(bsz, ATTN_HEADS),
        in_specs=[
            small((1, ATTN_QK_DIM)), small((1, ATTN_QK_DIM)),
            small((1, ATTN_QK_DIM)), small((1, ATTN_QK_DIM)),
            small((1, ATTN_V_DIM)),
            head_cols, head_cols, head_cols,
            pl.BlockSpec((None, s // tk, None, V_AUG, tk), lambda b, h: (b, 0, h, 0, 0)),
            pl.BlockSpec((None, 2 * LANES, LANES), lambda b, h: (h, 0, 0)),
        ],
        out_specs=head_cols,
        out_shape=jax.ShapeDtypeStruct((bsz, s, ATTN_WIDTH), BF16),
        scratch_shapes=[
            pltpu.VMEM((2, 1, tq), F32),
            pltpu.VMEM((2, V_AUG, tq), F32),
            pltpu.VMEM((2, tk, tq), F32),
            pltpu.VMEM((2, tk, tq), F32),
            pltpu.VMEM((2, 1, tq), F32),
            pltpu.VMEM((2, 1, tq), F32),
        ],
        compiler_params=pltpu.CompilerParams(
            dimension_semantics=("arbitrary", "arbitrary"),
            vmem_limit_bytes=VMEM_LIMIT),
        name="diff_attention",
    )(lq1, lk1, lq2, lk2, subg, q1, q2, k, vt, band)


def _outproj_kernel(x_ref, ys_ref, ya_ref, gt_ref, post_ref, w_hbm, o_ref, w_ref, stage_tall, sems,
                    *, row_chains):
    @pl.when(jnp.logical_and(pl.program_id(0) == 0, pl.program_id(1) == 0))
    def _weights():
        _load_weights_bf16((w_hbm,), (w_ref,), None, stage_tall, sems)

    k_ssd = ys_ref.shape[-1]
    rows = x_ref.shape[0] // row_chains
    for r in range(row_chains):
        sp = slice(r * rows, (r + 1) * rows)
        m = _dot(ys_ref[sp, :], w_ref[0:k_ssd, :]) + _dot(ya_ref[sp, :], w_ref[k_ssd:, :])
        o_ref[sp, :] = x_ref[sp, :] + gt_ref[...] * _rms(m, post_ref[...])


def _outproj(x, y_ssd, y_attn, mod4, post_g, w_out, tm, row_chains):
    bsz, s, d = x.shape

    def tok(width):
        return pl.BlockSpec((None, tm, width), lambda b, t: (b, t, 0))

    return pl.pallas_call(
        functools.partial(_outproj_kernel, row_chains=row_chains),
        grid=(bsz, s // tm),
        in_specs=[tok(d), tok(SSD_INNER), tok(ATTN_WIDTH), _mod_spec(5, d),
                  _const_spec((1, d)), pl.BlockSpec(memory_space=pl.ANY)],
        out_specs=tok(d),
        out_shape=jax.ShapeDtypeStruct((bsz, s, d), F32),
        scratch_shapes=[
            pltpu.VMEM((SSD_INNER + ATTN_WIDTH, d), BF16),
            pltpu.VMEM((_W_SLOTS, _W_ROWS_TALL, d), F32),
            pltpu.SemaphoreType.DMA((_W_SLOTS,)),
        ],
        compiler_params=pltpu.CompilerParams(
            dimension_semantics=("arbitrary", "arbitrary"), vmem_limit_bytes=VMEM_LIMIT),
        name="mixer_out_proj",
    )(x, y_ssd, y_attn, mod4, post_g, w_out)


def _tiles(s):
    return dict(
        tm_ffn=min(1024, s),
        ffn_chains=4,
        tm_proj=min(1024, s),
        proj_chains=2,
        tm_ssd=min(512, s),
        ff_chunk=2816,
        tq_attn=min(2048, s),
        tk_attn=min(512, s // 2),
        tn_mod=1152,
    )


def kernel(x, c, w_ada, b_ada, ffn1_pre_g, ffn1_post_g, ffn1_w_gate, ffn1_w_up, ffn1_w_down, mix_pre_g, mix_post_g, w_in, conv_w, conv_b, dt_bias, a_log, d_skip, ssd_norm_g, lambda_q1, lambda_k1, lambda_q2, lambda_k2, subln_g, w_out, ffn2_pre_g, ffn2_post_g, ffn2_w_gate, ffn2_w_up, ffn2_w_down, rel_bias):
    bsz, s, d = x.shape
    depth = w_ada.shape[0]
    cfg = _tiles(s)
    row = lambda v: v.reshape(1, -1)

    c_pad = jnp.zeros((8, d), F32).at[:bsz].set(c)
    band = _bias_tiles(rel_bias, LANES)

    for l in range(depth):
        mod = _modulation(c_pad, w_ada[l], row(b_ada[l]), cfg["tn_mod"])
        mod4 = mod[:bsz].reshape(bsz, N_MOD, 1, d)

        x = _ffn(x, mod4, 0, row(ffn1_pre_g[l]), row(ffn1_post_g[l]),
                 ffn1_w_gate[l], ffn1_w_up[l], ffn1_w_down[l], cfg["tm_ffn"], cfg["ff_chunk"],
                 cfg["ffn_chains"])

        w = w_in[l]
        dt0 = _SSD_COLS
        w_ssd = w[:, :dt0].astype(BF16)
        w_dt = jnp.pad(w[:, dt0:dt0 + SSD_HEADS], ((0, 0), (0, DT_PAD - SSD_HEADS))).astype(BF16)
        w_attn = w[:, dt0 + SSD_HEADS:].astype(BF16)
        z, xbc, dt, q1, q2, k, vt = _inproj(x, mod4, row(mix_pre_g[l]), w_ssd, w_dt, w_attn,
                                            cfg["tk_attn"])

        pad16 = lambda vec: jnp.zeros((1, DT_PAD), F32).at[0, :SSD_HEADS].set(vec)
        y_ssd = _ssd(xbc, dt, z, conv_w[l], row(conv_b[l]), pad16(dt_bias[l]), pad16(a_log[l]),
                     row(jnp.repeat(d_skip[l], SSD_HEAD_DIM)), row(ssd_norm_g[l]),
                     cfg["tm_ssd"])

        lam_init = 0.8 - 0.6 * math.exp(-0.3 * l)
        y_attn = _attention(q1, q2, k, vt, band, row(lambda_q1[l]), row(lambda_k1[l]),
                            row(lambda_q2[l]), row(lambda_k2[l]), row(subln_g[l]),
                            cfg["tq_attn"], cfg["tk_attn"], lam_init)

        x = _outproj(x, y_ssd, y_attn, mod4, row(mix_post_g[l]), w_out[l],
                     cfg["tm_proj"], cfg["proj_chains"])

        x = _ffn(x, mod4, 6, row(ffn2_pre_g[l]), row(ffn2_post_g[l]),
                 ffn2_w_gate[l], ffn2_w_up[l], ffn2_w_down[l], cfg["tm_ffn"], cfg["ff_chunk"],
                 cfg["ffn_chains"])
    return x
```

```python
import functools
import math

import jax
import jax.numpy as jnp
from jax import lax
from jax.experimental import pallas as pl
from jax.experimental.pallas import tpu as pltpu

F32 = jnp.float32
BF16 = jnp.bfloat16

LANES = 128
EPS = 1e-6

SSD_HEADS = 16
SSD_HEAD_DIM = 64
SSD_INNER = SSD_HEADS * SSD_HEAD_DIM
SSD_GROUPS = 2
SSD_STATE = 128
SSD_CONV = 4
SSD_CHUNK = 128
SSD_XBC = SSD_INNER + 2 * SSD_GROUPS * SSD_STATE
ATTN_HEADS = 8
ATTN_QK_DIM = 64
ATTN_V_DIM = 128
ATTN_WIDTH = ATTN_HEADS * ATTN_V_DIM
REL_BUCKETS = 32
REL_MAX_DIST = 128
N_MOD = 9
DT_PAD = LANES
BF16_SUBLANES = 16
V_AUG = ATTN_V_DIM + BF16_SUBLANES
LOG2E = 1.4426950408889634

V7X_VMEM_BYTES = 64 * 1024 * 1024
VMEM_LIMIT = V7X_VMEM_BYTES // 8 * 7


def _dot(a, b):
    return jnp.dot(a, b, preferred_element_type=F32)


def _dot_nt(a, b):
    return lax.dot_general(a, b, (((1,), (1,)), ((), ())), preferred_element_type=F32)


def _rms(x, g):
    return x * lax.rsqrt(jnp.mean(x * x, axis=-1, keepdims=True) + EPS) * g


def _silu(x):
    half = 0.5 * x
    return half + half * jnp.tanh(half)


def _split_bf16(v):
    hi = v.astype(BF16)
    lo = (v - hi.astype(F32)).astype(BF16)
    return hi, lo


def _mod_kernel(c_ref, w_ref, b_ref, o_ref):
    cs = _silu(c_ref[...]).astype(BF16)
    o_ref[...] = _dot(cs, w_ref[...].astype(BF16)) + b_ref[...]


def _modulation(c_pad, w_ada, b_ada, tn):
    rows, d = c_pad.shape
    n = w_ada.shape[1]
    return pl.pallas_call(
        _mod_kernel,
        grid=(n // tn,),
        in_specs=[
            pl.BlockSpec((rows, d), lambda j: (0, 0)),
            pl.BlockSpec((d, tn), lambda j: (0, j)),
            pl.BlockSpec((1, tn), lambda j: (0, j)),
        ],
        out_specs=pl.BlockSpec((rows, tn), lambda j: (0, j)),
        out_shape=jax.ShapeDtypeStruct((rows, n), F32),
        compiler_params=pltpu.CompilerParams(
            dimension_semantics=("arbitrary",), vmem_limit_bytes=VMEM_LIMIT),
        name="adaln_mod",
    )(c_pad, w_ada, b_ada)


def _mod_spec(idx, d):
    return pl.BlockSpec((None, None, 1, d), lambda b, t: (b, idx, 0, 0))


def _const_spec(shape):
    return pl.BlockSpec(shape, lambda b, t: tuple(0 for _ in shape))


_W_SLOTS = 4
_W_ROWS_WIDE = 128
_W_ROWS_TALL = 256


def _load_weights_bf16(hbm_refs, bf_refs, stage_wide, stage_tall, sems):
    chunks = []
    for src, dst in zip(hbm_refs, bf_refs):
        rows, cols = src.shape
        stage, step = (stage_wide, _W_ROWS_WIDE) if rows <= cols else (stage_tall, _W_ROWS_TALL)
        for r0 in range(0, rows, step):
            chunks.append((src.at[r0:r0 + step, :], stage, dst.at[r0:r0 + step, :]))

    def copy(n):
        src, stage, _ = chunks[n]
        return pltpu.make_async_copy(src, stage.at[n % _W_SLOTS], sems.at[n % _W_SLOTS])

    for n in range(min(_W_SLOTS, len(chunks))):
        copy(n).start()
    for n, (_, stage, dst) in enumerate(chunks):
        copy(n).wait()
        dst[...] = stage[n % _W_SLOTS].astype(BF16)
        if n + _W_SLOTS < len(chunks):
            copy(n + _W_SLOTS).start()


def _ffn_kernel(x_ref, sh_ref, sc_ref, gt_ref, pre_ref, post_ref, wg_hbm, wu_hbm, wd_hbm,
                o_ref, wg_ref, wu_ref, wd_ref, stage_wide, stage_tall, sems,
                *, ff_chunk, row_chains):
    @pl.when(jnp.logical_and(pl.program_id(0) == 0, pl.program_id(1) == 0))
    def _weights():
        _load_weights_bf16((wg_hbm, wu_hbm, wd_hbm), (wg_ref, wu_ref, wd_ref),
                           stage_wide, stage_tall, sems)

    tm = x_ref.shape[0]
    rows = tm // row_chains
    d_ff = wg_ref.shape[1]
    spans = [slice(r * rows, (r + 1) * rows) for r in range(row_chains)]
    hs = [(_rms(x_ref[sp, :], pre_ref[...]) * (1.0 + sc_ref[...]) + sh_ref[...]).astype(BF16)
          for sp in spans]
    ys = [None] * row_chains
    for c0 in range(0, d_ff, ff_chunk):
        for r in range(row_chains):
            g = _dot(hs[r], wg_ref[:, c0:c0 + ff_chunk])
            u = _dot(hs[r], wu_ref[:, c0:c0 + ff_chunk])
            a = (_silu(g) * u).astype(BF16)
            part = _dot(a, wd_ref[c0:c0 + ff_chunk, :])
            ys[r] = part if ys[r] is None else ys[r] + part
    for r, sp in enumerate(spans):
        o_ref[sp, :] = x_ref[sp, :] + (0.5 * gt_ref[...]) * _rms(ys[r], post_ref[...])


def _resident_spec(shape):
    return pl.BlockSpec(shape, lambda b, t: tuple(0 for _ in shape),
                        pipeline_mode=pl.Buffered(1))


def _ffn(x, mod4, mod_base, pre_g, post_g, wg, wu, wd, tm, ff_chunk, row_chains):
    bsz, s, d = x.shape
    d_ff = wg.shape[1]
    tok = pl.BlockSpec((None, tm, d), lambda b, t: (b, t, 0))
    return pl.pallas_call(
        functools.partial(_ffn_kernel, ff_chunk=ff_chunk, row_chains=row_chains),
        grid=(bsz, s // tm),
        in_specs=[
            tok,
            _mod_spec(mod_base, d), _mod_spec(mod_base + 1, d), _mod_spec(mod_base + 2, d),
            _const_spec((1, d)), _const_spec((1, d)),
            pl.BlockSpec(memory_space=pl.ANY), pl.BlockSpec(memory_space=pl.ANY),
            pl.BlockSpec(memory_space=pl.ANY),
        ],
        out_specs=tok,
        out_shape=jax.ShapeDtypeStruct((bsz, s, d), F32),
        scratch_shapes=[
            pltpu.VMEM((d, d_ff), BF16), pltpu.VMEM((d, d_ff), BF16), pltpu.VMEM((d_ff, d), BF16),
            pltpu.VMEM((_W_SLOTS, _W_ROWS_WIDE, d_ff), F32),
            pltpu.VMEM((_W_SLOTS, _W_ROWS_TALL, d), F32),
            pltpu.SemaphoreType.DMA((_W_SLOTS,)),
        ],
        compiler_params=pltpu.CompilerParams(
            dimension_semantics=("arbitrary", "arbitrary"), vmem_limit_bytes=VMEM_LIMIT),
        name="ffn_half_step",
    )(x, mod4, mod4, mod4, pre_g, post_g, wg, wu, wd)


_SSD_COLS = SSD_INNER + SSD_XBC
MXU_COLS = 256


def _inproj_kernel(x_ref, sh_ref, sc_ref, pre_ref, w_ssd_ref, w_dt_ref, w_attn_ref,
                   z_ref, xbc_ref, dt_ref, q1_ref, q2_ref, k_ref, vt_ref):
    h = (_rms(x_ref[...], pre_ref[...]) * (1.0 + sc_ref[...]) + sh_ref[...]).astype(BF16)
    z_ref[...] = _dot(h, w_ssd_ref[:, 0:SSD_INNER])
    xbc_ref[...] = _dot(h, w_ssd_ref[:, SSD_INNER:_SSD_COLS])
    dt_ref[...] = _dot(h, w_dt_ref[...])
    q = (_dot(h, w_attn_ref[:, 0:ATTN_WIDTH]) * (ATTN_QK_DIM ** -0.5 * LOG2E)).astype(BF16)
    lane = lax.broadcasted_iota(jnp.int32, q.shape, 1)
    is_map1 = lane % (2 * ATTN_QK_DIM) < ATTN_QK_DIM
    zero = jnp.zeros_like(q)
    q1_ref[...] = jnp.where(is_map1, q, zero)
    q2_ref[...] = jnp.where(is_map1, zero, q)
    k_ref[...] = _dot(h, w_attn_ref[:, ATTN_WIDTH:2 * ATTN_WIDTH]).astype(BF16)
    ones = jnp.ones((V_AUG - ATTN_V_DIM, x_ref.shape[0]), BF16)
    heads_per_dot = MXU_COLS // ATTN_V_DIM
    for h0 in range(0, ATTN_HEADS, heads_per_dot):
        c0 = 2 * ATTN_WIDTH + h0 * ATTN_V_DIM
        v_grp = _dot(h, w_attn_ref[:, c0:c0 + MXU_COLS])
        for hh in range(heads_per_dot):
            v_h = v_grp[:, hh * ATTN_V_DIM:(hh + 1) * ATTN_V_DIM]
            vt_ref[h0 + hh, 0:ATTN_V_DIM, :] = v_h.T.astype(BF16)
            vt_ref[h0 + hh, ATTN_V_DIM:V_AUG, :] = ones


def _inproj(x, mod4, pre_g, w_ssd, w_dt, w_attn, tm):
    bsz, s, d = x.shape

    def tok(width):
        return pl.BlockSpec((None, tm, width), lambda b, t: (b, t, 0))

    def out(width, dtype):
        return jax.ShapeDtypeStruct((bsz, s, width), dtype)

    return pl.pallas_call(
        _inproj_kernel,
        grid=(bsz, s // tm),
        in_specs=[tok(d), _mod_spec(3, d), _mod_spec(4, d), _const_spec((1, d)),
                  _resident_spec((d, _SSD_COLS)), _resident_spec((d, DT_PAD)),
                  _resident_spec((d, 3 * ATTN_WIDTH))],
        out_specs=[tok(SSD_INNER), tok(SSD_XBC), tok(DT_PAD),
                   tok(ATTN_WIDTH), tok(ATTN_WIDTH), tok(ATTN_WIDTH),
                   pl.BlockSpec((None, None, ATTN_HEADS, V_AUG, tm),
                                lambda b, t: (b, t, 0, 0, 0))],
        out_shape=[out(SSD_INNER, F32), out(SSD_XBC, F32), out(DT_PAD, F32),
                   out(ATTN_WIDTH, BF16), out(ATTN_WIDTH, BF16), out(ATTN_WIDTH, BF16),
                   jax.ShapeDtypeStruct((bsz, s // tm, ATTN_HEADS, V_AUG, tm), BF16)],
        compiler_params=pltpu.CompilerParams(
            dimension_semantics=("arbitrary", "arbitrary"), vmem_limit_bytes=VMEM_LIMIT),
        name="mixer_in_proj",
    )(x, mod4, mod4, pre_g, w_ssd, w_dt, w_attn)


_CONV_HALO = 8


def _ssd_kernel(xbc_ref, dt_ref, z_ref, cw_ref, cb_ref, dtb_ref, alog_ref, dskip_ref, ng_ref,
                y_ref, halo_ref, hstate, e64, e128, tril):
    q = SSD_CHUNK

    @pl.when(pl.program_id(1) == 0)
    def _init():
        halo_ref[...] = jnp.zeros((_CONV_HALO, SSD_XBC), F32)
        hstate[...] = jnp.zeros(hstate.shape, F32)
        r = lax.broadcasted_iota(jnp.int32, (LANES, SSD_INNER), 0)
        c = lax.broadcasted_iota(jnp.int32, (LANES, SSD_INNER), 1)
        e64[...] = (r == c // SSD_HEAD_DIM).astype(BF16)
        r = lax.broadcasted_iota(jnp.int32, (LANES, SSD_HEADS * LANES), 0)
        c = lax.broadcasted_iota(jnp.int32, (LANES, SSD_HEADS * LANES), 1)
        e128[...] = (r == c // LANES).astype(BF16)
        r = lax.broadcasted_iota(jnp.int32, (q, q), 0)
        c = lax.broadcasted_iota(jnp.int32, (q, q), 1)
        tril[...] = (r >= c).astype(BF16)

    def chunk(ci, carry):
        rows = pl.ds(pl.multiple_of(ci * q, q), q)
        _ssd_chunk(xbc_ref[rows, :], dt_ref[rows, :], z_ref[rows, :], cw_ref, cb_ref, dtb_ref,
                   alog_ref, dskip_ref, ng_ref, y_ref.at[rows, :], halo_ref, hstate, e64, e128,
                   tril)
        return carry

    lax.fori_loop(0, xbc_ref.shape[0] // q, chunk, 0)


def _ssd_chunk(u, dt_raw, z, cw_ref, cb_ref, dtb_ref, alog_ref, dskip_ref, ng_ref,
               y_ref, halo_ref, hstate, e64, e128, tril):
    q = SSD_CHUNK
    gw = SSD_INNER // SSD_GROUPS
    hpg = SSD_HEADS // SSD_GROUPS

    halo = halo_ref[...]
    row = lax.broadcasted_iota(jnp.int32, (_CONV_HALO, SSD_XBC), 0)

    def shift_rows(cur, prev_tail, n):
        rolled = pltpu.roll(cur, n, 0)
        head = jnp.where(row < n, pltpu.roll(prev_tail, n, 0), rolled[0:_CONV_HALO, :])
        return jnp.concatenate([head, rolled[_CONV_HALO:, :]], axis=0)

    w0, w1, w2, w3 = (cw_ref[k:k + 1, :] for k in range(SSD_CONV))
    u1 = shift_rows(u, halo, 1)
    b_cur = w1 * u + w0 * u1
    b_tail = w1 * halo + w0 * pltpu.roll(halo, 1, 0)
    acc = (cb_ref[...] + w3 * u + w2 * u1) + shift_rows(b_cur, b_tail, 2)
    halo_ref[...] = u[q - _CONV_HALO:q, :]
    xc = _silu(acc)
    xs = xc[:, :SSD_INNER]
    bm = xc[:, SSD_INNER:SSD_INNER + SSD_GROUPS * SSD_STATE]
    cm = xc[:, SSD_INNER + SSD_GROUPS * SSD_STATE:]

    dt_in = dt_raw + dtb_ref[...]
    dtv = jnp.maximum(dt_in, 0.0) + jnp.log1p(jnp.exp(-jnp.abs(dt_in)))
    d_a = dtv * (-jnp.exp(alog_ref[...]) * LOG2E)
    da_hi, da_lo = _split_bf16(d_a)
    acum = _dot(tril[...], da_hi) + _dot(tril[...], da_lo)
    acum_t = acum.T

    def expand(v, e_ref):
        hi, lo = _split_bf16(v)
        return _dot(hi, e_ref[...]) + _dot(lo, e_ref[...])

    dt_x = expand(dtv, e64)
    acum_x = expand(acum, e64)
    acum_x128 = expand(acum, e128)
    alast_x = acum_x[q - 1:q, :]
    ea_x = jnp.exp2(acum_x)
    dte_x = jnp.exp2(alast_x - acum_x)
    cd_x = jnp.exp2(alast_x)

    xdt = xs * dt_x
    xw = (xdt * dte_x).astype(BF16)
    causal = (lax.broadcasted_iota(jnp.int32, (q, q), 0)
              >= lax.broadcasted_iota(jnp.int32, (q, q), 1))
    lane = lax.broadcasted_iota(jnp.int32, (q, LANES), 1)
    low_half = lane < SSD_HEAD_DIM

    y_diag = []
    y_off = []
    for g in range(SSD_GROUPS):
        bg = bm[:, g * SSD_STATE:(g + 1) * SSD_STATE]
        cg = cm[:, g * SSD_STATE:(g + 1) * SSD_STATE].astype(BF16)
        cb = _dot_nt(cg, bg.astype(BF16))
        for pr in range(hpg // 2):
            blk = g * (hpg // 2) + pr
            ms = []
            for hh in (2 * blk, 2 * blk + 1):
                seg = acum_x128[:, hh * LANES:(hh + 1) * LANES] - acum_t[hh:hh + 1, :]
                decay = jnp.exp2(jnp.where(causal, seg, -jnp.inf))
                ms.append((cb * decay).astype(BF16))
            xb = xdt[:, blk * LANES:(blk + 1) * LANES]
            x_lo = jnp.where(low_half, xb, 0.0).astype(BF16)
            x_hi = jnp.where(low_half, 0.0, xb).astype(BF16)
            y_diag.append(_dot(jnp.concatenate(ms, axis=1),
                               jnp.concatenate([x_lo, x_hi], axis=0)))
        h_old = hstate[g]
        y_off.append(_dot(cg, h_old.astype(BF16)))
        new_state = _dot(bg.T.astype(BF16), xw[:, g * gw:(g + 1) * gw])
        hstate[g] = h_old * cd_x[:, g * gw:(g + 1) * gw] + new_state

    y = (jnp.concatenate(y_diag, axis=1) + jnp.concatenate(y_off, axis=1) * ea_x
         + dskip_ref[...] * xs)
    y = y * _silu(z)
    outs = []
    for g in range(SSD_GROUPS):
        outs.append(_rms(y[:, g * gw:(g + 1) * gw], ng_ref[:, g * gw:(g + 1) * gw]))
    y_ref[...] = jnp.concatenate(outs, axis=1).astype(y_ref.dtype)


def _ssd(xbc, dt, z, conv_w, conv_b, dt_bias, a_log, d_skip, norm_g, tm):
    bsz, s, _ = xbc.shape
    q = SSD_CHUNK

    def tok(width):
        return pl.BlockSpec((None, tm, width), lambda b, t: (b, t, 0))

    return pl.pallas_call(
        _ssd_kernel,
        grid=(bsz, s // tm),
        in_specs=[tok(SSD_XBC), tok(DT_PAD), tok(SSD_INNER),
                  _const_spec((SSD_CONV, SSD_XBC)), _const_spec((1, SSD_XBC)),
                  _const_spec((1, DT_PAD)), _const_spec((1, DT_PAD)),
                  _const_spec((1, SSD_INNER)), _const_spec((1, SSD_INNER))],
        out_specs=tok(SSD_INNER),
        out_shape=jax.ShapeDtypeStruct((bsz, s, SSD_INNER), BF16),
        scratch_shapes=[
            pltpu.VMEM((_CONV_HALO, SSD_XBC), F32),
            pltpu.VMEM((SSD_GROUPS, SSD_STATE, SSD_INNER // SSD_GROUPS), F32),
            pltpu.VMEM((LANES, SSD_INNER), BF16),
            pltpu.VMEM((LANES, SSD_HEADS * LANES), BF16),
            pltpu.VMEM((q, q), BF16),
        ],
        compiler_params=pltpu.CompilerParams(
            dimension_semantics=("arbitrary", "arbitrary"), vmem_limit_bytes=VMEM_LIMIT),
        name="ssd_scan",
    )(xbc, dt, z, conv_w, conv_b, dt_bias, a_log, d_skip, norm_g)


def _bias_kernel(tab_ref, o_ref, *, t):
    kk = lax.broadcasted_iota(jnp.int32, (2 * t, t), 0)
    qq = lax.broadcasted_iota(jnp.int32, (2 * t, t), 1)
    rel = qq - kk + t
    n = jnp.maximum(rel, 0)
    max_exact = REL_BUCKETS // 2
    nf = jnp.maximum(n, 1).astype(F32)
    large = max_exact + (jnp.log(nf / max_exact) / math.log(REL_MAX_DIST / max_exact)
                         * (REL_BUCKETS - max_exact)).astype(jnp.int32)
    large = jnp.minimum(large, REL_BUCKETS - 1)
    bucket = jnp.where(n < max_exact, n, large)
    for h in range(ATTN_HEADS):
        far = tab_ref[REL_BUCKETS - 1, h]
        acc = jnp.zeros((2 * t, t), F32)
        for b in range(REL_BUCKETS - 1):
            acc = jnp.where(bucket == b, (tab_ref[b, h] - far) * LOG2E, acc)
        o_ref[h] = jnp.where(rel >= 0, acc, -jnp.inf)


def _bias_tiles(rel_bias, t):
    return pl.pallas_call(
        functools.partial(_bias_kernel, t=t),
        grid=(1,),
        in_specs=[pl.BlockSpec(memory_space=pltpu.SMEM)],
        out_specs=pl.BlockSpec((ATTN_HEADS, 2 * t, t), lambda i: (0, 0, 0)),
        out_shape=jax.ShapeDtypeStruct((ATTN_HEADS, 2 * t, t), F32),
        compiler_params=pltpu.CompilerParams(
            dimension_semantics=("arbitrary",), vmem_limit_bytes=VMEM_LIMIT),
        name="rel_bias_tiles",
    )(rel_bias)


def _attn_kernel(lq1_ref, lk1_ref, lq2_ref, lk2_ref, subg_ref, q1_ref, q2_ref, k_ref, vt_ref,
                 band_ref, o_ref, m_sc, acc_sc, s_a, s_b, mx_a, mx_b, *, tq, tk, lam_init):
    ndiag = tq // tk
    assert ndiag % 2 == 0
    nq = q1_ref.shape[0] // tq
    q_refs = (q1_ref, q2_ref)
    lam = (jnp.exp(jnp.sum(lq1_ref[...] * lk1_ref[...], axis=-1, keepdims=True))
           - jnp.exp(jnp.sum(lq2_ref[...] * lk2_ref[...], axis=-1, keepdims=True))
           + lam_init)

    def add_bias(buf, key_off, c0=0):
        for mp in range(2):
            for r0 in range(0, tk, LANES):
                for cb in range(c0, tq, LANES):
                    rel0 = cb - r0 - key_off
                    blk = (slice(r0, r0 + LANES), slice(cb, cb + LANES))
                    if rel0 < 0:
                        buf[mp, blk[0], blk[1]] = jnp.full((LANES, LANES), -jnp.inf, F32)
                    elif rel0 == 0:
                        buf[mp, blk[0], blk[1]] += band_ref[LANES:2 * LANES, :]
                    elif rel0 == LANES:
                        buf[mp, blk[0], blk[1]] += band_ref[0:LANES, :]

    def step(qk_args, sm_args):
        if qk_args is not None:
            qi, qj, dst, q_c0, mx_dst = qk_args
            kt = k_ref[pl.ds(pl.multiple_of(qj * tk, tk), tk), :]
        if sm_args is not None:
            src, sj, s_c0, mx_src = sm_args
        for mp in range(2):
            for cb in range(0, tq, MXU_COLS):
                cols = slice(cb, cb + MXU_COLS)
                if qk_args is not None and cb >= q_c0:
                    qt = q_refs[mp][pl.ds(pl.multiple_of(qi * tq + cb, MXU_COLS), MXU_COLS), :]
                    sc = _dot_nt(kt, qt)
                    dst[mp, :, cols] = sc
                    if mx_dst is not None:
                        mx_dst[mp, :, cols] = jnp.max(sc, axis=0, keepdims=True)
                if sm_args is not None and cb >= s_c0:
                    s = src[mp, :, cols]
                    m_prev = m_sc[mp, :, cols]
                    if mx_src is not None:
                        tile_max = mx_src[mp, :, cols]
                    else:
                        tile_max = jnp.max(s, axis=0, keepdims=True)
                    m_new = jnp.maximum(m_prev, tile_max)
                    alpha = jnp.exp2(m_prev - m_new)
                    p = jnp.exp2(s - m_new).astype(BF16)
                    acc_sc[mp, :, cols] = alpha * acc_sc[mp, :, cols] + _dot(vt_ref[sj], p)
                    m_sc[mp, :, cols] = m_new

    step((0, 0, s_a, 0, mx_a), None)

    def q_tile(i, carry):
        base = ndiag * i
        m_sc[...] = jnp.full(m_sc.shape, -jnp.inf, F32)
        acc_sc[...] = jnp.zeros(acc_sc.shape, F32)

        def far_pair(jj, c):
            j = 2 * jj
            step((i, j + 1, s_b, 0, mx_b), (s_a, j, 0, mx_a))
            step((i, j + 2, s_a, 0, mx_a), (s_b, j + 1, 0, mx_b))
            return c

        lax.fori_loop(0, base // 2 - 1, far_pair, 0)

        @pl.when(i > 0)
        def _far_prev():
            step((i, base - 1, s_b, 0, None), (s_a, base - 2, 0, mx_a))
            add_bias(s_b, -tk)
            step((i, base, s_a, 0, None), (s_b, base - 1, 0, None))

        for d in range(0, ndiag, 2):
            add_bias(s_a, d * tk, c0=d * tk)
            step((i, base + d + 1, s_b, (d + 1) * tk, None), (s_a, base + d, d * tk, None))
            add_bias(s_b, (d + 1) * tk, c0=(d + 1) * tk)
            if d + 2 < ndiag:
                nxt = (i, base + d + 2, s_a, (d + 2) * tk, None)
            else:
                nxt = (jnp.minimum(i + 1, nq - 1), 0, s_a, 0, mx_a)
            step(nxt, (s_b, base + d + 1, (d + 1) * tk, None))

        inv1 = 1.0 / acc_sc[0, ATTN_V_DIM:ATTN_V_DIM + 1, :]
        inv2 = lam / acc_sc[1, ATTN_V_DIM:ATTN_V_DIM + 1, :]
        o = (acc_sc[0, 0:ATTN_V_DIM, :] * inv1 - acc_sc[1, 0:ATTN_V_DIM, :] * inv2).T
        o = _rms(o, subg_ref[...]) * (1.0 - lam_init)
        o_ref[pl.ds(pl.multiple_of(i * tq, tq), tq), :] = o.astype(o_ref.dtype)
        return carry

    lax.fori_loop(0, nq, q_tile, 0)


def _attention(q1, q2, k, vt, band, lq1, lk1, lq2, lk2, subg, tq, tk, lam_init):
    bsz, s, _ = q1.shape
    small = lambda shape: pl.BlockSpec(shape, lambda b, h: (0, 0))
    head_cols = pl.BlockSpec((None, s, LANES), lambda b, h: (b, 0, h))
    return pl.pallas_call(
        functools.partial(_attn_kernel, tq=tq, tk=tk, lam_init=lam_init),
        grid=(bsz, ATTN_HEADS),
        in_specs=[
            small((1, ATTN_QK_DIM)), small((1, ATTN_QK_DIM)),
            small((1, ATTN_QK_DIM)), small((1, ATTN_QK_DIM)),
            small((1, ATTN_V_DIM)),
            head_cols, head_cols, head_cols,
            pl.BlockSpec((None, s // tk, None, V_AUG, tk), lambda b, h: (b, 0, h, 0, 0)),
            pl.BlockSpec((None, 2 * LANES, LANES), lambda b, h: (h, 0, 0)),
        ],
        out_specs=head_cols,
        out_shape=jax.ShapeDtypeStruct((bsz, s, ATTN_WIDTH), BF16),
        scratch_shapes=[
            pltpu.VMEM((2, 1, tq), F32),
            pltpu.VMEM((2, V_AUG, tq), F32),
            pltpu.VMEM((2, tk, tq), F32),
            pltpu.VMEM((2, tk, tq), F32),
            pltpu.VMEM((2, 1, tq), F32),
            pltpu.VMEM((2, 1, tq), F32),
        ],
        compiler_params=pltpu.CompilerParams(
            dimension_semantics=("arbitrary", "arbitrary"),
            vmem_limit_bytes=VMEM_LIMIT),
        name="diff_attention",
    )(lq1, lk1, lq2, lk2, subg, q1, q2, k, vt, band)


def _outproj_kernel(x_ref, ys_ref, ya_ref, gt_ref, post_ref, w_ref, o_ref, *, row_chains):
    k_ssd = ys_ref.shape[-1]
    rows = x_ref.shape[0] // row_chains
    for r in range(row_chains):
        sp = slice(r * rows, (r + 1) * rows)
        m = _dot(ys_ref[sp, :], w_ref[0:k_ssd, :]) + _dot(ya_ref[sp, :], w_ref[k_ssd:, :])
        o_ref[sp, :] = x_ref[sp, :] + gt_ref[...] * _rms(m, post_ref[...])


def _outproj(x, y_ssd, y_attn, mod4, post_g, w_out, tm, row_chains):
    bsz, s, d = x.shape

    def tok(width):
        return pl.BlockSpec((None, tm, width), lambda b, t: (b, t, 0))

    return pl.pallas_call(
        functools.partial(_outproj_kernel, row_chains=row_chains),
        grid=(bsz, s // tm),
        in_specs=[tok(d), tok(SSD_INNER), tok(ATTN_WIDTH), _mod_spec(5, d),
                  _const_spec((1, d)), _resident_spec((SSD_INNER + ATTN_WIDTH, d))],
        out_specs=tok(d),
        out_shape=jax.ShapeDtypeStruct((bsz, s, d), F32),
        compiler_params=pltpu.CompilerParams(
            dimension_semantics=("arbitrary", "arbitrary"), vmem_limit_bytes=VMEM_LIMIT),
        name="mixer_out_proj",
    )(x, y_ssd, y_attn, mod4, post_g, w_out)


def _tiles(s):
    return dict(
        tm_ffn=min(1024, s),
        ffn_chains=4,
        tm_proj=min(1024, s),
        proj_chains=2,
        tm_ssd=min(512, s),
        ff_chunk=2816,
        tq_attn=min(2048, s),
        tk_attn=min(512, s // 2),
        tn_mod=1152,
    )


def kernel(x, c, w_ada, b_ada, ffn1_pre_g, ffn1_post_g, ffn1_w_gate, ffn1_w_up, ffn1_w_down, mix_pre_g, mix_post_g, w_in, conv_w, conv_b, dt_bias, a_log, d_skip, ssd_norm_g, lambda_q1, lambda_k1, lambda_q2, lambda_k2, subln_g, w_out, ffn2_pre_g, ffn2_post_g, ffn2_w_gate, ffn2_w_up, ffn2_w_down, rel_bias):
    bsz, s, d = x.shape
    depth = w_ada.shape[0]
    cfg = _tiles(s)
    row = lambda v: v.reshape(1, -1)

    c_pad = jnp.zeros((8, d), F32).at[:bsz].set(c)
    band = _bias_tiles(rel_bias, LANES)

    for l in range(depth):
        mod = _modulation(c_pad, w_ada[l], row(b_ada[l]), cfg["tn_mod"])
        mod4 = mod[:bsz].reshape(bsz, N_MOD, 1, d)

        x = _ffn(x, mod4, 0, row(ffn1_pre_g[l]), row(ffn1_post_g[l]),
                 ffn1_w_gate[l], ffn1_w_up[l], ffn1_w_down[l], cfg["tm_ffn"], cfg["ff_chunk"],
                 cfg["ffn_chains"])

        w = w_in[l]
        dt0 = _SSD_COLS
        w_ssd = w[:, :dt0].astype(BF16)
        w_dt = jnp.pad(w[:, dt0:dt0 + SSD_HEADS], ((0, 0), (0, DT_PAD - SSD_HEADS))).astype(BF16)
        w_attn = w[:, dt0 + SSD_HEADS:].astype(BF16)
        z, xbc, dt, q1, q2, k, vt = _inproj(x, mod4, row(mix_pre_g[l]), w_ssd, w_dt, w_attn,
                                            cfg["tk_attn"])

        pad16 = lambda vec: jnp.zeros((1, DT_PAD), F32).at[0, :SSD_HEADS].set(vec)
        y_ssd = _ssd(xbc, dt, z, conv_w[l], row(conv_b[l]), pad16(dt_bias[l]), pad16(a_log[l]),
                     row(jnp.repeat(d_skip[l], SSD_HEAD_DIM)), row(ssd_norm_g[l]),
                     cfg["tm_ssd"])

        lam_init = 0.8 - 0.6 * math.exp(-0.3 * l)
        y_attn = _attention(q1, q2, k, vt, band, row(lambda_q1[l]), row(lambda_k1[l]),
                            row(lambda_q2[l]), row(lambda_k2[l]), row(subln_g[l]),
                            cfg["tq_attn"], cfg["tk_attn"], lam_init)

        x = _outproj(x, y_ssd, y_attn, mod4, row(mix_post_g[l]), w_out[l].astype(BF16),
                     cfg["tm_proj"], cfg["proj_chains"])

        x = _ffn(x, mod4, 6, row(ffn2_pre_g[l]), row(ffn2_post_g[l]),
                 ffn2_w_gate[l], ffn2_w_up[l], ffn2_w_down[l], cfg["tm_ffn"], cfg["ff_chunk"],
                 cfg["ffn_chains"])
    return x
```
